```python
import math
import jax, jax.numpy as jnp
from jax import lax
import numpy as np

D_MODEL = 1024
BATCH = 8
SEQ = 2048
DEPTH = 1
DEC_BATCH = 128
DEC_SEQ = 8
PAST_LEN = 16384
PAGE_SIZE = 128

N_META = 16
DN_DK = 128
DN_DV = 128
DN_HEADS = D_MODEL // DN_DV
DN_CONV = 4
DN_CHUNK = 64
DN_QKV = DN_HEADS * (2 * DN_DK + DN_DV)
DN_Z = DN_HEADS * DN_DV
RW_HD = 64
RW_HEADS = D_MODEL // RW_HD
RW_W = RW_HEADS * RW_HD
RW_DECAY_LORA = 64
RW_A_LORA = 64
RW_G_LORA = 128
RW_SHIFT_W = 3 * RW_W + RW_DECAY_LORA + RW_A_LORA + RW_G_LORA
RW_GN_EPS = 64e-5
PROJ_W = DN_QKV + DN_Z + 2 * DN_HEADS + RW_SHIFT_W + 2 * D_MODEL
D_FF = 4 * D_MODEL
NORM_EPS = 1e-6

kernel_name = 'gdn_rwkv7_parallel_hybrid_step'


def split_points(sizes):
    pts, acc = [], 0
    for s in sizes[:-1]:
        acc += s
        pts.append(acc)
    return pts


def rmsnorm(x, g):
    xf = x.astype(jnp.float32)
    y = xf * lax.rsqrt(jnp.mean(xf * xf, axis=-1, keepdims=True) + NORM_EPS)
    return (y * g.astype(jnp.float32)).astype(x.dtype)


def l2norm(x):
    return x * lax.rsqrt(jnp.sum(x * x, axis=-1, keepdims=True) + 1e-6)


def causal_depthwise_conv(x, buf, w):
    t_len = x.shape[1]
    width = w.shape[0]
    w = w.astype(x.dtype)
    xp = jnp.concatenate([buf.astype(x.dtype), x], axis=1)
    out = xp[:, 0:t_len] * w[0]
    for j in range(1, width):
        out = out + xp[:, j:j + t_len] * w[j]
    return out, xp[:, -(width - 1):]


def gated_delta_chunked(q, k, v, g, beta, S0, chunk):
    nb, t_len, nh, dk = q.shape
    dv = v.shape[-1]
    n = t_len // chunk

    def blocks(t):
        t = t.reshape((nb, n, chunk) + t.shape[2:])
        return jnp.swapaxes(jnp.moveaxis(t, 1, 0), 2, 3)

    qc, kc, vc, gc, bc = blocks(q), blocks(k), blocks(v), blocks(g), blocks(beta)
    G = jnp.cumsum(gc, axis=-1)
    idx = jnp.arange(chunk)
    incl = idx[:, None] >= idx[None, :]
    strict = idx[:, None] > idx[None, :]
    decay = jnp.exp(jnp.where(incl, G[..., :, None] - G[..., None, :], -jnp.inf))
    kb = kc * bc[..., None]
    A = jnp.where(strict, jnp.einsum('nbhid,nbhjd->nbhij', kb, kc) * decay, 0.0)
    Tm = A + jnp.eye(chunk, dtype=A.dtype)
    rhs = jnp.concatenate([vc * bc[..., None], kb * jnp.exp(G)[..., None]], axis=-1)
    sol = lax.linalg.triangular_solve(Tm, rhs, left_side=True, lower=True, unit_diagonal=True)
    U, W = sol[..., :dv], sol[..., dv:]
    Aqk = jnp.einsum('nbhid,nbhjd->nbhij', qc, kc) * decay
    qg = qc * jnp.exp(G)[..., None]
    kd = kc * jnp.exp(G[..., -1:] - G)[..., None]
    gl = jnp.exp(G[..., -1])

    def step(S, inp):
        qg_i, kd_i, U_i, W_i, Aqk_i, gl_i = inp
        u = U_i - jnp.einsum('bhck,bhkv->bhcv', W_i, S)
        o = jnp.einsum('bhck,bhkv->bhcv', qg_i, S) + jnp.einsum('bhij,bhjv->bhiv', Aqk_i, u)
        S = S * gl_i[..., None, None] + jnp.einsum('bhck,bhcv->bhkv', kd_i, u)
        return S, o

    S, o = lax.scan(step, S0, (qg, kd, U, W, Aqk, gl))
    o = jnp.moveaxis(jnp.swapaxes(o, 2, 3), 0, 1).reshape(nb, t_len, nh, dv)
    return o, S


def rwkv7_step(S, inp):
    r, w, k, v, kk, a = inp
    sa = jnp.einsum('bhvk,bhk->bhv', S, -kk)
    S = S * w[:, :, None, :] + sa[..., None] * (kk * a)[:, :, None, :] + v[..., None] * k[:, :, None, :]
    return S, jnp.einsum('bhvk,bhk->bhv', S, r)


def hybrid_layer(x, segments, conv_buf, dn_S, rw_prev, rw_S, weights):
    (g_mix_norm, w_in, dn_conv_w, dn_a_log, dn_dt_bias, dn_norm_w,
     rw_mu, rw_w0, rw_w2, rw_a0, rw_a2, rw_g2, rw_k_k, rw_k_a, rw_r_k,
     rw_gn_w, rw_gn_b, w_out, g_ffn_norm, w_ff1, w_ff2) = weights
    f32 = jnp.float32
    nb, t_len, _ = x.shape
    h = rmsnorm(x, g_mix_norm)
    proj = h @ w_in.astype(x.dtype)
    p_qkv, p_z, p_a, p_b, p_rw, gate_a, gate_b = jnp.split(
        proj, split_points((DN_QKV, DN_Z, DN_HEADS, DN_HEADS, RW_SHIFT_W, D_MODEL, D_MODEL)), axis=-1)

    qkv, conv_new = causal_depthwise_conv(p_qkv, conv_buf, dn_conv_w)
    qkv = jax.nn.silu(qkv.astype(f32))
    q, k, v = jnp.split(qkv, split_points((DN_HEADS * DN_DK, DN_HEADS * DN_DK, DN_HEADS * DN_DV)), axis=-1)
    q = l2norm(q.reshape(nb, t_len, DN_HEADS, DN_DK)) * (DN_DK ** -0.5)
    k = l2norm(k.reshape(nb, t_len, DN_HEADS, DN_DK))
    v = v.reshape(nb, t_len, DN_HEADS, DN_DV)
    beta = jax.nn.sigmoid(p_b.astype(f32))
    g = -jnp.exp(dn_a_log.astype(f32)) * jax.nn.softplus(p_a.astype(f32) + dn_dt_bias.astype(f32))
    S = dn_S.astype(f32)
    outs = []
    start = 0
    for seg_len, chunk in segments:
        sl = slice(start, start + seg_len)
        o_seg, S = gated_delta_chunked(q[:, sl], k[:, sl], v[:, sl], g[:, sl], beta[:, sl], S, chunk)
        outs.append(o_seg)
        start += seg_len
    o_dn = jnp.concatenate(outs, axis=1)
    z = p_z.astype(f32).reshape(nb, t_len, DN_HEADS, DN_DV)
    o_dn = (o_dn * lax.rsqrt(jnp.mean(o_dn * o_dn, axis=-1, keepdims=True) + NORM_EPS)
            * dn_norm_w.astype(f32) * jax.nn.silu(z)).reshape(nb, t_len, DN_HEADS * DN_DV)

    prev = jnp.concatenate([rw_prev[:, None].astype(p_rw.dtype), p_rw[:, :-1]], axis=1)
    xm = (p_rw + (prev - p_rw) * rw_mu.astype(p_rw.dtype)).astype(f32)
    rw_prev_new = p_rw[:, -1]
    r, kr, vr, wl, al, gl = jnp.split(
        xm, split_points((RW_W, RW_W, RW_W, RW_DECAY_LORA, RW_A_LORA, RW_G_LORA)), axis=-1)
    w_log = -jax.nn.softplus(-(rw_w0.astype(f32) + jnp.tanh(wl) @ rw_w2.astype(f32))) - 0.5
    decay = jnp.exp(-jnp.exp(w_log))
    a = jax.nn.sigmoid(rw_a0.astype(f32) + al @ rw_a2.astype(f32))
    gate = jax.nn.sigmoid(gl) @ rw_g2.astype(f32)

    def heads(t):
        return t.reshape(nb, t_len, RW_HEADS, RW_HD)

    kk = l2norm(heads(kr * rw_k_k.astype(f32)))
    kr = kr * (1.0 + (a - 1.0) * rw_k_a.astype(f32))
    r, decay, kr, vr, a = heads(r), heads(decay), heads(kr), heads(vr), heads(a)
    S_rw, o_rw = lax.scan(rwkv7_step, rw_S.astype(f32),
                          (jnp.swapaxes(r, 0, 1), jnp.swapaxes(decay, 0, 1), jnp.swapaxes(kr, 0, 1),
                           jnp.swapaxes(vr, 0, 1), jnp.swapaxes(kk, 0, 1), jnp.swapaxes(a, 0, 1)))
    o_rw = jnp.swapaxes(o_rw, 0, 1)
    mu = jnp.mean(o_rw, axis=-1, keepdims=True)
    var = jnp.mean(jnp.square(o_rw - mu), axis=-1, keepdims=True)
    o_rw = ((o_rw - mu) * lax.rsqrt(var + RW_GN_EPS) * rw_gn_w.astype(f32).reshape(RW_HEADS, RW_HD)
            + rw_gn_b.astype(f32).reshape(RW_HEADS, RW_HD))
    o_rw = o_rw + jnp.sum(r * kr * rw_r_k.astype(f32), axis=-1, keepdims=True) * vr
    o_rw = o_rw.reshape(nb, t_len, RW_W) * gate

    mix = jax.nn.sigmoid(gate_a.astype(f32)) * o_dn + jax.nn.sigmoid(gate_b.astype(f32)) * o_rw
    x = x + mix.astype(x.dtype) @ w_out.astype(x.dtype)

    h2 = rmsnorm(x, g_ffn_norm)
    x = x + jnp.square(jax.nn.relu(h2 @ w_ff1.astype(x.dtype))) @ w_ff2.astype(x.dtype)
    return x, (conv_new, S, rw_prev_new, S_rw)


def run_stack(x, segments, conv_bufs, dn_states, rw_prevs, rw_states, layer_weights, g_final):
    new_conv, new_dn, new_prev, new_rw = [], [], [], []
    for layer in range(DEPTH):
        x, (c, s, p, r) = hybrid_layer(x, segments, conv_bufs[layer], dn_states[layer], rw_prevs[layer],
                                       rw_states[layer], tuple(w[layer] for w in layer_weights))
        new_conv.append(c)
        new_dn.append(s)
        new_prev.append(p)
        new_rw.append(r)
    return (rmsnorm(x, g_final), jnp.stack(new_conv), jnp.stack(new_dn),
            jnp.stack(new_prev), jnp.stack(new_rw))


def setup_inputs(seed: int = 0) -> dict:
    key = jax.random.key(seed)
    ks = jax.random.split(key, 32)
    f32 = jnp.float32
    L = DEPTH

    def nrm(k, shape, scale):
        return jax.random.normal(k, shape, f32) * scale

    dt = jnp.exp(jax.random.uniform(ks[11], (L, DN_HEADS), f32, math.log(1e-3), math.log(1e-1)))
    return {
        'x_prompt': nrm(ks[0], (BATCH, SEQ, D_MODEL), 1.0),
        'x_sample': nrm(ks[1], (DEC_BATCH, DEC_SEQ, D_MODEL), 1.0),
        'state_dn_conv': nrm(ks[2], (L, DEC_BATCH, DN_CONV - 1, DN_QKV), 1.0),
        'state_dn': nrm(ks[3], (L, DEC_BATCH, DN_HEADS, DN_DK, DN_DV), 0.1),
        'state_rw_shift': nrm(ks[4], (L, DEC_BATCH, RW_SHIFT_W), 1.0),
        'state_rw': nrm(ks[5], (L, DEC_BATCH, RW_HEADS, RW_HD, RW_HD), 0.1),
        'meta_tokens': nrm(ks[6], (N_META, D_MODEL), 1.0),
        'g_mix_norm': 1.0 + nrm(ks[7], (L, D_MODEL), 0.02),
        'w_in': nrm(ks[8], (L, D_MODEL, PROJ_W), D_MODEL ** -0.5),
        'dn_conv_w': nrm(ks[9], (L, DN_CONV, DN_QKV), DN_CONV ** -0.5),
        'dn_a_log': jnp.log(jax.random.uniform(ks[10], (L, DN_HEADS), f32, 1.0, 16.0)),
        'dn_dt_bias': dt + jnp.log(-jnp.expm1(-dt)),
        'dn_norm_w': 1.0 + nrm(ks[12], (L, DN_DV), 0.02),
        'rw_mu': jax.random.uniform(ks[13], (L, RW_SHIFT_W), f32),
        'rw_w0': jax.random.uniform(ks[14], (L, RW_W), f32, -4.0, 1.0),
        'rw_w2': nrm(ks[15], (L, RW_DECAY_LORA, RW_W), 0.1 * RW_DECAY_LORA ** -0.5),
        'rw_a0': nrm(ks[16], (L, RW_W), 0.1),
        'rw_a2': nrm(ks[17], (L, RW_A_LORA, RW_W), 0.1 * RW_A_LORA ** -0.5),
        'rw_g2': nrm(ks[18], (L, RW_G_LORA, RW_W), RW_G_LORA ** -0.5),
        'rw_k_k': 0.85 + nrm(ks[19], (L, RW_W), 0.02),
        'rw_k_a': 1.0 + nrm(ks[20], (L, RW_W), 0.02),
        'rw_r_k': nrm(ks[21], (L, RW_HEADS, RW_HD), 0.1),
        'rw_gn_w': 1.0 + nrm(ks[22], (L, RW_W), 0.02),
        'rw_gn_b': nrm(ks[23], (L, RW_W), 0.02),
        'w_out': nrm(ks[24], (L, D_MODEL, D_MODEL), D_MODEL ** -0.5),
        'g_ffn_norm': 1.0 + nrm(ks[25], (L, D_MODEL), 0.02),
        'w_ff1': nrm(ks[26], (L, D_MODEL, D_FF), D_MODEL ** -0.5),
        'w_ff2': nrm(ks[27], (L, D_FF, D_MODEL), D_FF ** -0.5),
        'g_final': 1.0 + nrm(ks[28], (D_MODEL,), 0.02),
    }


def reference(x_prompt, x_sample, state_dn_conv, state_dn, state_rw_shift, state_rw, meta_tokens,
              g_mix_norm, w_in, dn_conv_w, dn_a_log, dn_dt_bias, dn_norm_w,
              rw_mu, rw_w0, rw_w2, rw_a0, rw_a2, rw_g2, rw_k_k, rw_k_a, rw_r_k, rw_gn_w, rw_gn_b,
              w_out, g_ffn_norm, w_ff1, w_ff2, g_final):
    layer_weights = (g_mix_norm, w_in, dn_conv_w, dn_a_log, dn_dt_bias, dn_norm_w,
                     rw_mu, rw_w0, rw_w2, rw_a0, rw_a2, rw_g2, rw_k_k, rw_k_a, rw_r_k,
                     rw_gn_w, rw_gn_b, w_out, g_ffn_norm, w_ff1, w_ff2)
    f32 = jnp.float32

    nbp, seq_len, _ = x_prompt.shape
    meta = jnp.broadcast_to(meta_tokens.astype(x_prompt.dtype)[None], (nbp, N_META, D_MODEL))
    xp = jnp.concatenate([meta, x_prompt], axis=1)
    seg_p = ((N_META, N_META), (seq_len, math.gcd(seq_len, DN_CHUNK)))
    yp, conv_p, dn_p, shift_p, rw_p = run_stack(
        xp, seg_p,
        jnp.zeros((DEPTH, nbp, DN_CONV - 1, DN_QKV), x_prompt.dtype),
        jnp.zeros((DEPTH, nbp, DN_HEADS, DN_DK, DN_DV), f32),
        jnp.zeros((DEPTH, nbp, RW_SHIFT_W), x_prompt.dtype),
        jnp.zeros((DEPTH, nbp, RW_HEADS, RW_HD, RW_HD), f32),
        layer_weights, g_final)
    y_prompt = yp[:, N_META:]

    dec_len = x_sample.shape[1]
    seg_s = ((dec_len, math.gcd(dec_len, DN_CHUNK)),)
    y_sample, conv_s, dn_s, shift_s, rw_s = run_stack(
        x_sample, seg_s, state_dn_conv, state_dn, state_rw_shift, state_rw, layer_weights, g_final)

    return (y_prompt, y_sample, conv_p, dn_p, shift_p, rw_p, conv_s, dn_s, shift_s, rw_s)
```

```python
import functools
import math

import jax
import jax.numpy as jnp
from jax import lax
from jax.experimental import pallas as pl
from jax.experimental.pallas import tpu as pltpu

F32 = jnp.float32
BF16 = jnp.bfloat16

D_MODEL = 1024
N_META = 16
DN_HEADS = 8
DN_D = 128
DN_QKV = 3 * DN_HEADS * DN_D
RW_HEADS = 16
RW_HD = 64
RW_PAIRS = RW_HEADS // 2
RW_W = RW_HEADS * RW_HD
RW_LORA = 256
RW_SHIFT_W = 3 * RW_W + RW_LORA
D_FF = 4 * D_MODEL
NORM_EPS = 1e-6
RW_GN_EPS = 64e-5
LANES = 128
SUBLANES = 8
INV_BLOCK = 16

COL_QKV = 0
COL_Z = DN_QKV
COL_GATE = COL_Z + D_MODEL
COL_RW = COL_GATE + 2 * D_MODEL
P_W = COL_RW + RW_SHIFT_W
P_HALF = P_W // 2

VMEM_LIMIT = 56 * 1024 * 1024


def _sigmoid(x):
    return 1.0 / (1.0 + jnp.exp(-x))


def _silu(x):
    return x * _sigmoid(x)


def _softplus(x):
    return jnp.maximum(x, 0.0) + jnp.log(1.0 + jnp.exp(-jnp.abs(x)))


def _dot(a, b):
    return jnp.dot(a.astype(BF16), b.astype(BF16), preferred_element_type=F32)


def _dot_nt(a, b):
    return lax.dot_general(a.astype(BF16), b.astype(BF16), (((1,), (1,)), ((), ())),
                           preferred_element_type=F32)


def _dot_tn(a, b):
    return lax.dot_general(a.astype(BF16), b.astype(BF16), (((0,), (0,)), ((), ())),
                           preferred_element_type=F32)


def _split2(x):
    hi = x.astype(BF16)
    lo = (x - hi.astype(F32)).astype(BF16)
    return hi, lo


def _dot_hp(a, b):
    ah, al = _split2(a)
    bh, bl = _split2(b)
    return (jnp.dot(ah, bh, preferred_element_type=F32)
            + (jnp.dot(ah, bl, preferred_element_type=F32)
               + jnp.dot(al, bh, preferred_element_type=F32)))


def _dot_mask(mask_bf16, x):
    h1 = x.astype(BF16)
    r1 = x - h1.astype(F32)
    h2 = r1.astype(BF16)
    h3 = (r1 - h2.astype(F32)).astype(BF16)
    return (jnp.dot(mask_bf16, h1, preferred_element_type=F32)
            + (jnp.dot(mask_bf16, h2, preferred_element_type=F32)
               + jnp.dot(mask_bf16, h3, preferred_element_type=F32)))


def _iota2(n, m):
    return (lax.broadcasted_iota(jnp.int32, (n, m), 0),
            lax.broadcasted_iota(jnp.int32, (n, m), 1))


def _group(idx, size):
    return lax.shift_right_logical(idx, int(math.log2(size)))


def _neumann_inverse(a, eye, nil):
    inv = eye - a
    power = a
    k = 2
    while k < nil:
        power = _dot_hp(power, power)
        inv = inv + _dot_hp(inv, power)
        k *= 2
    return inv


def _unit_lower_inverse(a, ri, ci, tc):
    eye = jnp.where(ri == ci, 1.0, 0.0).astype(F32)
    if tc <= INV_BLOCK:
        return _neumann_inverse(a, eye, tc)
    assert tc // INV_BLOCK <= 4
    diag = _group(ri, INV_BLOCK) == _group(ci, INV_BLOCK)
    d = jnp.where(diag, a, 0.0)
    low = a - d
    dinv = _neumann_inverse(d, eye, INV_BLOCK)
    b = _dot_hp(dinv, low)
    x = dinv + _dot_hp(_dot_hp(b, b), dinv)
    return x - _dot_hp(b, x)


def _proj_kernel(x_ref, g_ref, w_ref, wab_ref, alog_ref, dtb_ref, p_ref, gb_ref, *, tm, rc):
    x = x_ref[...]
    h = x * lax.rsqrt(jnp.mean(x * x, axis=-1, keepdims=True) + NORM_EPS) * g_ref[...]
    hb = h.astype(BF16)
    for c0 in range(0, P_HALF, 512):
        cw = min(512, P_HALF - c0)
        p_ref[:, c0:c0 + cw] = jnp.dot(hb, w_ref[:, c0:c0 + cw], preferred_element_type=F32)
    @pl.when(pl.program_id(0) == 0)
    def _():
        for c in range(tm // rc):
            ab = lax.dot_general(wab_ref[...], hb[c * rc:(c + 1) * rc], (((1,), (1,)), ((), ())),
                                 preferred_element_type=F32)
            g = -jnp.exp(alog_ref[...]) * _softplus(ab[0:DN_HEADS] + dtb_ref[...])
            gb_ref[c, 0:DN_HEADS, :] = g
            gb_ref[c, DN_HEADS:2 * DN_HEADS, :] = _sigmoid(ab[DN_HEADS:2 * DN_HEADS])


def _proj(x2d, g_norm, w_main, w_ab_t, a_log, dt_bias, *, tm, rc):
    n = x2d.shape[0]
    assert n % tm == 0 and tm % rc == 0
    return pl.pallas_call(
        functools.partial(_proj_kernel, tm=tm, rc=rc),
        out_shape=(jax.ShapeDtypeStruct((n, P_W), F32),
                   jax.ShapeDtypeStruct((n // rc, 2 * DN_HEADS, rc), F32)),
        grid=(2, n // tm),
        in_specs=[
            pl.BlockSpec((tm, D_MODEL), lambda j, i: (i, 0)),
            pl.BlockSpec((1, D_MODEL), lambda j, i: (0, 0)),
            pl.BlockSpec((D_MODEL, P_HALF), lambda j, i: (0, j)),
            pl.BlockSpec((2 * DN_HEADS, D_MODEL), lambda j, i: (0, 0)),
            pl.BlockSpec((DN_HEADS, 1), lambda j, i: (0, 0)),
            pl.BlockSpec((DN_HEADS, 1), lambda j, i: (0, 0)),
        ],
        out_specs=(
            pl.BlockSpec((tm, P_HALF), lambda j, i: (i, j)),
            pl.BlockSpec((tm // rc, 2 * DN_HEADS, rc),
                         lambda j, i: (i * (1 - j) + (n // tm - 1) * j, 0, 0)),
        ),
        compiler_params=pltpu.CompilerParams(
            dimension_semantics=("arbitrary", "arbitrary"), vmem_limit_bytes=VMEM_LIMIT),
        name="proj",
    )(x2d, g_norm, w_main, w_ab_t, a_log, dt_bias)


def _dn_kernel(q_ref, k_ref, v_ref, z_ref, gb_ref, cbq_ref, cbk_ref, cbv_ref,
               cwq_ref, cwk_ref, cwv_ref, nw_ref, s0_ref,
               o_ref, sout_ref, xbq, xbk, xbv, s_scr, *, sb, tc, nt):
    h = pl.program_id(1)
    t = pl.program_id(2)
    r = sb * tc

    @pl.when(t == 0)
    def _():
        s_scr[...] = s0_ref[:, 0]
        xbq[:, 0:SUBLANES, :] = cbq_ref[...]
        xbk[:, 0:SUBLANES, :] = cbk_ref[...]
        xbv[:, 0:SUBLANES, :] = cbv_ref[...]

    @pl.when(t > 0)
    def _():
        xbq[:, 0:SUBLANES, :] = xbq[:, tc:tc + SUBLANES, :]
        xbk[:, 0:SUBLANES, :] = xbk[:, tc:tc + SUBLANES, :]
        xbv[:, 0:SUBLANES, :] = xbv[:, tc:tc + SUBLANES, :]

    def conv(x_ref, xb, cw_ref):
        xb[:, SUBLANES:, :] = x_ref[...].reshape(sb, tc, DN_D)
        acc = xb[:, 5:5 + tc, :] * cw_ref[0:1, :]
        acc = acc + xb[:, 6:6 + tc, :] * cw_ref[1:2, :]
        acc = acc + xb[:, 7:7 + tc, :] * cw_ref[2:3, :]
        acc = acc + xb[:, 8:8 + tc, :] * cw_ref[3:4, :]
        return _silu(acc).reshape(r, DN_D)

    q = conv(q_ref, xbq, cwq_ref)
    k = conv(k_ref, xbk, cwk_ref)
    v = conv(v_ref, xbv, cwv_ref)
    q = q * lax.rsqrt(jnp.sum(q * q, axis=-1, keepdims=True) + 1e-6) * (DN_D ** -0.5)
    k = k * lax.rsqrt(jnp.sum(k * k, axis=-1, keepdims=True) + 1e-6)

    g_row = gb_ref[0, pl.ds(h, 1), :]
    b_row = gb_ref[0, pl.ds(DN_HEADS + h, 1), :]
    ri, ci = _iota2(r, r)
    same = _group(ri, tc) == _group(ci, tc)
    incl = same & (ri >= ci)
    strict = same & (ri > ci)
    eye = ri == ci
    g_col = jnp.sum(jnp.where(incl, g_row, 0.0), axis=1, keepdims=True)
    g_cum_row = jnp.sum(jnp.where(eye, g_col, 0.0), axis=0, keepdims=True)
    g_tot = jnp.sum(jnp.where(same, g_row, 0.0), axis=1, keepdims=True)
    b_col = jnp.sum(jnp.where(eye, b_row, 0.0), axis=1, keepdims=True)
    decay = jnp.where(incl, jnp.exp(jnp.minimum(g_col - g_cum_row, 0.0)), 0.0)
    e_g = jnp.exp(g_col)
    e_rest = jnp.exp(g_tot - g_col)

    kb = k * b_col
    a = jnp.where(strict, _dot_nt(kb, k) * decay, 0.0)
    t_inv = _unit_lower_inverse(a, ri, ci, tc)
    rhs = jnp.concatenate([v * b_col, kb * e_g], axis=1)
    sol = _dot_hp(t_inv, rhs)
    u_all = sol[:, :DN_D]
    w_all = sol[:, DN_D:]
    aqk = _dot_nt(q, k) * decay
    qg = q * e_g
    kd = k * e_rest

    us, qs = [], []
    for s in range(sb):
        rows = slice(s * tc, (s + 1) * tc)
        ws = _dot(jnp.concatenate([w_all[rows], qg[rows]], axis=0), s_scr[s])
        us.append(u_all[rows] - ws[:tc])
        qs.append(ws[tc:])
    u = us[0] if sb == 1 else jnp.concatenate(us, axis=0)
    q_s = qs[0] if sb == 1 else jnp.concatenate(qs, axis=0)
    o = q_s + _dot(aqk, u)
    for s in range(sb):
        rows = slice(s * tc, (s + 1) * tc)
        gl = jnp.exp(g_tot[s * tc:s * tc + 1, :])
        s_scr[s] = s_scr[s] * gl + _dot_tn(kd[rows], u[rows])

    z = z_ref[...]
    o = o * lax.rsqrt(jnp.mean(o * o, axis=-1, keepdims=True) + NORM_EPS) * nw_ref[...] * _silu(z)
    o_ref[...] = o

    @pl.when(t == nt - 1)
    def _():
        sout_ref[:, 0] = s_scr[...]


def _dn(p, gb, conv_buf8, conv_w, norm_w, s0, *, nb, seq, sb, tc):
    nt = seq // tc
    r = sb * tc
    assert seq % tc == 0 and nb % sb == 0 and (nt == 1 or sb == 1)
    bcast = s0.shape[0] == 1
    assert bcast == (conv_buf8.shape[0] == 1) and (not bcast or sb == 1)
    qb, kb_, vb, zb = 0, DN_HEADS, 2 * DN_HEADS, COL_Z // DN_D

    def row_map(off):
        return lambda i, h, t: (i * nt + t, off + h)

    def st_idx(i):
        return 0 if bcast else i

    return pl.pallas_call(
        functools.partial(_dn_kernel, sb=sb, tc=tc, nt=nt),
        out_shape=(jax.ShapeDtypeStruct((nb * seq, DN_HEADS * DN_D), F32),
                   jax.ShapeDtypeStruct((nb, DN_HEADS, DN_D, DN_D), F32)),
        grid=(nb // sb, DN_HEADS, nt),
        in_specs=[
            pl.BlockSpec((r, DN_D), row_map(qb)),
            pl.BlockSpec((r, DN_D), row_map(kb_)),
            pl.BlockSpec((r, DN_D), row_map(vb)),
            pl.BlockSpec((r, DN_D), row_map(zb)),
            pl.BlockSpec((1, 2 * DN_HEADS, r), lambda i, h, t: (i * nt + t, 0, 0)),
            pl.BlockSpec((sb, SUBLANES, DN_D), lambda i, h, t: (st_idx(i), 0, qb + h)),
            pl.BlockSpec((sb, SUBLANES, DN_D), lambda i, h, t: (st_idx(i), 0, kb_ + h)),
            pl.BlockSpec((sb, SUBLANES, DN_D), lambda i, h, t: (st_idx(i), 0, vb + h)),
            pl.BlockSpec((4, DN_D), lambda i, h, t: (0, qb + h)),
            pl.BlockSpec((4, DN_D), lambda i, h, t: (0, kb_ + h)),
            pl.BlockSpec((4, DN_D), lambda i, h, t: (0, vb + h)),
            pl.BlockSpec((1, DN_D), lambda i, h, t: (0, 0)),
            pl.BlockSpec((sb, 1, DN_D, DN_D), lambda i, h, t: (st_idx(i), h, 0, 0)),
        ],
        out_specs=(
            pl.BlockSpec((r, DN_D), lambda i, h, t: (i * nt + t, h)),
            pl.BlockSpec((sb, 1, DN_D, DN_D), lambda i, h, t: (i, h, 0, 0)),
        ),
        scratch_shapes=[
            pltpu.VMEM((sb, tc + SUBLANES, DN_D), F32),
            pltpu.VMEM((sb, tc + SUBLANES, DN_D), F32),
            pltpu.VMEM((sb, tc + SUBLANES, DN_D), F32),
            pltpu.VMEM((sb, DN_D, DN_D), F32),
        ],
        compiler_params=pltpu.CompilerParams(
            dimension_semantics=("arbitrary", "arbitrary", "arbitrary"), vmem_limit_bytes=VMEM_LIMIT),
        name="dn",
    )(p, p, p, p, gb, conv_buf8, conv_buf8, conv_buf8, conv_w, conv_w, conv_w, norm_w, s0)


def _rw_kernel(r_ref, k_ref, v_ref, l_ref, prev_ref, mu_ref, w0_ref, a0_ref, kkw_ref, kaw_ref, rkw_ref,
               gnw_ref, gnb_ref, w2_ref, a2_ref, g2_ref, s0_ref,
               o_ref, sout_ref,
               xbr, xbk, xbv, xbl, c_r, c_lw, c_kk, c_k2, c_v, c_a, c_g, s_scr, *, sb, tc, nt):
    t = pl.program_id(1)
    r = sb * tc
    half = lax.broadcasted_iota(jnp.int32, (1, LANES), 1) < RW_HD
    ri128, ci128 = _iota2(LANES, LANES)
    bd_mask = _group(ri128, RW_HD) == _group(ci128, RW_HD)
    bd_ones = jnp.where(bd_mask, 1.0, 0.0).astype(BF16)

    def seg_sum(x):
        hi, lo = _split2(x)
        return (jnp.dot(hi, bd_ones, preferred_element_type=F32)
                + jnp.dot(lo, bd_ones, preferred_element_type=F32))

    @pl.when(t == 0)
    def _():
        xbr[:, 0:SUBLANES, :] = prev_ref[:, :, 0:RW_W]
        xbk[:, 0:SUBLANES, :] = prev_ref[:, :, RW_W:2 * RW_W]
        xbv[:, 0:SUBLANES, :] = prev_ref[:, :, 2 * RW_W:3 * RW_W]
        xbl[:, 0:SUBLANES, :] = prev_ref[:, :, 3 * RW_W:]
        zero = jnp.zeros((RW_HD, RW_HD), F32)
        for s in range(sb):
            for p in range(RW_PAIRS):
                top = jnp.concatenate([s0_ref[s, 2 * p], zero], axis=1)
                bot = jnp.concatenate([zero, s0_ref[s, 2 * p + 1]], axis=1)
                s_scr[s, p] = jnp.concatenate([top, bot], axis=0)

    @pl.when(t > 0)
    def _():
        for xb in (xbr, xbk, xbv, xbl):
            xb[:, 0:SUBLANES, :] = xb[:, tc:tc + SUBLANES, :]

    def shift_mix(x_ref, xb, lo, width):
        xb[:, SUBLANES:, :] = x_ref[...].reshape(sb, tc, width)
        x = xb[:, SUBLANES:, :]
        prev = xb[:, SUBLANES - 1:SUBLANES - 1 + tc, :]
        return (x + (prev - x) * mu_ref[:, lo:lo + width]).reshape(r, width)

    xm_r = shift_mix(r_ref, xbr, 0, RW_W)
    xm_k = shift_mix(k_ref, xbk, RW_W, RW_W)
    xm_v = shift_mix(v_ref, xbv, 2 * RW_W, RW_W)
    xm_l = shift_mix(l_ref, xbl, 3 * RW_W, RW_LORA)
    lw = _dot(jnp.tanh(xm_l), w2_ref[...])
    la = _dot(xm_l, a2_ref[...])
    lg = _dot(_sigmoid(xm_l), g2_ref[...])
    w_log = -_softplus(-(w0_ref[...] + lw)) - 0.5
    a_all = _sigmoid(a0_ref[...] + la)
    c_r[...] = xm_r
    c_lw[...] = -jnp.exp(w_log)
    c_kk[...] = xm_k * kkw_ref[...]
    c_k2[...] = xm_k * (1.0 + (a_all - 1.0) * kaw_ref[...])
    c_v[...] = xm_v
    c_a[...] = a_all
    c_g[...] = lg

    ri, ci = _iota2(r, r)
    cum_mask = jnp.where((_group(ri, tc) == _group(ci, tc)) & (ri >= ci), 1.0, 0.0).astype(BF16)
    r2 = 2 * r
    ri2, ci2 = _iota2(r2, r2)
    same2 = _group(ri2, tc) == _group(ci2, tc)
    incl2 = same2 & (ri2 >= ci2)
    strict2 = same2 & (ri2 > ci2)

    def stack(x):
        return jnp.concatenate([jnp.where(half, x, 0.0), jnp.where(half, 0.0, x)], axis=0)

    def unstack(x2):
        return x2[:r] + x2[r:]

    def pair_body(p, carry):
        off = pl.multiple_of(p * LANES, LANES)
        cols = pl.ds(off, LANES)
        rr = c_r[:, cols]
        logw = c_lw[:, cols]
        kk = c_kk[:, cols]
        k2 = c_k2[:, cols]
        vv = c_v[:, cols]
        aa = c_a[:, cols]
        kk = kk * lax.rsqrt(seg_sum(kk * kk) + 1e-6)
        bb = kk * aa
        lcum = _dot_mask(cum_mask, logw)
        ltot = jnp.broadcast_to(lcum.reshape(sb, tc, LANES)[:, tc - 1:tc, :],
                                (sb, tc, LANES)).reshape(r, LANES)
        p_in = jnp.exp(lcum)
        p_inv = jnp.exp(-lcum)
        alpha = kk * jnp.exp(lcum - logw)
        beta = bb * p_inv
        kt = k2 * p_inv
        rt = rr * p_in
        e_rest = jnp.exp(ltot - lcum)
        k_end = k2 * e_rest
        b_end = bb * e_rest

        alpha2, rt2, beta2, kt2, v2 = stack(alpha), stack(rt), stack(beta), stack(kt), stack(vv)
        lmat = jnp.where(strict2, _dot_nt(alpha2, beta2), 0.0)
        lk = jnp.where(strict2, _dot_nt(alpha2, kt2), 0.0)
        rb = jnp.where(incl2, _dot_nt(rt2, beta2), 0.0)
        rk = jnp.where(incl2, _dot_nt(rt2, kt2), 0.0)

        a_s, r_s = [], []
        for s in range(sb):
            rows = slice(s * tc, (s + 1) * tc)
            both = _dot_nt(jnp.concatenate([alpha[rows], rt[rows]], axis=0), s_scr[s, p])
            a_s.append(both[:tc])
            r_s.append(both[tc:])
        alpha_s = a_s[0] if sb == 1 else jnp.concatenate(a_s, axis=0)
        r_state = r_s[0] if sb == 1 else jnp.concatenate(r_s, axis=0)

        rhs2 = stack(alpha_s) + _dot(lk, v2)
        t_inv = _unit_lower_inverse(lmat, ri2, ci2, tc)
        u2 = _dot_hp(t_inv, rhs2)
        o2 = _dot(rk, v2) - _dot(rb, u2)
        o = r_state + unstack(o2)
        u = unstack(u2)
        for s in range(sb):
            rows = slice(s * tc, (s + 1) * tc)
            p_end = p_in[(s + 1) * tc - 1:(s + 1) * tc, :]
            upd = _dot_tn(jnp.concatenate([vv[rows], -u[rows]], axis=0),
                          jnp.concatenate([k_end[rows], b_end[rows]], axis=0))
            s_scr[s, p] = s_scr[s, p] * p_end + jnp.where(bd_mask, upd, 0.0)

        mean = seg_sum(o) * (1.0 / RW_HD)
        d = o - mean
        var = seg_sum(d * d) * (1.0 / RW_HD)
        on = d * lax.rsqrt(var + RW_GN_EPS) * gnw_ref[:, cols] + gnb_ref[:, cols]
        bonus = seg_sum(rr * k2 * rkw_ref[:, cols]) * vv
        o_ref[:, cols] = (on + bonus) * c_g[:, cols]
        return carry

    lax.fori_loop(0, RW_PAIRS, pair_body, 0)

    @pl.when(t == nt - 1)
    def _():
        for s in range(sb):
            for p in range(RW_PAIRS):
                sp = s_scr[s, p]
                sout_ref[s, 2 * p] = sp[0:RW_HD, 0:RW_HD]
                sout_ref[s, 2 * p + 1] = sp[RW_HD:, RW_HD:]


def _rw(p, prev8, weights, s0, *, nb, seq, sb, tc):
    nt = seq // tc
    r = sb * tc
    assert seq % tc == 0 and nb % sb == 0 and (nt == 1 or sb == 1)
    bcast = s0.shape[0] == 1
    assert bcast == (prev8.shape[0] == 1) and (not bcast or sb == 1)
    (mu, w0, a0, kkw, kaw, rkw, gnw, gnb, w2p, a2p, g2p) = weights

    def st_idx(i):
        return 0 if bcast else i

    def vec(width):
        return pl.BlockSpec((1, width), lambda i, t: (0, 0))

    def lora_w():
        return pl.BlockSpec((RW_LORA, RW_W), lambda i, t: (0, 0))

    rcol = COL_RW // RW_W
    return pl.pallas_call(
        functools.partial(_rw_kernel, sb=sb, tc=tc, nt=nt),
        out_shape=(jax.ShapeDtypeStruct((nb * seq, RW_W), F32),
                   jax.ShapeDtypeStruct((nb, RW_HEADS, RW_HD, RW_HD), F32)),
        grid=(nb // sb, nt),
        in_specs=[
            pl.BlockSpec((r, RW_W), lambda i, t: (i * nt + t, rcol)),
            pl.BlockSpec((r, RW_W), lambda i, t: (i * nt + t, rcol + 1)),
            pl.BlockSpec((r, RW_W), lambda i, t: (i * nt + t, rcol + 2)),
            pl.BlockSpec((r, RW_LORA), lambda i, t: (i * nt + t, (COL_RW + 3 * RW_W) // RW_LORA)),
            pl.BlockSpec((sb, SUBLANES, RW_SHIFT_W), lambda i, t: (st_idx(i), 0, 0)),
            vec(RW_SHIFT_W), vec(RW_W), vec(RW_W), vec(RW_W), vec(RW_W), vec(RW_W), vec(RW_W), vec(RW_W),
            lora_w(), lora_w(), lora_w(),
            pl.BlockSpec((sb, RW_HEADS, RW_HD, RW_HD), lambda i, t: (st_idx(i), 0, 0, 0)),
        ],
        out_specs=(
            pl.BlockSpec((r, RW_W), lambda i, t: (i * nt + t, 0)),
            pl.BlockSpec((sb, RW_HEADS, RW_HD, RW_HD), lambda i, t: (i, 0, 0, 0)),
        ),
        scratch_shapes=[
            pltpu.VMEM((sb, tc + SUBLANES, RW_W), F32),
            pltpu.VMEM((sb, tc + SUBLANES, RW_W), F32),
            pltpu.VMEM((sb, tc + SUBLANES, RW_W), F32),
            pltpu.VMEM((sb, tc + SUBLANES, RW_LORA), F32),
        ] + [pltpu.VMEM((r, RW_W), F32)] * 7 + [
            pltpu.VMEM((sb, RW_PAIRS, LANES, LANES), F32),
        ],
        compiler_params=pltpu.CompilerParams(
            dimension_semantics=("arbitrary", "arbitrary"), vmem_limit_bytes=VMEM_LIMIT),
        name="rw",
    )(p, p, p, p, prev8, mu, w0, a0, kkw, kaw, rkw, gnw, gnb, w2p, a2p, g2p, s0)


def _merge_kernel(x_ref, odn_ref, orw_ref, ga_ref, gb_ref, wout_ref, gffn_ref, w1_ref, w2_ref, gfin_ref,
                  y_ref, x1_scr, h2_scr, acc_scr, *, nf):
    f = pl.program_id(1)

    @pl.when(f == 0)
    def _():
        mix = _sigmoid(ga_ref[...]) * odn_ref[...] + _sigmoid(gb_ref[...]) * orw_ref[...]
        x1 = x_ref[...] + jnp.dot(mix.astype(BF16), wout_ref[...], preferred_element_type=F32)
        x1_scr[...] = x1
        h2 = x1 * lax.rsqrt(jnp.mean(x1 * x1, axis=-1, keepdims=True) + NORM_EPS) * gffn_ref[...]
        h2_scr[...] = h2.astype(BF16)
        acc_scr[...] = jnp.zeros_like(acc_scr)

    hid = jnp.dot(h2_scr[...], w1_ref[...], preferred_element_type=F32)
    hid = jnp.square(jnp.maximum(hid, 0.0))
    acc_scr[...] += jnp.dot(hid.astype(BF16), w2_ref[...], preferred_element_type=F32)

    @pl.when(f == nf - 1)
    def _():
        y = x1_scr[...] + acc_scr[...]
        y_ref[...] = y * lax.rsqrt(jnp.mean(y * y, axis=-1, keepdims=True) + NORM_EPS) * gfin_ref[...]


def _merge(x2d, o_dn, o_rw, p, w_out, g_ffn, w1, w2, g_fin, *, tm, tf):
    n = x2d.shape[0]
    nf = D_FF // tf
    assert n % tm == 0 and D_FF % tf == 0
    gcol = COL_GATE // D_MODEL
    row = lambda i, f: (i, 0)
    return pl.pallas_call(
        functools.partial(_merge_kernel, nf=nf),
        out_shape=jax.ShapeDtypeStruct((n, D_MODEL), F32),
        grid=(n // tm, nf),
        in_specs=[
            pl.BlockSpec((tm, D_MODEL), row),
            pl.BlockSpec((tm, D_MODEL), row),
            pl.BlockSpec((tm, D_MODEL), row),
            pl.BlockSpec((tm, D_MODEL), lambda i, f: (i, gcol)),
            pl.BlockSpec((tm, D_MODEL), lambda i, f: (i, gcol + 1)),
            pl.BlockSpec((D_MODEL, D_MODEL), lambda i, f: (0, 0)),
            pl.BlockSpec((1, D_MODEL), lambda i, f: (0, 0)),
            pl.BlockSpec((D_MODEL, tf), lambda i, f: (0, f)),
            pl.BlockSpec((tf, D_MODEL), lambda i, f: (f, 0)),
            pl.BlockSpec((1, D_MODEL), lambda i, f: (0, 0)),
        ],
        out_specs=pl.BlockSpec((tm, D_MODEL), row),
        scratch_shapes=[
            pltpu.VMEM((tm, D_MODEL), F32),
            pltpu.VMEM((tm, D_MODEL), BF16),
            pltpu.VMEM((tm, D_MODEL), F32),
        ],
        compiler_params=pltpu.CompilerParams(
            dimension_semantics=("arbitrary", "arbitrary"), vmem_limit_bytes=VMEM_LIMIT),
        name="merge",
    )(x2d, o_dn, o_rw, p, p, w_out, g_ffn, w1, w2, g_fin)


def _pad_rows_front(x, rows):
    b, n, w = x.shape
    return jnp.concatenate([jnp.zeros((b, rows - n, w), x.dtype), x], axis=1)


def _layer(x, conv_buf, dn_s, rw_prev, rw_s, wts, *, sb, tc, tm, sb_rw=None):
    sb_rw = sb if sb_rw is None else sb_rw
    nb, seq, _ = x.shape
    n = nb * seq
    x2d = x.reshape(n, D_MODEL)
    p, gb = _proj(x2d, wts["g_mix"], wts["w_main"], wts["w_ab_t"], wts["a_log"], wts["dt_bias"],
                  tm=tm, rc=sb * tc)
    o_dn, dn_new = _dn(p, gb, _pad_rows_front(conv_buf, SUBLANES), wts["conv_w"], wts["dn_norm_w"], dn_s,
                       nb=nb, seq=seq, sb=sb, tc=tc)
    o_rw, rw_new = _rw(p, _pad_rows_front(rw_prev[:, None, :], SUBLANES), wts["rw"], rw_s,
                       nb=nb, seq=seq, sb=sb_rw, tc=tc)
    y = _merge(x2d, o_dn, o_rw, p, wts["w_out"], wts["g_ffn"], wts["w_ff1"], wts["w_ff2"], wts["g_final"],
               tm=tm, tf=1024)
    p3 = p.reshape(nb, seq, P_W)
    conv_new = p3[:, seq - 3:, COL_QKV:COL_QKV + DN_QKV]
    shift_new = p3[:, seq - 1, COL_RW:]
    return y.reshape(nb, seq, D_MODEL), conv_new, dn_new, shift_new, rw_new


def kernel(x_prompt, x_sample, state_dn_conv, state_dn, state_rw_shift, state_rw, meta_tokens, g_mix_norm, w_in, dn_conv_w, dn_a_log, dn_dt_bias, dn_norm_w, rw_mu, rw_w0, rw_w2, rw_a0, rw_a2, rw_g2, rw_k_k, rw_k_a, rw_r_k, rw_gn_w, rw_gn_b, w_out, g_ffn_norm, w_ff1, w_ff2, g_final):
    assert g_mix_norm.shape[0] == 1, "single layer"
    w = w_in[0]
    o_z, o_a = DN_QKV, DN_QKV + D_MODEL
    o_rwp = o_a + 2 * DN_HEADS
    o_gate = o_rwp + RW_SHIFT_W
    w_main = jnp.concatenate([w[:, :o_a], w[:, o_gate:], w[:, o_rwp:o_gate]], axis=1).astype(BF16)
    w_ab_t = w[:, o_a:o_rwp].T.astype(BF16)

    def lora_pad(wl, lo):
        return jnp.zeros((RW_LORA, RW_W), F32).at[lo:lo + wl.shape[0]].set(wl).astype(BF16)

    row = lambda v: v.reshape(1, -1).astype(F32)
    wts = {
        "g_mix": row(g_mix_norm[0]), "w_main": w_main, "w_ab_t": w_ab_t,
        "a_log": dn_a_log[0].reshape(DN_HEADS, 1), "dt_bias": dn_dt_bias[0].reshape(DN_HEADS, 1),
        "conv_w": dn_conv_w[0], "dn_norm_w": row(dn_norm_w[0]),
        "rw": (row(rw_mu[0]), row(rw_w0[0]), row(rw_a0[0]), row(rw_k_k[0]), row(rw_k_a[0]), row(rw_r_k[0]),
               row(rw_gn_w[0]), row(rw_gn_b[0]),
               lora_pad(rw_w2[0], 0), lora_pad(rw_a2[0], 64), lora_pad(rw_g2[0], 128)),
        "w_out": w_out[0].astype(BF16), "g_ffn": row(g_ffn_norm[0]),
        "w_ff1": w_ff1[0].astype(BF16), "w_ff2": w_ff2[0].astype(BF16), "g_final": row(g_final),
    }
    nbp = x_prompt.shape[0]

    nbm = SUBLANES
    xm = jnp.broadcast_to(meta_tokens.astype(F32)[None], (nbm, N_META, D_MODEL))
    _, conv_m, dn_m, shift_m, rw_m = _layer(
        xm, jnp.zeros((nbm, 3, DN_QKV), F32), jnp.zeros((nbm, DN_HEADS, DN_D, DN_D), F32),
        jnp.zeros((nbm, RW_SHIFT_W), F32), jnp.zeros((nbm, RW_HEADS, RW_HD, RW_HD), F32), wts,
        sb=nbm, tc=N_META, tm=nbm * N_META)

    y_p, conv_p, dn_p, shift_p, rw_p = _layer(
        x_prompt, conv_m[:1], dn_m[:1], shift_m[:1], rw_m[:1], wts, sb=1, tc=64, tm=512)

    dec_len = x_sample.shape[1]
    y_s, conv_s, dn_s, shift_s, rw_s = _layer(
        x_sample, state_dn_conv[0], state_dn[0], state_rw_shift[0], state_rw[0], wts,
        sb=LANES // dec_len, tc=dec_len, tm=512, sb_rw=SUBLANES)

    del nbp
    return (y_p, y_s, conv_p[None], dn_p[None], shift_p[None], rw_p[None],
            conv_s[None], dn_s[None], shift_s[None], rw_s[None])
```

```python
import functools
import math

import jax
import jax.numpy as jnp
from jax import lax
from jax.experimental import pallas as pl
from jax.experimental.pallas import tpu as pltpu

F32 = jnp.float32
BF16 = jnp.bfloat16

D_MODEL = 1024
N_META = 16
DN_HEADS = 8
DN_D = 128
DN_QKV = 3 * DN_HEADS * DN_D
RW_HEADS = 16
RW_HD = 64
RW_PAIRS = RW_HEADS // 2
RW_W = RW_HEADS * RW_HD
RW_LORA = 256
RW_SHIFT_W = 3 * RW_W + RW_LORA
D_FF = 4 * D_MODEL
NORM_EPS = 1e-6
RW_GN_EPS = 64e-5
LANES = 128
SUBLANES = 8
INV_BLOCK = 16

COL_QKV = 0
COL_Z = DN_QKV
COL_GATE = COL_Z + D_MODEL
COL_RW = COL_GATE + 2 * D_MODEL
P_W = COL_RW + RW_SHIFT_W
P_HALF = P_W // 2

VMEM_LIMIT = 56 * 1024 * 1024


def _sigmoid(x):
    return 1.0 / (1.0 + jnp.exp(-x))


def _silu(x):
    return x * _sigmoid(x)


def _softplus(x):
    return jnp.maximum(x, 0.0) + jnp.log(1.0 + jnp.exp(-jnp.abs(x)))


def _dot(a, b):
    return jnp.dot(a.astype(BF16), b.astype(BF16), preferred_element_type=F32)


def _dot_nt(a, b):
    return lax.dot_general(a.astype(BF16), b.astype(BF16), (((1,), (1,)), ((), ())),
                           preferred_element_type=F32)


def _dot_tn(a, b):
    return lax.dot_general(a.astype(BF16), b.astype(BF16), (((0,), (0,)), ((), ())),
                           preferred_element_type=F32)


def _split2(x):
    hi = x.astype(BF16)
    lo = (x - hi.astype(F32)).astype(BF16)
    return hi, lo


def _each(f, *lists):
    return [f(*xs) for xs in zip(*lists)]


def _dot_hp(a, b):
    sa = [_split2(x) for x in a]
    sb = [_split2(x) for x in b]
    hh = [jnp.dot(x[0], y[0], preferred_element_type=F32) for x, y in zip(sa, sb)]
    hl = [jnp.dot(x[0], y[1], preferred_element_type=F32) for x, y in zip(sa, sb)]
    lh = [jnp.dot(x[1], y[0], preferred_element_type=F32) for x, y in zip(sa, sb)]
    return [p + (q + r) for p, q, r in zip(hh, hl, lh)]


def _dot_mask(mask_bf16, x):
    h1 = x.astype(BF16)
    r1 = x - h1.astype(F32)
    h2 = r1.astype(BF16)
    h3 = (r1 - h2.astype(F32)).astype(BF16)
    return (jnp.dot(mask_bf16, h1, preferred_element_type=F32)
            + (jnp.dot(mask_bf16, h2, preferred_element_type=F32)
               + jnp.dot(mask_bf16, h3, preferred_element_type=F32)))


def _iota2(n, m):
    return (lax.broadcasted_iota(jnp.int32, (n, m), 0),
            lax.broadcasted_iota(jnp.int32, (n, m), 1))


def _group(idx, size):
    return lax.shift_right_logical(idx, int(math.log2(size)))


def _neumann_inverse(a, eye, nil):
    inv = [eye - x for x in a]
    power = a
    k = 2
    while k < nil:
        power = _dot_hp(power, power)
        inv = _each(jnp.add, inv, _dot_hp(inv, power))
        k *= 2
    return inv


def _unit_lower_inverse(a, ri, ci, tc):
    eye = jnp.where(ri == ci, 1.0, 0.0).astype(F32)
    if tc <= INV_BLOCK:
        return _neumann_inverse(a, eye, tc)
    assert tc // INV_BLOCK <= 4
    diag = _group(ri, INV_BLOCK) == _group(ci, INV_BLOCK)
    d = [jnp.where(diag, x, 0.0) for x in a]
    low = _each(jnp.subtract, a, d)
    dinv = _neumann_inverse(d, eye, INV_BLOCK)
    b = _dot_hp(dinv, low)
    x = _each(jnp.add, dinv, _dot_hp(_dot_hp(b, b), dinv))
    return _each(jnp.subtract, x, _dot_hp(b, x))


def _proj_kernel(x_ref, g_ref, w_ref, wab_ref, alog_ref, dtb_ref, p_ref, gb_ref, *, tm, rc):
    x = x_ref[...]
    h = x * lax.rsqrt(jnp.mean(x * x, axis=-1, keepdims=True) + NORM_EPS) * g_ref[...]
    hb = h.astype(BF16)
    for c0 in range(0, P_HALF, 512):
        cw = min(512, P_HALF - c0)
        p_ref[:, c0:c0 + cw] = jnp.dot(hb, w_ref[:, c0:c0 + cw], preferred_element_type=F32)
    @pl.when(pl.program_id(0) == 0)
    def _():
        for c in range(tm // rc):
            ab = lax.dot_general(wab_ref[...], hb[c * rc:(c + 1) * rc], (((1,), (1,)), ((), ())),
                                 preferred_element_type=F32)
            g = -jnp.exp(alog_ref[...]) * _softplus(ab[0:DN_HEADS] + dtb_ref[...])
            gb_ref[c, 0:DN_HEADS, :] = g
            gb_ref[c, DN_HEADS:2 * DN_HEADS, :] = _sigmoid(ab[DN_HEADS:2 * DN_HEADS])


def _proj(x2d, g_norm, w_main, w_ab_t, a_log, dt_bias, *, tm, rc):
    n = x2d.shape[0]
    assert n % tm == 0 and tm % rc == 0
    return pl.pallas_call(
        functools.partial(_proj_kernel, tm=tm, rc=rc),
        out_shape=(jax.ShapeDtypeStruct((n, P_W), F32),
                   jax.ShapeDtypeStruct((n // rc, 2 * DN_HEADS, rc), F32)),
        grid=(2, n // tm),
        in_specs=[
            pl.BlockSpec((tm, D_MODEL), lambda j, i: (i, 0)),
            pl.BlockSpec((1, D_MODEL), lambda j, i: (0, 0)),
            pl.BlockSpec((D_MODEL, P_HALF), lambda j, i: (0, j)),
            pl.BlockSpec((2 * DN_HEADS, D_MODEL), lambda j, i: (0, 0)),
            pl.BlockSpec((DN_HEADS, 1), lambda j, i: (0, 0)),
            pl.BlockSpec((DN_HEADS, 1), lambda j, i: (0, 0)),
        ],
        out_specs=(
            pl.BlockSpec((tm, P_HALF), lambda j, i: (i, j)),
            pl.BlockSpec((tm // rc, 2 * DN_HEADS, rc),
                         lambda j, i: (i * (1 - j) + (n // tm - 1) * j, 0, 0)),
        ),
        compiler_params=pltpu.CompilerParams(
            dimension_semantics=("arbitrary", "arbitrary"), vmem_limit_bytes=VMEM_LIMIT),
        name="proj",
    )(x2d, g_norm, w_main, w_ab_t, a_log, dt_bias)


def _dn_kernel(qkv_ref, z_ref, gb_ref, cb_ref, cw_ref, nw_ref, s0_ref,
               o_ref, sout_ref, xb, c_qkv, s_scr, *, sb, tc, nt):
    t = pl.program_id(1)
    r = sb * tc
    hw = DN_HEADS * DN_D

    @pl.when(t == 0)
    def _():
        s_scr[...] = s0_ref[...]
        xb[:, 0:SUBLANES, :] = cb_ref[...]

    @pl.when(t > 0)
    def _():
        xb[:, 0:SUBLANES, :] = xb[:, tc:tc + SUBLANES, :]

    xb[:, SUBLANES:, :] = qkv_ref[...].reshape(sb, tc, DN_QKV)
    acc = xb[:, 5:5 + tc, :] * cw_ref[0:1, :]
    acc = acc + xb[:, 6:6 + tc, :] * cw_ref[1:2, :]
    acc = acc + xb[:, 7:7 + tc, :] * cw_ref[2:3, :]
    acc = acc + xb[:, 8:8 + tc, :] * cw_ref[3:4, :]
    c_qkv[...] = _silu(acc).reshape(r, DN_QKV)

    ri, ci = _iota2(r, r)
    same = _group(ri, tc) == _group(ci, tc)
    incl = same & (ri >= ci)
    strict = same & (ri > ci)
    eye = ri == ci

    heads = range(DN_HEADS)
    q = [c_qkv[:, h * DN_D:(h + 1) * DN_D] for h in heads]
    k = [c_qkv[:, hw + h * DN_D:hw + (h + 1) * DN_D] for h in heads]
    v = [c_qkv[:, 2 * hw + h * DN_D:2 * hw + (h + 1) * DN_D] for h in heads]
    q = [x * lax.rsqrt(jnp.sum(x * x, axis=-1, keepdims=True) + 1e-6) * (DN_D ** -0.5) for x in q]
    k = [x * lax.rsqrt(jnp.sum(x * x, axis=-1, keepdims=True) + 1e-6) for x in k]

    g_row = [gb_ref[0, h:h + 1, :] for h in heads]
    b_row = [gb_ref[0, DN_HEADS + h:DN_HEADS + h + 1, :] for h in heads]
    g_col = [jnp.sum(jnp.where(incl, x, 0.0), axis=1, keepdims=True) for x in g_row]
    g_cum_row = [jnp.sum(jnp.where(eye, x, 0.0), axis=0, keepdims=True) for x in g_col]
    g_tot = [jnp.sum(jnp.where(same, x, 0.0), axis=1, keepdims=True) for x in g_row]
    b_col = [jnp.sum(jnp.where(eye, x, 0.0), axis=1, keepdims=True) for x in b_row]
    decay = [jnp.where(incl, jnp.exp(jnp.minimum(c - rw_, 0.0)), 0.0) for c, rw_ in zip(g_col, g_cum_row)]
    e_g = [jnp.exp(x) for x in g_col]
    e_rest = [jnp.exp(tot - c) for tot, c in zip(g_tot, g_col)]

    kb = _each(jnp.multiply, k, b_col)
    a = [jnp.where(strict, _dot_nt(x, y) * d, 0.0) for x, y, d in zip(kb, k, decay)]
    t_inv = _unit_lower_inverse(a, ri, ci, tc)
    rhs = [jnp.concatenate([vv * b, x * e], axis=1) for vv, b, x, e in zip(v, b_col, kb, e_g)]
    sol = _dot_hp(t_inv, rhs)
    u_all = [x[:, :DN_D] for x in sol]
    w_all = [x[:, DN_D:] for x in sol]
    aqk = [_dot_nt(x, y) * d for x, y, d in zip(q, k, decay)]
    qg = _each(jnp.multiply, q, e_g)
    kd = _each(jnp.multiply, k, e_rest)

    us = [[] for _ in heads]
    qs = [[] for _ in heads]
    for s in range(sb):
        rows = slice(s * tc, (s + 1) * tc)
        ws = [_dot(jnp.concatenate([w_all[h][rows], qg[h][rows]], axis=0), s_scr[s, h]) for h in heads]
        for h in heads:
            us[h].append(u_all[h][rows] - ws[h][:tc])
            qs[h].append(ws[h][tc:])
    u = [x[0] if sb == 1 else jnp.concatenate(x, axis=0) for x in us]
    q_s = [x[0] if sb == 1 else jnp.concatenate(x, axis=0) for x in qs]
    o = [x + _dot(m, y) for x, m, y in zip(q_s, aqk, u)]
    for s in range(sb):
        rows = slice(s * tc, (s + 1) * tc)
        upd = [_dot_tn(kd[h][rows], u[h][rows]) for h in heads]
        for h in heads:
            gl = jnp.exp(g_tot[h][s * tc:s * tc + 1, :])
            s_scr[s, h] = s_scr[s, h] * gl + upd[h]

    for h in heads:
        cols = slice(h * DN_D, (h + 1) * DN_D)
        oh = o[h]
        oh = oh * lax.rsqrt(jnp.mean(oh * oh, axis=-1, keepdims=True) + NORM_EPS) * nw_ref[...]
        o_ref[:, cols] = oh * _silu(z_ref[:, cols])

    @pl.when(t == nt - 1)
    def _():
        sout_ref[...] = s_scr[...]


def _dn(p, gb, conv_buf8, conv_w, norm_w, s0, *, nb, seq, sb, tc):
    nt = seq // tc
    r = sb * tc
    assert seq % tc == 0 and nb % sb == 0 and (nt == 1 or sb == 1)
    bcast = s0.shape[0] == 1
    assert bcast == (conv_buf8.shape[0] == 1) and (not bcast or sb == 1)
    hw = DN_HEADS * DN_D

    def st_idx(i):
        return 0 if bcast else i

    return pl.pallas_call(
        functools.partial(_dn_kernel, sb=sb, tc=tc, nt=nt),
        out_shape=(jax.ShapeDtypeStruct((nb * seq, hw), F32),
                   jax.ShapeDtypeStruct((nb, DN_HEADS, DN_D, DN_D), F32)),
        grid=(nb // sb, nt),
        in_specs=[
            pl.BlockSpec((r, DN_QKV), lambda i, t: (i * nt + t, COL_QKV // DN_QKV)),
            pl.BlockSpec((r, hw), lambda i, t: (i * nt + t, COL_Z // hw)),
            pl.BlockSpec((1, 2 * DN_HEADS, r), lambda i, t: (i * nt + t, 0, 0)),
            pl.BlockSpec((sb, SUBLANES, DN_QKV), lambda i, t: (st_idx(i), 0, 0)),
            pl.BlockSpec((4, DN_QKV), lambda i, t: (0, 0)),
            pl.BlockSpec((1, DN_D), lambda i, t: (0, 0)),
            pl.BlockSpec((sb, DN_HEADS, DN_D, DN_D), lambda i, t: (st_idx(i), 0, 0, 0)),
        ],
        out_specs=(
            pl.BlockSpec((r, hw), lambda i, t: (i * nt + t, 0)),
            pl.BlockSpec((sb, DN_HEADS, DN_D, DN_D), lambda i, t: (i, 0, 0, 0)),
        ),
        scratch_shapes=[
            pltpu.VMEM((sb, tc + SUBLANES, DN_QKV), F32),
            pltpu.VMEM((r, DN_QKV), F32),
            pltpu.VMEM((sb, DN_HEADS, DN_D, DN_D), F32),
        ],
        compiler_params=pltpu.CompilerParams(
            dimension_semantics=("arbitrary", "arbitrary"), vmem_limit_bytes=VMEM_LIMIT),
        name="dn",
    )(p, p, gb, conv_buf8, conv_w, norm_w, s0)


def _rw_kernel(r_ref, k_ref, v_ref, l_ref, prev_ref, mu_ref, w0_ref, a0_ref, kkw_ref, kaw_ref, rkw_ref,
               gnw_ref, gnb_ref, w2_ref, a2_ref, g2_ref, s0_ref,
               o_ref, sout_ref,
               xbr, xbk, xbv, xbl, c_r, c_lw, c_kk, c_k2, c_v, c_a, c_g, s_scr, *, sb, tc, nt):
    t = pl.program_id(1)
    r = sb * tc
    half = lax.broadcasted_iota(jnp.int32, (1, LANES), 1) < RW_HD
    ri128, ci128 = _iota2(LANES, LANES)
    bd_mask = _group(ri128, RW_HD) == _group(ci128, RW_HD)
    bd_ones = jnp.where(bd_mask, 1.0, 0.0).astype(BF16)

    def seg_sum(xs):
        parts = [_split2(x) for x in xs]
        hi = [jnp.dot(x[0], bd_ones, preferred_element_type=F32) for x in parts]
        lo = [jnp.dot(x[1], bd_ones, preferred_element_type=F32) for x in parts]
        return _each(jnp.add, hi, lo)

    @pl.when(t == 0)
    def _():
        xbr[:, 0:SUBLANES, :] = prev_ref[:, :, 0:RW_W]
        xbk[:, 0:SUBLANES, :] = prev_ref[:, :, RW_W:2 * RW_W]
        xbv[:, 0:SUBLANES, :] = prev_ref[:, :, 2 * RW_W:3 * RW_W]
        xbl[:, 0:SUBLANES, :] = prev_ref[:, :, 3 * RW_W:]
        zero = jnp.zeros((RW_HD, RW_HD), F32)
        for s in range(sb):
            for p in range(RW_PAIRS):
                top = jnp.concatenate([s0_ref[s, 2 * p], zero], axis=1)
                bot = jnp.concatenate([zero, s0_ref[s, 2 * p + 1]], axis=1)
                s_scr[s, p] = jnp.concatenate([top, bot], axis=0)

    @pl.when(t > 0)
    def _():
        for xb in (xbr, xbk, xbv, xbl):
            xb[:, 0:SUBLANES, :] = xb[:, tc:tc + SUBLANES, :]

    def shift_mix(x_ref, xb, lo, width):
        xb[:, SUBLANES:, :] = x_ref[...].reshape(sb, tc, width)
        x = xb[:, SUBLANES:, :]
        prev = xb[:, SUBLANES - 1:SUBLANES - 1 + tc, :]
        return (x + (prev - x) * mu_ref[:, lo:lo + width]).reshape(r, width)

    xm_r = shift_mix(r_ref, xbr, 0, RW_W)
    xm_k = shift_mix(k_ref, xbk, RW_W, RW_W)
    xm_v = shift_mix(v_ref, xbv, 2 * RW_W, RW_W)
    xm_l = shift_mix(l_ref, xbl, 3 * RW_W, RW_LORA)
    lw = _dot(jnp.tanh(xm_l), w2_ref[...])
    la = _dot(xm_l, a2_ref[...])
    lg = _dot(_sigmoid(xm_l), g2_ref[...])
    w_log = -_softplus(-(w0_ref[...] + lw)) - 0.5
    a_all = _sigmoid(a0_ref[...] + la)
    c_r[...] = xm_r
    c_lw[...] = -jnp.exp(w_log)
    c_kk[...] = xm_k * kkw_ref[...]
    c_k2[...] = xm_k * (1.0 + (a_all - 1.0) * kaw_ref[...])
    c_v[...] = xm_v
    c_a[...] = a_all
    c_g[...] = lg

    ri, ci = _iota2(r, r)
    cum_mask = jnp.where((_group(ri, tc) == _group(ci, tc)) & (ri >= ci), 1.0, 0.0).astype(BF16)
    r2 = 2 * r
    ri2, ci2 = _iota2(r2, r2)
    same2 = _group(ri2, tc) == _group(ci2, tc)
    incl2 = same2 & (ri2 >= ci2)
    strict2 = same2 & (ri2 > ci2)

    def stack(x):
        return jnp.concatenate([jnp.where(half, x, 0.0), jnp.where(half, 0.0, x)], axis=0)

    def unstack(x2):
        return x2[:r] + x2[r:]

    pairs = range(RW_PAIRS)
    cols = [slice(p * LANES, (p + 1) * LANES) for p in pairs]
    rr = [c_r[:, c] for c in cols]
    logw = [c_lw[:, c] for c in cols]
    kk = [c_kk[:, c] for c in cols]
    k2 = [c_k2[:, c] for c in cols]
    vv = [c_v[:, c] for c in cols]
    kk_ss = seg_sum([x * x for x in kk])
    kk = [x * lax.rsqrt(ss + 1e-6) for x, ss in zip(kk, kk_ss)]
    bb = [x * c_a[:, c] for x, c in zip(kk, cols)]
    lcum = [_dot_mask(cum_mask, x) for x in logw]
    ltot = [jnp.broadcast_to(x.reshape(sb, tc, LANES)[:, tc - 1:tc, :], (sb, tc, LANES)).reshape(r, LANES)
            for x in lcum]
    p_in = [jnp.exp(x) for x in lcum]
    p_inv = [jnp.exp(-x) for x in lcum]
    alpha = [x * jnp.exp(lc - lw_) for x, lc, lw_ in zip(kk, lcum, logw)]
    beta = _each(jnp.multiply, bb, p_inv)
    kt = _each(jnp.multiply, k2, p_inv)
    rt = _each(jnp.multiply, rr, p_in)
    e_rest = [jnp.exp(lt - lc) for lt, lc in zip(ltot, lcum)]
    k_end = _each(jnp.multiply, k2, e_rest)
    b_end = _each(jnp.multiply, bb, e_rest)

    alpha2, rt2, beta2, kt2, v2 = ([stack(x) for x in lst] for lst in (alpha, rt, beta, kt, vv))
    lmat = [jnp.where(strict2, _dot_nt(x, y), 0.0) for x, y in zip(alpha2, beta2)]
    lk = [jnp.where(strict2, _dot_nt(x, y), 0.0) for x, y in zip(alpha2, kt2)]
    rb = [jnp.where(incl2, _dot_nt(x, y), 0.0) for x, y in zip(rt2, beta2)]
    rk = [jnp.where(incl2, _dot_nt(x, y), 0.0) for x, y in zip(rt2, kt2)]

    a_s = [[] for _ in pairs]
    r_s = [[] for _ in pairs]
    for s in range(sb):
        rows = slice(s * tc, (s + 1) * tc)
        both = [_dot_nt(jnp.concatenate([alpha[p][rows], rt[p][rows]], axis=0), s_scr[s, p]) for p in pairs]
        for p in pairs:
            a_s[p].append(both[p][:tc])
            r_s[p].append(both[p][tc:])
    alpha_s = [x[0] if sb == 1 else jnp.concatenate(x, axis=0) for x in a_s]
    r_state = [x[0] if sb == 1 else jnp.concatenate(x, axis=0) for x in r_s]

    lkv = _each(_dot, lk, v2)
    rhs2 = [stack(x) + y for x, y in zip(alpha_s, lkv)]
    t_inv = _unit_lower_inverse(lmat, ri2, ci2, tc)
    u2 = _dot_hp(t_inv, rhs2)
    rkv = _each(_dot, rk, v2)
    rbu = _each(_dot, rb, u2)
    o = [x + unstack(y - z) for x, y, z in zip(r_state, rkv, rbu)]
    u = [unstack(x) for x in u2]
    for s in range(sb):
        rows = slice(s * tc, (s + 1) * tc)
        upd = [_dot_tn(jnp.concatenate([vv[p][rows], -u[p][rows]], axis=0),
                       jnp.concatenate([k_end[p][rows], b_end[p][rows]], axis=0)) for p in pairs]
        for p in pairs:
            p_end = p_in[p][(s + 1) * tc - 1:(s + 1) * tc, :]
            s_scr[s, p] = s_scr[s, p] * p_end + jnp.where(bd_mask, upd[p], 0.0)

    mean = [x * (1.0 / RW_HD) for x in seg_sum(o)]
    d = _each(jnp.subtract, o, mean)
    var = [x * (1.0 / RW_HD) for x in seg_sum([x * x for x in d])]
    bonus = seg_sum([x * y * rkw_ref[:, c] for x, y, c in zip(rr, k2, cols)])
    for p in pairs:
        c = cols[p]
        on = d[p] * lax.rsqrt(var[p] + RW_GN_EPS) * gnw_ref[:, c] + gnb_ref[:, c]
        o_ref[:, c] = (on + bonus[p] * vv[p]) * c_g[:, c]

    @pl.when(t == nt - 1)
    def _():
        for s in range(sb):
            for p in range(RW_PAIRS):
                sp = s_scr[s, p]
                sout_ref[s, 2 * p] = sp[0:RW_HD, 0:RW_HD]
                sout_ref[s, 2 * p + 1] = sp[RW_HD:, RW_HD:]


def _rw(p, prev8, weights, s0, *, nb, seq, sb, tc):
    nt = seq // tc
    r = sb * tc
    assert seq % tc == 0 and nb % sb == 0 and (nt == 1 or sb == 1)
    bcast = s0.shape[0] == 1
    assert bcast == (prev8.shape[0] == 1) and (not bcast or sb == 1)
    (mu, w0, a0, kkw, kaw, rkw, gnw, gnb, w2p, a2p, g2p) = weights

    def st_idx(i):
        return 0 if bcast else i

    def vec(width):
        return pl.BlockSpec((1, width), lambda i, t: (0, 0))

    def lora_w():
        return pl.BlockSpec((RW_LORA, RW_W), lambda i, t: (0, 0))

    rcol = COL_RW // RW_W
    return pl.pallas_call(
        functools.partial(_rw_kernel, sb=sb, tc=tc, nt=nt),
        out_shape=(jax.ShapeDtypeStruct((nb * seq, RW_W), F32),
                   jax.ShapeDtypeStruct((nb, RW_HEADS, RW_HD, RW_HD), F32)),
        grid=(nb // sb, nt),
        in_specs=[
            pl.BlockSpec((r, RW_W), lambda i, t: (i * nt + t, rcol)),
            pl.BlockSpec((r, RW_W), lambda i, t: (i * nt + t, rcol + 1)),
            pl.BlockSpec((r, RW_W), lambda i, t: (i * nt + t, rcol + 2)),
            pl.BlockSpec((r, RW_LORA), lambda i, t: (i * nt + t, (COL_RW + 3 * RW_W) // RW_LORA)),
            pl.BlockSpec((sb, SUBLANES, RW_SHIFT_W), lambda i, t: (st_idx(i), 0, 0)),
            vec(RW_SHIFT_W), vec(RW_W), vec(RW_W), vec(RW_W), vec(RW_W), vec(RW_W), vec(RW_W), vec(RW_W),
            lora_w(), lora_w(), lora_w(),
            pl.BlockSpec((sb, RW_HEADS, RW_HD, RW_HD), lambda i, t: (st_idx(i), 0, 0, 0)),
        ],
        out_specs=(
            pl.BlockSpec((r, RW_W), lambda i, t: (i * nt + t, 0)),
            pl.BlockSpec((sb, RW_HEADS, RW_HD, RW_HD), lambda i, t: (i, 0, 0, 0)),
        ),
        scratch_shapes=[
            pltpu.VMEM((sb, tc + SUBLANES, RW_W), F32),
            pltpu.VMEM((sb, tc + SUBLANES, RW_W), F32),
            pltpu.VMEM((sb, tc + SUBLANES, RW_W), F32),
            pltpu.VMEM((sb, tc + SUBLANES, RW_LORA), F32),
        ] + [pltpu.VMEM((r, RW_W), F32)] * 7 + [
            pltpu.VMEM((sb, RW_PAIRS, LANES, LANES), F32),
        ],
        compiler_params=pltpu.CompilerParams(
            dimension_semantics=("arbitrary", "arbitrary"), vmem_limit_bytes=VMEM_LIMIT),
        name="rw",
    )(p, p, p, p, prev8, mu, w0, a0, kkw, kaw, rkw, gnw, gnb, w2p, a2p, g2p, s0)


def _merge_kernel(x_ref, odn_ref, orw_ref, ga_ref, gb_ref, wout_ref, gffn_ref, w1_ref, w2_ref, gfin_ref,
                  y_ref, x1_scr, h2_scr, acc_scr, *, nf):
    f = pl.program_id(1)

    @pl.when(f == 0)
    def _():
        mix = _sigmoid(ga_ref[...]) * odn_ref[...] + _sigmoid(gb_ref[...]) * orw_ref[...]
        x1 = x_ref[...] + jnp.dot(mix.astype(BF16), wout_ref[...], preferred_element_type=F32)
        x1_scr[...] = x1
        h2 = x1 * lax.rsqrt(jnp.mean(x1 * x1, axis=-1, keepdims=True) + NORM_EPS) * gffn_ref[...]
        h2_scr[...] = h2.astype(BF16)
        acc_scr[...] = jnp.zeros_like(acc_scr)

    hid = jnp.dot(h2_scr[...], w1_ref[...], preferred_element_type=F32)
    hid = jnp.square(jnp.maximum(hid, 0.0))
    acc_scr[...] += jnp.dot(hid.astype(BF16), w2_ref[...], preferred_element_type=F32)

    @pl.when(f == nf - 1)
    def _():
        y = x1_scr[...] + acc_scr[...]
        y_ref[...] = y * lax.rsqrt(jnp.mean(y * y, axis=-1, keepdims=True) + NORM_EPS) * gfin_ref[...]


def _merge(x2d, o_dn, o_rw, p, w_out, g_ffn, w1, w2, g_fin, *, tm, tf):
    n = x2d.shape[0]
    nf = D_FF // tf
    assert n % tm == 0 and D_FF % tf == 0
    gcol = COL_GATE // D_MODEL
    row = lambda i, f: (i, 0)
    return pl.pallas_call(
        functools.partial(_merge_kernel, nf=nf),
        out_shape=jax.ShapeDtypeStruct((n, D_MODEL), F32),
        grid=(n // tm, nf),
        in_specs=[
            pl.BlockSpec((tm, D_MODEL), row),
            pl.BlockSpec((tm, D_MODEL), row),
            pl.BlockSpec((tm, D_MODEL), row),
            pl.BlockSpec((tm, D_MODEL), lambda i, f: (i, gcol)),
            pl.BlockSpec((tm, D_MODEL), lambda i, f: (i, gcol + 1)),
            pl.BlockSpec((D_MODEL, D_MODEL), lambda i, f: (0, 0)),
            pl.BlockSpec((1, D_MODEL), lambda i, f: (0, 0)),
            pl.BlockSpec((D_MODEL, tf), lambda i, f: (0, f)),
            pl.BlockSpec((tf, D_MODEL), lambda i, f: (f, 0)),
            pl.BlockSpec((1, D_MODEL), lambda i, f: (0, 0)),
        ],
        out_specs=pl.BlockSpec((tm, D_MODEL), row),
        scratch_shapes=[
            pltpu.VMEM((tm, D_MODEL), F32),
            pltpu.VMEM((tm, D_MODEL), BF16),
            pltpu.VMEM((tm, D_MODEL), F32),
        ],
        compiler_params=pltpu.CompilerParams(
            dimension_semantics=("arbitrary", "arbitrary"), vmem_limit_bytes=VMEM_LIMIT),
        name="merge",
    )(x2d, o_dn, o_rw, p, p, w_out, g_ffn, w1, w2, g_fin)


def _pad_rows_front(x, rows):
    b, n, w = x.shape
    return jnp.concatenate([jnp.zeros((b, rows - n, w), x.dtype), x], axis=1)


def _layer(x, conv_buf, dn_s, rw_prev, rw_s, wts, *, sb, tc, tm, sb_rw=None):
    sb_rw = sb if sb_rw is None else sb_rw
    nb, seq, _ = x.shape
    n = nb * seq
    x2d = x.reshape(n, D_MODEL)
    p, gb = _proj(x2d, wts["g_mix"], wts["w_main"], wts["w_ab_t"], wts["a_log"], wts["dt_bias"],
                  tm=tm, rc=sb * tc)
    o_dn, dn_new = _dn(p, gb, _pad_rows_front(conv_buf, SUBLANES), wts["conv_w"], wts["dn_norm_w"], dn_s,
                       nb=nb, seq=seq, sb=sb, tc=tc)
    o_rw, rw_new = _rw(p, _pad_rows_front(rw_prev[:, None, :], SUBLANES), wts["rw"], rw_s,
                       nb=nb, seq=seq, sb=sb_rw, tc=tc)
    y = _merge(x2d, o_dn, o_rw, p, wts["w_out"], wts["g_ffn"], wts["w_ff1"], wts["w_ff2"], wts["g_final"],
               tm=tm, tf=1024)
    p3 = p.reshape(nb, seq, P_W)
    conv_new = p3[:, seq - 3:, COL_QKV:COL_QKV + DN_QKV]
    shift_new = p3[:, seq - 1, COL_RW:]
    return y.reshape(nb, seq, D_MODEL), conv_new, dn_new, shift_new, rw_new


def kernel(x_prompt, x_sample, state_dn_conv, state_dn, state_rw_shift, state_rw, meta_tokens, g_mix_norm, w_in, dn_conv_w, dn_a_log, dn_dt_bias, dn_norm_w, rw_mu, rw_w0, rw_w2, rw_a0, rw_a2, rw_g2, rw_k_k, rw_k_a, rw_r_k, rw_gn_w, rw_gn_b, w_out, g_ffn_norm, w_ff1, w_ff2, g_final):
    assert g_mix_norm.shape[0] == 1, "single layer"
    w = w_in[0]
    o_z, o_a = DN_QKV, DN_QKV + D_MODEL
    o_rwp = o_a + 2 * DN_HEADS
    o_gate = o_rwp + RW_SHIFT_W
    w_main = jnp.concatenate([w[:, :o_a], w[:, o_gate:], w[:, o_rwp:o_gate]], axis=1).astype(BF16)
    w_ab_t = w[:, o_a:o_rwp].T.astype(BF16)

    def lora_pad(wl, lo):
        return jnp.zeros((RW_LORA, RW_W), F32).at[lo:lo + wl.shape[0]].set(wl).astype(BF16)

    row = lambda v: v.reshape(1, -1).astype(F32)
    wts = {
        "g_mix": row(g_mix_norm[0]), "w_main": w_main, "w_ab_t": w_ab_t,
        "a_log": dn_a_log[0].reshape(DN_HEADS, 1), "dt_bias": dn_dt_bias[0].reshape(DN_HEADS, 1),
        "conv_w": dn_conv_w[0], "dn_norm_w": row(dn_norm_w[0]),
        "rw": (row(rw_mu[0]), row(rw_w0[0]), row(rw_a0[0]), row(rw_k_k[0]), row(rw_k_a[0]), row(rw_r_k[0]),
               row(rw_gn_w[0]), row(rw_gn_b[0]),
               lora_pad(rw_w2[0], 0), lora_pad(rw_a2[0], 64), lora_pad(rw_g2[0], 128)),
        "w_out": w_out[0].astype(BF16), "g_ffn": row(g_ffn_norm[0]),
        "w_ff1": w_ff1[0].astype(BF16), "w_ff2": w_ff2[0].astype(BF16), "g_final": row(g_final),
    }
    nbp = x_prompt.shape[0]

    nbm = SUBLANES
    xm = jnp.broadcast_to(meta_tokens.astype(F32)[None], (nbm, N_META, D_MODEL))
    _, conv_m, dn_m, shift_m, rw_m = _layer(
        xm, jnp.zeros((nbm, 3, DN_QKV), F32), jnp.zeros((nbm, DN_HEADS, DN_D, DN_D), F32),
        jnp.zeros((nbm, RW_SHIFT_W), F32), jnp.zeros((nbm, RW_HEADS, RW_HD, RW_HD), F32), wts,
        sb=nbm, tc=N_META, tm=nbm * N_META)

    y_p, conv_p, dn_p, shift_p, rw_p = _layer(
        x_prompt, conv_m[:1], dn_m[:1], shift_m[:1], rw_m[:1], wts, sb=1, tc=64, tm=512)

    dec_len = x_sample.shape[1]
    y_s, conv_s, dn_s, shift_s, rw_s = _layer(
        x_sample, state_dn_conv[0], state_dn[0], state_rw_shift[0], state_rw[0], wts,
        sb=SUBLANES, tc=dec_len, tm=512)

    del nbp
    return (y_p, y_s, conv_p[None], dn_p[None], shift_p[None], rw_p[None],
            conv_s[None], dn_s[None], shift_s[None], rw_s[None])
```

```python
import functools
import math

import jax
import jax.numpy as jnp
from jax import lax
from jax.experimental import pallas as pl
from jax.experimental.pallas import tpu as pltpu

F32 = jnp.float32
BF16 = jnp.bfloat16

D_MODEL = 1024
N_META = 16
DN_HEADS = 8
DN_D = 128
DN_QKV = 3 * DN_HEADS * DN_D
RW_HEADS = 16
RW_HD = 64
RW_PAIRS = RW_HEADS // 2
RW_W = RW_HEADS * RW_HD
RW_LORA = 256
RW_SHIFT_W = 3 * RW_W + RW_LORA
D_FF = 4 * D_MODEL
NORM_EPS = 1e-6
RW_GN_EPS = 64e-5
LANES = 128
SUBLANES = 8
INV_BLOCK = 16

COL_QKV = 0
COL_Z = DN_QKV
COL_GATE = COL_Z + D_MODEL
COL_RW = COL_GATE + 2 * D_MODEL
P_W = COL_RW + RW_SHIFT_W
P_HALF = P_W // 2

VMEM_LIMIT = 56 * 1024 * 1024


def _sigmoid(x):
    return 1.0 / (1.0 + jnp.exp(-x))


def _silu(x):
    return x * _sigmoid(x)


def _softplus(x):
    return jnp.maximum(x, 0.0) + jnp.log(1.0 + jnp.exp(-jnp.abs(x)))


def _dot(a, b):
    return jnp.dot(a.astype(BF16), b.astype(BF16), preferred_element_type=F32)


def _dot_nt(a, b):
    return lax.dot_general(a.astype(BF16), b.astype(BF16), (((1,), (1,)), ((), ())),
                           preferred_element_type=F32)


def _dot_tn(a, b):
    return lax.dot_general(a.astype(BF16), b.astype(BF16), (((0,), (0,)), ((), ())),
                           preferred_element_type=F32)


def _split2(x):
    hi = x.astype(BF16)
    lo = (x - hi.astype(F32)).astype(BF16)
    return hi, lo


def _each(f, *lists):
    return [f(*xs) for xs in zip(*lists)]


def _rows(xs):
    return xs[0] if len(xs) == 1 else jnp.concatenate(xs, axis=0)


def _dot_shared(lhs, b, dot=_dot):
    out = [dot(_rows(xs), y) for xs, y in zip(lhs, b)]
    res = []
    for xs, o in zip(lhs, out):
        off, parts = 0, []
        for x in xs:
            parts.append(o[off:off + x.shape[0]])
            off += x.shape[0]
        res.append(parts)
    return res


def _dot_mask(mask_bf16, x):
    h1 = x.astype(BF16)
    r1 = x - h1.astype(F32)
    h2 = r1.astype(BF16)
    h3 = (r1 - h2.astype(F32)).astype(BF16)
    return (jnp.dot(mask_bf16, h1, preferred_element_type=F32)
            + (jnp.dot(mask_bf16, h2, preferred_element_type=F32)
               + jnp.dot(mask_bf16, h3, preferred_element_type=F32)))


def _iota2(n, m):
    return (lax.broadcasted_iota(jnp.int32, (n, m), 0),
            lax.broadcasted_iota(jnp.int32, (n, m), 1))


def _group(idx, size):
    return lax.shift_right_logical(idx, int(math.log2(size)))


def _neumann_inverse(a, eye, nil):
    inv = [eye - x for x in a]
    if nil <= 2:
        return inv
    power = [p[0] for p in _dot_shared([[x] for x in a], a)]
    k = 4
    while k < nil:
        res = _dot_shared([[i, p] for i, p in zip(inv, power)], power)
        inv = [i + p[0] for i, p in zip(inv, res)]
        power = [p[1] for p in res]
        k *= 2
    res = _dot_shared([[i] for i in inv], power)
    return [i + p[0] for i, p in zip(inv, res)]


def _unit_lower_inverse(a, ri, ci, tc):
    eye = jnp.where(ri == ci, 1.0, 0.0).astype(F32)
    if tc <= INV_BLOCK:
        return _neumann_inverse(a, eye, tc)
    assert tc // INV_BLOCK <= 4
    diag = _group(ri, INV_BLOCK) == _group(ci, INV_BLOCK)
    d = [jnp.where(diag, x, 0.0) for x in a]
    low = _each(jnp.subtract, a, d)
    dinv = _neumann_inverse(d, eye, INV_BLOCK)
    b = [p[0] for p in _dot_shared([[x] for x in low], dinv)]
    res = _dot_shared([[x, y] for x, y in zip(dinv, b)], b)
    p1 = [x - p[0] for x, p in zip(dinv, res)]
    tail = _dot_shared([[x] for x in p1], [p[1] for p in res])
    return [x + p[0] for x, p in zip(p1, tail)]


def _proj_kernel(x_ref, g_ref, w_ref, wab_ref, alog_ref, dtb_ref, p_ref, gb_ref, *, tm, rc):
    x = x_ref[...]
    h = x * lax.rsqrt(jnp.mean(x * x, axis=-1, keepdims=True) + NORM_EPS) * g_ref[...]
    hb = h.astype(BF16)
    for c0 in range(0, P_HALF, 512):
        cw = min(512, P_HALF - c0)
        p_ref[:, c0:c0 + cw] = jnp.dot(hb, w_ref[:, c0:c0 + cw], preferred_element_type=F32)
    @pl.when(pl.program_id(0) == 0)
    def _():
        for c in range(tm // rc):
            ab = lax.dot_general(wab_ref[...], hb[c * rc:(c + 1) * rc], (((1,), (1,)), ((), ())),
                                 preferred_element_type=F32)
            g = -jnp.exp(alog_ref[...]) * _softplus(ab[0:DN_HEADS] + dtb_ref[...])
            gb_ref[c, 0:DN_HEADS, :] = g
            gb_ref[c, DN_HEADS:2 * DN_HEADS, :] = _sigmoid(ab[DN_HEADS:2 * DN_HEADS])


def _proj(x2d, g_norm, w_main, w_ab_t, a_log, dt_bias, *, tm, rc):
    n = x2d.shape[0]
    assert n % tm == 0 and tm % rc == 0
    return pl.pallas_call(
        functools.partial(_proj_kernel, tm=tm, rc=rc),
        out_shape=(jax.ShapeDtypeStruct((n, P_W), F32),
                   jax.ShapeDtypeStruct((n // rc, 2 * DN_HEADS, rc), F32)),
        grid=(2, n // tm),
        in_specs=[
            pl.BlockSpec((tm, D_MODEL), lambda j, i: (i, 0)),
            pl.BlockSpec((1, D_MODEL), lambda j, i: (0, 0)),
            pl.BlockSpec((D_MODEL, P_HALF), lambda j, i: (0, j)),
            pl.BlockSpec((2 * DN_HEADS, D_MODEL), lambda j, i: (0, 0)),
            pl.BlockSpec((DN_HEADS, 1), lambda j, i: (0, 0)),
            pl.BlockSpec((DN_HEADS, 1), lambda j, i: (0, 0)),
        ],
        out_specs=(
            pl.BlockSpec((tm, P_HALF), lambda j, i: (i, j)),
            pl.BlockSpec((tm // rc, 2 * DN_HEADS, rc),
                         lambda j, i: (i * (1 - j) + (n // tm - 1) * j, 0, 0)),
        ),
        compiler_params=pltpu.CompilerParams(
            dimension_semantics=("arbitrary", "arbitrary"), vmem_limit_bytes=VMEM_LIMIT),
        name="proj",
    )(x2d, g_norm, w_main, w_ab_t, a_log, dt_bias)


def _dn_kernel(qkv_ref, z_ref, gb_ref, cb_ref, cw_ref, nw_ref, s0_ref,
               o_ref, sout_ref, cout_ref, xb, c_qkv, s_scr, *, sb, tc, nt):
    t = pl.program_id(1)
    r = sb * tc
    hw = DN_HEADS * DN_D

    @pl.when(t == 0)
    def _():
        s_scr[...] = s0_ref[...]
        xb[:, 0:SUBLANES, :] = cb_ref[...]

    @pl.when(t > 0)
    def _():
        xb[:, 0:SUBLANES, :] = xb[:, tc:tc + SUBLANES, :]

    xb[:, SUBLANES:, :] = qkv_ref[...].reshape(sb, tc, DN_QKV)
    acc = xb[:, 5:5 + tc, :] * cw_ref[0:1, :]
    acc = acc + xb[:, 6:6 + tc, :] * cw_ref[1:2, :]
    acc = acc + xb[:, 7:7 + tc, :] * cw_ref[2:3, :]
    acc = acc + xb[:, 8:8 + tc, :] * cw_ref[3:4, :]
    c_qkv[...] = _silu(acc).reshape(r, DN_QKV)

    ri, ci = _iota2(r, r)
    same = _group(ri, tc) == _group(ci, tc)
    incl = same & (ri >= ci)
    strict = same & (ri > ci)
    eye = ri == ci

    heads = range(DN_HEADS)
    q = [c_qkv[:, h * DN_D:(h + 1) * DN_D] for h in heads]
    k = [c_qkv[:, hw + h * DN_D:hw + (h + 1) * DN_D] for h in heads]
    v = [c_qkv[:, 2 * hw + h * DN_D:2 * hw + (h + 1) * DN_D] for h in heads]
    q = [x * lax.rsqrt(jnp.sum(x * x, axis=-1, keepdims=True) + 1e-6) * (DN_D ** -0.5) for x in q]
    k = [x * lax.rsqrt(jnp.sum(x * x, axis=-1, keepdims=True) + 1e-6) for x in k]

    g_row = [gb_ref[0, h:h + 1, :] for h in heads]
    b_row = [gb_ref[0, DN_HEADS + h:DN_HEADS + h + 1, :] for h in heads]
    g_col = [jnp.sum(jnp.where(incl, x, 0.0), axis=1, keepdims=True) for x in g_row]
    g_cum_row = [jnp.sum(jnp.where(eye, x, 0.0), axis=0, keepdims=True) for x in g_col]
    g_tot = [jnp.sum(jnp.where(same, x, 0.0), axis=1, keepdims=True) for x in g_row]
    b_col = [jnp.sum(jnp.where(eye, x, 0.0), axis=1, keepdims=True) for x in b_row]
    decay = [jnp.where(incl, jnp.exp(jnp.minimum(c - rw_, 0.0)), 0.0) for c, rw_ in zip(g_col, g_cum_row)]
    e_g = [jnp.exp(x) for x in g_col]
    e_rest = [jnp.exp(tot - c) for tot, c in zip(g_tot, g_col)]

    kb = _each(jnp.multiply, k, b_col)
    kk_t = _dot_shared([[x, y] for x, y in zip(kb, q)], k, dot=_dot_nt)
    a = [jnp.where(strict, p[0] * d, 0.0) for p, d in zip(kk_t, decay)]
    aqk = [p[1] * d for p, d in zip(kk_t, decay)]
    t_inv = _unit_lower_inverse(a, ri, ci, tc)
    rhs = [jnp.concatenate([vv * b, x * e], axis=1) for vv, b, x, e in zip(v, b_col, kb, e_g)]
    sol = [p[0] for p in _dot_shared([[x] for x in t_inv], rhs)]
    u_all = [x[:, :DN_D] for x in sol]
    w_all = [x[:, DN_D:] for x in sol]
    qg = _each(jnp.multiply, q, e_g)
    kd = _each(jnp.multiply, k, e_rest)

    us = [[] for _ in heads]
    qs = [[] for _ in heads]
    for s in range(sb):
        rows = slice(s * tc, (s + 1) * tc)
        ws = [_dot(jnp.concatenate([w_all[h][rows], qg[h][rows]], axis=0), s_scr[s, h]) for h in heads]
        for h in heads:
            us[h].append(u_all[h][rows] - ws[h][:tc])
            qs[h].append(ws[h][tc:])
    u = [x[0] if sb == 1 else jnp.concatenate(x, axis=0) for x in us]
    q_s = [x[0] if sb == 1 else jnp.concatenate(x, axis=0) for x in qs]
    o = [x + _dot(m, y) for x, m, y in zip(q_s, aqk, u)]
    for s in range(sb):
        rows = slice(s * tc, (s + 1) * tc)
        upd = [_dot_tn(kd[h][rows], u[h][rows]) for h in heads]
        for h in heads:
            gl = jnp.exp(g_tot[h][s * tc:s * tc + 1, :])
            s_scr[s, h] = s_scr[s, h] * gl + upd[h]

    for h in heads:
        cols = slice(h * DN_D, (h + 1) * DN_D)
        oh = o[h]
        oh = oh * lax.rsqrt(jnp.mean(oh * oh, axis=-1, keepdims=True) + NORM_EPS) * nw_ref[...]
        o_ref[:, cols] = oh * _silu(z_ref[:, cols])

    @pl.when(t == nt - 1)
    def _():
        sout_ref[...] = s_scr[...]
        cout_ref[...] = xb[:, tc + SUBLANES - 3:tc + SUBLANES, :]


def _dn(p, gb, conv_buf8, conv_w, norm_w, s0, *, nb, seq, sb, tc):
    nt = seq // tc
    r = sb * tc
    assert seq % tc == 0 and nb % sb == 0 and (nt == 1 or sb == 1)
    bcast = s0.shape[0] == 1
    assert bcast == (conv_buf8.shape[0] == 1) and (not bcast or sb == 1)
    hw = DN_HEADS * DN_D

    def st_idx(i):
        return 0 if bcast else i

    return pl.pallas_call(
        functools.partial(_dn_kernel, sb=sb, tc=tc, nt=nt),
        out_shape=(jax.ShapeDtypeStruct((nb * seq, hw), F32),
                   jax.ShapeDtypeStruct((nb, DN_HEADS, DN_D, DN_D), F32),
                   jax.ShapeDtypeStruct((nb, 3, DN_QKV), F32)),
        grid=(nb // sb, nt),
        in_specs=[
            pl.BlockSpec((r, DN_QKV), lambda i, t: (i * nt + t, COL_QKV // DN_QKV)),
            pl.BlockSpec((r, hw), lambda i, t: (i * nt + t, COL_Z // hw)),
            pl.BlockSpec((1, 2 * DN_HEADS, r), lambda i, t: (i * nt + t, 0, 0)),
            pl.BlockSpec((sb, SUBLANES, DN_QKV), lambda i, t: (st_idx(i), 0, 0)),
            pl.BlockSpec((4, DN_QKV), lambda i, t: (0, 0)),
            pl.BlockSpec((1, DN_D), lambda i, t: (0, 0)),
            pl.BlockSpec((sb, DN_HEADS, DN_D, DN_D), lambda i, t: (st_idx(i), 0, 0, 0)),
        ],
        out_specs=(
            pl.BlockSpec((r, hw), lambda i, t: (i * nt + t, 0)),
            pl.BlockSpec((sb, DN_HEADS, DN_D, DN_D), lambda i, t: (i, 0, 0, 0)),
            pl.BlockSpec((sb, 3, DN_QKV), lambda i, t: (i, 0, 0)),
        ),
        scratch_shapes=[
            pltpu.VMEM((sb, tc + SUBLANES, DN_QKV), F32),
            pltpu.VMEM((r, DN_QKV), F32),
            pltpu.VMEM((sb, DN_HEADS, DN_D, DN_D), F32),
        ],
        compiler_params=pltpu.CompilerParams(
            dimension_semantics=("arbitrary", "arbitrary"), vmem_limit_bytes=VMEM_LIMIT),
        name="dn",
    )(p, p, gb, conv_buf8, conv_w, norm_w, s0)


def _rw_kernel(r_ref, k_ref, v_ref, l_ref, prev_ref, mu_ref, w0_ref, a0_ref, kkw_ref, kaw_ref, rkw_ref,
               gnw_ref, gnb_ref, w2_ref, a2_ref, g2_ref, s0_ref, odn_ref, ga_ref, gb_ref,
               o_ref, sout_ref, shift_ref,
               xbr, xbk, xbv, xbl, c_r, c_lw, c_kk, c_k2, c_v, c_a, c_g, s_scr, *, sb, tc, nt):
    t = pl.program_id(1)
    r = sb * tc
    half = lax.broadcasted_iota(jnp.int32, (1, LANES), 1) < RW_HD
    ri128, ci128 = _iota2(LANES, LANES)
    bd_mask = _group(ri128, RW_HD) == _group(ci128, RW_HD)
    bd_ones = jnp.where(bd_mask, 1.0, 0.0).astype(BF16)

    def seg_sum(xs):
        parts = [_split2(x) for x in xs]
        n = xs[0].shape[0]
        both = jnp.dot(_rows([p[0] for p in parts] + [p[1] for p in parts]), bd_ones,
                       preferred_element_type=F32)
        m = len(xs) * n
        return [both[i * n:(i + 1) * n] + both[m + i * n:m + (i + 1) * n] for i in range(len(xs))]

    @pl.when(t == 0)
    def _():
        xbr[:, 0:SUBLANES, :] = prev_ref[:, :, 0:RW_W]
        xbk[:, 0:SUBLANES, :] = prev_ref[:, :, RW_W:2 * RW_W]
        xbv[:, 0:SUBLANES, :] = prev_ref[:, :, 2 * RW_W:3 * RW_W]
        xbl[:, 0:SUBLANES, :] = prev_ref[:, :, 3 * RW_W:]
        zero = jnp.zeros((RW_HD, RW_HD), F32)
        for s in range(sb):
            for p in range(RW_PAIRS):
                top = jnp.concatenate([s0_ref[s, 2 * p], zero], axis=1)
                bot = jnp.concatenate([zero, s0_ref[s, 2 * p + 1]], axis=1)
                s_scr[s, p] = jnp.concatenate([top, bot], axis=0)

    @pl.when(t > 0)
    def _():
        for xb in (xbr, xbk, xbv, xbl):
            xb[:, 0:SUBLANES, :] = xb[:, tc:tc + SUBLANES, :]

    def shift_mix(x_ref, xb, lo, width):
        xb[:, SUBLANES:, :] = x_ref[...].reshape(sb, tc, width)
        x = xb[:, SUBLANES:, :]
        prev = xb[:, SUBLANES - 1:SUBLANES - 1 + tc, :]
        return (x + (prev - x) * mu_ref[:, lo:lo + width]).reshape(r, width)

    xm_r = shift_mix(r_ref, xbr, 0, RW_W)
    xm_k = shift_mix(k_ref, xbk, RW_W, RW_W)
    xm_v = shift_mix(v_ref, xbv, 2 * RW_W, RW_W)
    xm_l = shift_mix(l_ref, xbl, 3 * RW_W, RW_LORA)
    lw = _dot(jnp.tanh(xm_l), w2_ref[...])
    la = _dot(xm_l, a2_ref[...])
    lg = _dot(_sigmoid(xm_l), g2_ref[...])
    w_log = -_softplus(-(w0_ref[...] + lw)) - 0.5
    a_all = _sigmoid(a0_ref[...] + la)
    c_r[...] = xm_r
    c_lw[...] = -jnp.exp(w_log)
    c_kk[...] = xm_k * kkw_ref[...]
    c_k2[...] = xm_k * (1.0 + (a_all - 1.0) * kaw_ref[...])
    c_v[...] = xm_v
    c_a[...] = a_all
    c_g[...] = lg

    ri, ci = _iota2(r, r)
    cum_mask = jnp.where((_group(ri, tc) == _group(ci, tc)) & (ri >= ci), 1.0, 0.0).astype(BF16)
    r2 = 2 * r
    ri2, ci2 = _iota2(r2, r2)
    same2 = _group(ri2, tc) == _group(ci2, tc)
    incl2 = same2 & (ri2 >= ci2)
    strict2 = same2 & (ri2 > ci2)

    def stack(x):
        return jnp.concatenate([jnp.where(half, x, 0.0), jnp.where(half, 0.0, x)], axis=0)

    def unstack(x2):
        return x2[:r] + x2[r:]

    pairs = range(RW_PAIRS)
    cols = [slice(p * LANES, (p + 1) * LANES) for p in pairs]
    rr = [c_r[:, c] for c in cols]
    logw = [c_lw[:, c] for c in cols]
    kk = [c_kk[:, c] for c in cols]
    k2 = [c_k2[:, c] for c in cols]
    vv = [c_v[:, c] for c in cols]
    kk_ss = seg_sum([x * x for x in kk])
    kk = [x * lax.rsqrt(ss + 1e-6) for x, ss in zip(kk, kk_ss)]
    bb = [x * c_a[:, c] for x, c in zip(kk, cols)]
    lcum = [_dot_mask(cum_mask, x) for x in logw]
    ltot = [jnp.broadcast_to(x.reshape(sb, tc, LANES)[:, tc - 1:tc, :], (sb, tc, LANES)).reshape(r, LANES)
            for x in lcum]
    p_in = [jnp.exp(x) for x in lcum]
    p_inv = [jnp.exp(-x) for x in lcum]
    alpha = [x * jnp.exp(lc - lw_) for x, lc, lw_ in zip(kk, lcum, logw)]
    beta = _each(jnp.multiply, bb, p_inv)
    kt = _each(jnp.multiply, k2, p_inv)
    rt = _each(jnp.multiply, rr, p_in)
    e_rest = [jnp.exp(lt - lc) for lt, lc in zip(ltot, lcum)]
    k_end = _each(jnp.multiply, k2, e_rest)
    b_end = _each(jnp.multiply, bb, e_rest)

    alpha2, rt2, beta2, kt2, v2 = ([stack(x) for x in lst] for lst in (alpha, rt, beta, kt, vv))
    ar2 = [[x, y] for x, y in zip(alpha2, rt2)]
    on_beta = _dot_shared(ar2, beta2, dot=_dot_nt)
    on_kt = _dot_shared(ar2, kt2, dot=_dot_nt)
    lmat = [jnp.where(strict2, p[0], 0.0) for p in on_beta]
    rb = [jnp.where(incl2, p[1], 0.0) for p in on_beta]
    lk = [jnp.where(strict2, p[0], 0.0) for p in on_kt]
    rk = [jnp.where(incl2, p[1], 0.0) for p in on_kt]

    a_s = [[] for _ in pairs]
    r_s = [[] for _ in pairs]
    for s in range(sb):
        rows = slice(s * tc, (s + 1) * tc)
        both = [_dot_nt(jnp.concatenate([alpha[p][rows], rt[p][rows]], axis=0), s_scr[s, p]) for p in pairs]
        for p in pairs:
            a_s[p].append(both[p][:tc])
            r_s[p].append(both[p][tc:])
    alpha_s = [x[0] if sb == 1 else jnp.concatenate(x, axis=0) for x in a_s]
    r_state = [x[0] if sb == 1 else jnp.concatenate(x, axis=0) for x in r_s]

    on_v = _dot_shared([[x, y] for x, y in zip(lk, rk)], v2)
    rhs2 = [stack(x) + p[0] for x, p in zip(alpha_s, on_v)]
    t_inv = _unit_lower_inverse(lmat, ri2, ci2, tc)
    u2 = [p[0] for p in _dot_shared([[x] for x in t_inv], rhs2)]
    rbu = _each(_dot, rb, u2)
    o = [x + unstack(p[1] - z) for x, p, z in zip(r_state, on_v, rbu)]
    u = [unstack(x) for x in u2]
    for s in range(sb):
        rows = slice(s * tc, (s + 1) * tc)
        upd = [_dot_tn(jnp.concatenate([vv[p][rows], -u[p][rows]], axis=0),
                       jnp.concatenate([k_end[p][rows], b_end[p][rows]], axis=0)) for p in pairs]
        for p in pairs:
            p_end = p_in[p][(s + 1) * tc - 1:(s + 1) * tc, :]
            s_scr[s, p] = s_scr[s, p] * p_end + jnp.where(bd_mask, upd[p], 0.0)

    mean = [x * (1.0 / RW_HD) for x in seg_sum(o)]
    d = _each(jnp.subtract, o, mean)
    var = [x * (1.0 / RW_HD) for x in seg_sum([x * x for x in d])]
    bonus = seg_sum([x * y * rkw_ref[:, c] for x, y, c in zip(rr, k2, cols)])
    for p in pairs:
        c = cols[p]
        on = d[p] * lax.rsqrt(var[p] + RW_GN_EPS) * gnw_ref[:, c] + gnb_ref[:, c]
        o_rw = (on + bonus[p] * vv[p]) * c_g[:, c]
        mix = _sigmoid(ga_ref[:, c]) * odn_ref[:, c] + _sigmoid(gb_ref[:, c]) * o_rw
        o_ref[:, c] = mix.astype(BF16)

    @pl.when(t == nt - 1)
    def _():
        shift_ref[:, :, 0:RW_W] = xbr[:, tc + SUBLANES - 1:tc + SUBLANES, :]
        shift_ref[:, :, RW_W:2 * RW_W] = xbk[:, tc + SUBLANES - 1:tc + SUBLANES, :]
        shift_ref[:, :, 2 * RW_W:3 * RW_W] = xbv[:, tc + SUBLANES - 1:tc + SUBLANES, :]
        shift_ref[:, :, 3 * RW_W:] = xbl[:, tc + SUBLANES - 1:tc + SUBLANES, :]
        for s in range(sb):
            for p in range(RW_PAIRS):
                sp = s_scr[s, p]
                sout_ref[s, 2 * p] = sp[0:RW_HD, 0:RW_HD]
                sout_ref[s, 2 * p + 1] = sp[RW_HD:, RW_HD:]


def _rw(p, o_dn, prev8, weights, s0, *, nb, seq, sb, tc):
    nt = seq // tc
    r = sb * tc
    assert seq % tc == 0 and nb % sb == 0 and (nt == 1 or sb == 1)
    bcast = s0.shape[0] == 1
    assert bcast == (prev8.shape[0] == 1) and (not bcast or sb == 1)
    (mu, w0, a0, kkw, kaw, rkw, gnw, gnb, w2p, a2p, g2p) = weights

    def st_idx(i):
        return 0 if bcast else i

    def vec(width):
        return pl.BlockSpec((1, width), lambda i, t: (0, 0))

    def lora_w():
        return pl.BlockSpec((RW_LORA, RW_W), lambda i, t: (0, 0))

    def rows(col):
        return pl.BlockSpec((r, RW_W), lambda i, t: (i * nt + t, col))

    rcol = COL_RW // RW_W
    gcol = COL_GATE // D_MODEL
    return pl.pallas_call(
        functools.partial(_rw_kernel, sb=sb, tc=tc, nt=nt),
        out_shape=(jax.ShapeDtypeStruct((nb * seq, RW_W), BF16),
                   jax.ShapeDtypeStruct((nb, RW_HEADS, RW_HD, RW_HD), F32),
                   jax.ShapeDtypeStruct((nb, 1, RW_SHIFT_W), F32)),
        grid=(nb // sb, nt),
        in_specs=[
            rows(rcol), rows(rcol + 1), rows(rcol + 2),
            pl.BlockSpec((r, RW_LORA), lambda i, t: (i * nt + t, (COL_RW + 3 * RW_W) // RW_LORA)),
            pl.BlockSpec((sb, SUBLANES, RW_SHIFT_W), lambda i, t: (st_idx(i), 0, 0)),
            vec(RW_SHIFT_W), vec(RW_W), vec(RW_W), vec(RW_W), vec(RW_W), vec(RW_W), vec(RW_W), vec(RW_W),
            lora_w(), lora_w(), lora_w(),
            pl.BlockSpec((sb, RW_HEADS, RW_HD, RW_HD), lambda i, t: (st_idx(i), 0, 0, 0)),
            rows(0), rows(gcol), rows(gcol + 1),
        ],
        out_specs=(
            rows(0),
            pl.BlockSpec((sb, RW_HEADS, RW_HD, RW_HD), lambda i, t: (i, 0, 0, 0)),
            pl.BlockSpec((sb, 1, RW_SHIFT_W), lambda i, t: (i, 0, 0)),
        ),
        scratch_shapes=[
            pltpu.VMEM((sb, tc + SUBLANES, RW_W), F32),
            pltpu.VMEM((sb, tc + SUBLANES, RW_W), F32),
            pltpu.VMEM((sb, tc + SUBLANES, RW_W), F32),
            pltpu.VMEM((sb, tc + SUBLANES, RW_LORA), F32),
        ] + [pltpu.VMEM((r, RW_W), F32)] * 7 + [
            pltpu.VMEM((sb, RW_PAIRS, LANES, LANES), F32),
        ],
        compiler_params=pltpu.CompilerParams(
            dimension_semantics=("arbitrary", "arbitrary"), vmem_limit_bytes=VMEM_LIMIT),
        name="rw",
    )(p, p, p, p, prev8, mu, w0, a0, kkw, kaw, rkw, gnw, gnb, w2p, a2p, g2p, s0, o_dn, p, p)


def _merge_kernel(x_ref, mix_ref, wout_ref, gffn_ref, w1_ref, w2_ref, gfin_ref,
                  y_ref, x1_scr, h2_scr, acc_scr, *, nf):
    f = pl.program_id(1)

    @pl.when(f == 0)
    def _():
        x1 = x_ref[...] + jnp.dot(mix_ref[...], wout_ref[...], preferred_element_type=F32)
        x1_scr[...] = x1
        h2 = x1 * lax.rsqrt(jnp.mean(x1 * x1, axis=-1, keepdims=True) + NORM_EPS) * gffn_ref[...]
        h2_scr[...] = h2.astype(BF16)
        acc_scr[...] = jnp.zeros_like(acc_scr)

    hid = jnp.dot(h2_scr[...], w1_ref[...], preferred_element_type=F32)
    hid = jnp.square(jnp.maximum(hid, 0.0))
    acc_scr[...] += jnp.dot(hid.astype(BF16), w2_ref[...], preferred_element_type=F32)

    @pl.when(f == nf - 1)
    def _():
        y = x1_scr[...] + acc_scr[...]
        y_ref[...] = y * lax.rsqrt(jnp.mean(y * y, axis=-1, keepdims=True) + NORM_EPS) * gfin_ref[...]


def _merge(x2d, mix, w_out, g_ffn, w1, w2, g_fin, *, tm, tf):
    n = x2d.shape[0]
    nf = D_FF // tf
    assert n % tm == 0 and D_FF % tf == 0
    row = lambda i, f: (i, 0)
    return pl.pallas_call(
        functools.partial(_merge_kernel, nf=nf),
        out_shape=jax.ShapeDtypeStruct((n, D_MODEL), F32),
        grid=(n // tm, nf),
        in_specs=[
            pl.BlockSpec((tm, D_MODEL), row),
            pl.BlockSpec((tm, D_MODEL), row),
            pl.BlockSpec((D_MODEL, D_MODEL), lambda i, f: (0, 0)),
            pl.BlockSpec((1, D_MODEL), lambda i, f: (0, 0)),
            pl.BlockSpec((D_MODEL, tf), lambda i, f: (0, f)),
            pl.BlockSpec((tf, D_MODEL), lambda i, f: (f, 0)),
            pl.BlockSpec((1, D_MODEL), lambda i, f: (0, 0)),
        ],
        out_specs=pl.BlockSpec((tm, D_MODEL), row),
        scratch_shapes=[
            pltpu.VMEM((tm, D_MODEL), F32),
            pltpu.VMEM((tm, D_MODEL), BF16),
            pltpu.VMEM((tm, D_MODEL), F32),
        ],
        compiler_params=pltpu.CompilerParams(
            dimension_semantics=("arbitrary", "arbitrary"), vmem_limit_bytes=VMEM_LIMIT),
        name="merge",
    )(x2d, mix, w_out, g_ffn, w1, w2, g_fin)


def _pad_rows_front(x, rows):
    b, n, w = x.shape
    return jnp.concatenate([jnp.zeros((b, rows - n, w), x.dtype), x], axis=1)


def _layer(x, conv_buf, dn_s, rw_prev, rw_s, wts, *, sb, tc, tm, tm_mlp):
    nb, seq, _ = x.shape
    n = nb * seq
    x2d = x.reshape(n, D_MODEL)
    p, gb = _proj(x2d, wts["g_mix"], wts["w_main"], wts["w_ab_t"], wts["a_log"], wts["dt_bias"],
                  tm=tm, rc=sb * tc)
    o_dn, dn_new, conv_new = _dn(p, gb, _pad_rows_front(conv_buf, SUBLANES), wts["conv_w"], wts["dn_norm_w"],
                                 dn_s, nb=nb, seq=seq, sb=sb, tc=tc)
    mix, rw_new, shift_new = _rw(p, o_dn, _pad_rows_front(rw_prev[:, None, :], SUBLANES), wts["rw"], rw_s,
                                 nb=nb, seq=seq, sb=sb, tc=tc)
    y = _merge(x2d, mix, wts["w_out"], wts["g_ffn"], wts["w_ff1"], wts["w_ff2"], wts["g_final"],
               tm=tm_mlp, tf=1024)
    return y.reshape(nb, seq, D_MODEL), conv_new, dn_new, shift_new[:, 0], rw_new


def kernel(x_prompt, x_sample, state_dn_conv, state_dn, state_rw_shift, state_rw, meta_tokens, g_mix_norm, w_in, dn_conv_w, dn_a_log, dn_dt_bias, dn_norm_w, rw_mu, rw_w0, rw_w2, rw_a0, rw_a2, rw_g2, rw_k_k, rw_k_a, rw_r_k, rw_gn_w, rw_gn_b, w_out, g_ffn_norm, w_ff1, w_ff2, g_final):
    assert g_mix_norm.shape[0] == 1, "single layer"
    w = w_in[0]
    o_z, o_a = DN_QKV, DN_QKV + D_MODEL
    o_rwp = o_a + 2 * DN_HEADS
    o_gate = o_rwp + RW_SHIFT_W
    w_main = jnp.concatenate([w[:, :o_a], w[:, o_gate:], w[:, o_rwp:o_gate]], axis=1).astype(BF16)
    w_ab_t = w[:, o_a:o_rwp].T.astype(BF16)

    def lora_pad(wl, lo):
        return jnp.zeros((RW_LORA, RW_W), F32).at[lo:lo + wl.shape[0]].set(wl).astype(BF16)

    row = lambda v: v.reshape(1, -1).astype(F32)
    wts = {
        "g_mix": row(g_mix_norm[0]), "w_main": w_main, "w_ab_t": w_ab_t,
        "a_log": dn_a_log[0].reshape(DN_HEADS, 1), "dt_bias": dn_dt_bias[0].reshape(DN_HEADS, 1),
        "conv_w": dn_conv_w[0], "dn_norm_w": row(dn_norm_w[0]),
        "rw": (row(rw_mu[0]), row(rw_w0[0]), row(rw_a0[0]), row(rw_k_k[0]), row(rw_k_a[0]), row(rw_r_k[0]),
               row(rw_gn_w[0]), row(rw_gn_b[0]),
               lora_pad(rw_w2[0], 0), lora_pad(rw_a2[0], 64), lora_pad(rw_g2[0], 128)),
        "w_out": w_out[0].astype(BF16), "g_ffn": row(g_ffn_norm[0]),
        "w_ff1": w_ff1[0].astype(BF16), "w_ff2": w_ff2[0].astype(BF16), "g_final": row(g_final),
    }
    nbp = x_prompt.shape[0]

    nbm = SUBLANES
    xm = jnp.broadcast_to(meta_tokens.astype(F32)[None], (nbm, N_META, D_MODEL))
    _, conv_m, dn_m, shift_m, rw_m = _layer(
        xm, jnp.zeros((nbm, 3, DN_QKV), F32), jnp.zeros((nbm, DN_HEADS, DN_D, DN_D), F32),
        jnp.zeros((nbm, RW_SHIFT_W), F32), jnp.zeros((nbm, RW_HEADS, RW_HD, RW_HD), F32), wts,
        sb=nbm, tc=N_META, tm=nbm * N_META, tm_mlp=nbm * N_META)

    y_p, conv_p, dn_p, shift_p, rw_p = _layer(
        x_prompt, conv_m[:1], dn_m[:1], shift_m[:1], rw_m[:1], wts, sb=1, tc=64, tm=512, tm_mlp=1024)

    dec_len = x_sample.shape[1]
    y_s, conv_s, dn_s, shift_s, rw_s = _layer(
        x_sample, state_dn_conv[0], state_dn[0], state_rw_shift[0], state_rw[0], wts,
        sb=SUBLANES, tc=dec_len, tm=512, tm_mlp=1024)

    del nbp
    return (y_p, y_s, conv_p[None], dn_p[None], shift_p[None], rw_p[None],
            conv_s[None], dn_s[None], shift_s[None], rw_s[None])
```

```python
import functools
import math

import jax
import jax.numpy as jnp
from jax import lax
from jax.experimental import pallas as pl
from jax.experimental.pallas import tpu as pltpu

F32 = jnp.float32
BF16 = jnp.bfloat16

D_MODEL = 1024
N_META = 16
DN_HEADS = 8
DN_D = 128
DN_QKV = 3 * DN_HEADS * DN_D
RW_HEADS = 16
RW_HD = 64
RW_PAIRS = RW_HEADS // 2
RW_W = RW_HEADS * RW_HD
RW_LORA = 256
RW_SHIFT_W = 3 * RW_W + RW_LORA
D_FF = 4 * D_MODEL
NORM_EPS = 1e-6
RW_GN_EPS = 64e-5
LANES = 128
SUBLANES = 8
INV_BLOCK = 16

COL_QKV = 0
COL_Z = DN_QKV
COL_GATE = COL_Z + D_MODEL
COL_RW = COL_GATE + 2 * D_MODEL
P_W = COL_RW + RW_SHIFT_W
P_HALF = P_W // 2

VMEM_LIMIT = 56 * 1024 * 1024


def _sigmoid(x):
    return 1.0 / (1.0 + jnp.exp(-x))


def _silu(x):
    return x * _sigmoid(x)


def _softplus(x):
    return jnp.maximum(x, 0.0) + jnp.log(1.0 + jnp.exp(-jnp.abs(x)))


def _dot(a, b):
    return jnp.dot(a.astype(BF16), b.astype(BF16), preferred_element_type=F32)


def _dot_nt(a, b):
    return lax.dot_general(a.astype(BF16), b.astype(BF16), (((1,), (1,)), ((), ())),
                           preferred_element_type=F32)


def _dot_tn(a, b):
    return lax.dot_general(a.astype(BF16), b.astype(BF16), (((0,), (0,)), ((), ())),
                           preferred_element_type=F32)


def _rows(xs):
    return xs[0] if len(xs) == 1 else jnp.concatenate(xs, axis=0)


def _dot_rows(xs, b, dot=_dot):
    out = dot(_rows(xs), b)
    off, parts = 0, []
    for x in xs:
        parts.append(out[off:off + x.shape[0]])
        off += x.shape[0]
    return parts


def _dot_mask(mask_bf16, x):
    h1 = x.astype(BF16)
    r1 = x - h1.astype(F32)
    h2 = r1.astype(BF16)
    h3 = (r1 - h2.astype(F32)).astype(BF16)
    return (jnp.dot(mask_bf16, h1, preferred_element_type=F32)
            + (jnp.dot(mask_bf16, h2, preferred_element_type=F32)
               + jnp.dot(mask_bf16, h3, preferred_element_type=F32)))


def _iota2(n, m):
    return (lax.broadcasted_iota(jnp.int32, (n, m), 0),
            lax.broadcasted_iota(jnp.int32, (n, m), 1))


def _group(idx, size):
    return lax.shift_right_logical(idx, int(math.log2(size)))


def _lockstep(chains):
    chains = list(chains)
    results = [None] * len(chains)
    active = list(range(len(chains)))
    while active:
        for i in list(active):
            try:
                next(chains[i])
            except StopIteration as done:
                results[i] = done.value
                active.remove(i)
        if active:
            yield
    return results


def _run(chain):
    for _ in chain:
        pass


def _neumann_inverse(a, eye, nil):
    inv = eye - a
    if nil <= 2:
        return inv
    power = _dot(a, a)
    yield
    k = 4
    while k < nil:
        step, power = _dot_rows([inv, power], power)
        yield
        inv = inv + step
        k *= 2
    step = _dot(inv, power)
    yield
    return inv + step


def _unit_lower_inverse(a, ri, ci, tc):
    eye = jnp.where(ri == ci, 1.0, 0.0).astype(F32)
    if tc <= INV_BLOCK:
        return (yield from _neumann_inverse(a, eye, tc))
    assert tc // INV_BLOCK <= 4
    diag = _group(ri, INV_BLOCK) == _group(ci, INV_BLOCK)
    d = jnp.where(diag, a, 0.0)
    low = a - d
    dinv = yield from _neumann_inverse(d, eye, INV_BLOCK)
    b = _dot(low, dinv)
    yield
    db, bb = _dot_rows([dinv, b], b)
    yield
    p1 = dinv - db
    tail = _dot(p1, bb)
    yield
    return p1 + tail


def _proj_kernel(x_ref, g_ref, w_ref, wab_ref, alog_ref, dtb_ref, p_ref, gb_ref, *, tm, rc):
    x = x_ref[...]
    h = x * lax.rsqrt(jnp.mean(x * x, axis=-1, keepdims=True) + NORM_EPS) * g_ref[...]
    hb = h.astype(BF16)
    for c0 in range(0, P_HALF, 512):
        cw = min(512, P_HALF - c0)
        p_ref[:, c0:c0 + cw] = jnp.dot(hb, w_ref[:, c0:c0 + cw], preferred_element_type=F32)
    @pl.when(pl.program_id(0) == 0)
    def _():
        for c in range(tm // rc):
            ab = lax.dot_general(wab_ref[...], hb[c * rc:(c + 1) * rc], (((1,), (1,)), ((), ())),
                                 preferred_element_type=F32)
            g = -jnp.exp(alog_ref[...]) * _softplus(ab[0:DN_HEADS] + dtb_ref[...])
            gb_ref[c, 0:DN_HEADS, :] = g
            gb_ref[c, DN_HEADS:2 * DN_HEADS, :] = _sigmoid(ab[DN_HEADS:2 * DN_HEADS])


def _proj(x2d, g_norm, w_main, w_ab_t, a_log, dt_bias, *, tm, rc):
    n = x2d.shape[0]
    assert n % tm == 0 and tm % rc == 0
    return pl.pallas_call(
        functools.partial(_proj_kernel, tm=tm, rc=rc),
        out_shape=(jax.ShapeDtypeStruct((n, P_W), F32),
                   jax.ShapeDtypeStruct((n // rc, 2 * DN_HEADS, rc), F32)),
        grid=(2, n // tm),
        in_specs=[
            pl.BlockSpec((tm, D_MODEL), lambda j, i: (i, 0)),
            pl.BlockSpec((1, D_MODEL), lambda j, i: (0, 0)),
            pl.BlockSpec((D_MODEL, P_HALF), lambda j, i: (0, j)),
            pl.BlockSpec((2 * DN_HEADS, D_MODEL), lambda j, i: (0, 0)),
            pl.BlockSpec((DN_HEADS, 1), lambda j, i: (0, 0)),
            pl.BlockSpec((DN_HEADS, 1), lambda j, i: (0, 0)),
        ],
        out_specs=(
            pl.BlockSpec((tm, P_HALF), lambda j, i: (i, j)),
            pl.BlockSpec((tm // rc, 2 * DN_HEADS, rc),
                         lambda j, i: (i * (1 - j) + (n // tm - 1) * j, 0, 0)),
        ),
        compiler_params=pltpu.CompilerParams(
            dimension_semantics=("arbitrary", "arbitrary"), vmem_limit_bytes=VMEM_LIMIT),
        name="proj",
    )(x2d, g_norm, w_main, w_ab_t, a_log, dt_bias)


def _dn_kernel(qkv_ref, z_ref, gb_ref, cb_ref, cw_ref, nw_ref, s0_ref,
               o_ref, sout_ref, cout_ref, xb, s_scr, *, sb, tc, nc, nt):
    t = pl.program_id(1)
    tt = nc * tc
    ru = sb * tc
    hw = DN_HEADS * DN_D

    @pl.when(t == 0)
    def _():
        s_scr[...] = s0_ref[...]
        xb[:, 0:SUBLANES, :] = cb_ref[...]

    @pl.when(t > 0)
    def _():
        xb[:, 0:SUBLANES, :] = xb[:, tt:tt + SUBLANES, :]

    xb[:, SUBLANES:, :] = qkv_ref[...].reshape(sb, tt, DN_QKV)

    def conv(cols):
        acc = xb[:, 5:5 + tt, cols] * cw_ref[0:1, cols]
        acc = acc + xb[:, 6:6 + tt, cols] * cw_ref[1:2, cols]
        acc = acc + xb[:, 7:7 + tt, cols] * cw_ref[2:3, cols]
        acc = acc + xb[:, 8:8 + tt, cols] * cw_ref[3:4, cols]
        return _silu(acc).reshape(sb * tt, DN_D)

    ri, ci = _iota2(ru, ru)
    same = _group(ri, tc) == _group(ci, tc)
    incl = same & (ri >= ci)
    strict = same & (ri > ci)
    eye = ri == ci
    seqs = [slice(s * tc, (s + 1) * tc) for s in range(sb)]
    units = [slice(c * ru, (c + 1) * ru) for c in range(nc)]

    def unit_chain(h, rows, q, k, v):
        g_row = gb_ref[0, h:h + 1, :][:, rows]
        b_row = gb_ref[0, DN_HEADS + h:DN_HEADS + h + 1, :][:, rows]
        g_col = jnp.sum(jnp.where(incl, g_row, 0.0), axis=1, keepdims=True)
        g_cum_row = jnp.sum(jnp.where(eye, g_col, 0.0), axis=0, keepdims=True)
        g_tot = jnp.sum(jnp.where(same, g_row, 0.0), axis=1, keepdims=True)
        b_col = jnp.sum(jnp.where(eye, b_row, 0.0), axis=1, keepdims=True)
        decay = jnp.where(incl, jnp.exp(jnp.minimum(g_col - g_cum_row, 0.0)), 0.0)
        e_g = jnp.exp(g_col)
        e_rest = jnp.exp(g_tot - g_col)
        kb = k * b_col
        a_raw, qk_raw = _dot_rows([kb, q], k, dot=_dot_nt)
        yield
        a = jnp.where(strict, a_raw * decay, 0.0)
        aqk = qk_raw * decay
        t_inv = yield from _unit_lower_inverse(a, ri, ci, tc)
        rhs = jnp.concatenate([v * b_col, kb * e_g], axis=1)
        sol = _dot(t_inv, rhs)
        yield
        return dict(u_all=sol[:, :DN_D], w_all=sol[:, DN_D:], aqk=aqk, qg=q * e_g, kd=k * e_rest,
                    g_tot=g_tot)

    def head_chain(h):
        cols = slice(h * DN_D, (h + 1) * DN_D)
        q = conv(cols)
        k = conv(slice(hw + h * DN_D, hw + (h + 1) * DN_D))
        v = conv(slice(2 * hw + h * DN_D, 2 * hw + (h + 1) * DN_D))
        q = q * lax.rsqrt(jnp.sum(q * q, axis=-1, keepdims=True) + 1e-6) * (DN_D ** -0.5)
        k = k * lax.rsqrt(jnp.sum(k * k, axis=-1, keepdims=True) + 1e-6)
        wy = yield from _lockstep(unit_chain(h, rows, q[rows], k[rows], v[rows]) for rows in units)
        for rows, w in zip(units, wy):
            ws = [_dot(jnp.concatenate([w["w_all"][sq], w["qg"][sq]], axis=0), s_scr[s, h])
                  for s, sq in enumerate(seqs)]
            yield
            u = _rows([w["u_all"][sq] - x[:tc] for sq, x in zip(seqs, ws)])
            q_s = _rows([x[tc:] for x in ws])
            intra = _dot(w["aqk"], u)
            upd = [_dot_tn(w["kd"][sq], u[sq]) for sq in seqs]
            yield
            for s in range(sb):
                gl = jnp.exp(w["g_tot"][s * tc:s * tc + 1, :])
                s_scr[s, h] = s_scr[s, h] * gl + upd[s]
            o = q_s + intra
            o = o * lax.rsqrt(jnp.mean(o * o, axis=-1, keepdims=True) + NORM_EPS) * nw_ref[...]
            o_ref[rows, cols] = o * _silu(z_ref[rows, cols])

    _run(_lockstep(head_chain(h) for h in range(DN_HEADS)))

    @pl.when(t == nt - 1)
    def _():
        sout_ref[...] = s_scr[...]
        cout_ref[...] = xb[:, tt + SUBLANES - 3:tt + SUBLANES, :]


def _dn(p, gb, conv_buf8, conv_w, norm_w, s0, *, nb, seq, sb, tc, nc):
    nt = seq // (nc * tc)
    r = sb * nc * tc
    assert seq % (nc * tc) == 0 and nb % sb == 0 and (nt == 1 or sb == 1) and (nc == 1 or sb == 1)
    bcast = s0.shape[0] == 1
    assert bcast == (conv_buf8.shape[0] == 1) and (not bcast or sb == 1)
    hw = DN_HEADS * DN_D

    def st_idx(i):
        return 0 if bcast else i

    return pl.pallas_call(
        functools.partial(_dn_kernel, sb=sb, tc=tc, nc=nc, nt=nt),
        out_shape=(jax.ShapeDtypeStruct((nb * seq, hw), F32),
                   jax.ShapeDtypeStruct((nb, DN_HEADS, DN_D, DN_D), F32),
                   jax.ShapeDtypeStruct((nb, 3, DN_QKV), F32)),
        grid=(nb // sb, nt),
        in_specs=[
            pl.BlockSpec((r, DN_QKV), lambda i, t: (i * nt + t, COL_QKV // DN_QKV)),
            pl.BlockSpec((r, hw), lambda i, t: (i * nt + t, COL_Z // hw)),
            pl.BlockSpec((1, 2 * DN_HEADS, r), lambda i, t: (i * nt + t, 0, 0)),
            pl.BlockSpec((sb, SUBLANES, DN_QKV), lambda i, t: (st_idx(i), 0, 0)),
            pl.BlockSpec((4, DN_QKV), lambda i, t: (0, 0)),
            pl.BlockSpec((1, DN_D), lambda i, t: (0, 0)),
            pl.BlockSpec((sb, DN_HEADS, DN_D, DN_D), lambda i, t: (st_idx(i), 0, 0, 0)),
        ],
        out_specs=(
            pl.BlockSpec((r, hw), lambda i, t: (i * nt + t, 0)),
            pl.BlockSpec((sb, DN_HEADS, DN_D, DN_D), lambda i, t: (i, 0, 0, 0)),
            pl.BlockSpec((sb, 3, DN_QKV), lambda i, t: (i, 0, 0)),
        ),
        scratch_shapes=[
            pltpu.VMEM((sb, nc * tc + SUBLANES, DN_QKV), F32),
            pltpu.VMEM((sb, DN_HEADS, DN_D, DN_D), F32),
        ],
        compiler_params=pltpu.CompilerParams(
            dimension_semantics=("arbitrary", "arbitrary"), vmem_limit_bytes=VMEM_LIMIT),
        name="dn",
    )(p, p, gb, conv_buf8, conv_w, norm_w, s0)


def _rw_kernel(r_ref, k_ref, v_ref, l_ref, prev_ref, mu_ref, w0_ref, a0_ref, kkw_ref, kaw_ref, rkw_ref,
               gnw_ref, gnb_ref, w2_ref, a2_ref, g2_ref, s0_ref, odn_ref, ga_ref, gb_ref,
               o_ref, sout_ref, shift_ref,
               xbr, xbk, xbv, xbl, s_scr, *, sb, tc, nc, nt):
    t = pl.program_id(1)
    tt = nc * tc
    ru = sb * tc
    half =lax.broadcasted_iota(jnp.int32, (1, LANES), 1) < RW_HD
    ri128, ci128 = _iota2(LANES, LANES)
    bd_mask = _group(ri128, RW_HD) == _group(ci128, RW_HD)

    def seg_sum(x):
        first = jnp.sum(jnp.where(half, x, 0.0), axis=-1, keepdims=True)
        second = jnp.sum(jnp.where(half, 0.0, x), axis=-1, keepdims=True)
        return jnp.where(half, first, second)

    @pl.when(t == 0)
    def _():
        xbr[:, 0:SUBLANES, :] = prev_ref[:, :, 0:RW_W]
        xbk[:, 0:SUBLANES, :] = prev_ref[:, :, RW_W:2 * RW_W]
        xbv[:, 0:SUBLANES, :] = prev_ref[:, :, 2 * RW_W:3 * RW_W]
        xbl[:, 0:SUBLANES, :] = prev_ref[:, :, 3 * RW_W:]
        zero = jnp.zeros((RW_HD, RW_HD), F32)
        for s in range(sb):
            for p in range(RW_PAIRS):
                top = jnp.concatenate([s0_ref[s, 2 * p], zero], axis=1)
                bot = jnp.concatenate([zero, s0_ref[s, 2 * p + 1]], axis=1)
                s_scr[s, p] = jnp.concatenate([top, bot], axis=0)

    @pl.when(t > 0)
    def _():
        for xb in (xbr, xbk, xbv, xbl):
            xb[:, 0:SUBLANES, :] = xb[:, tt:tt + SUBLANES, :]

    xbr[:, SUBLANES:, :] = r_ref[...].reshape(sb, tt, RW_W)
    xbk[:, SUBLANES:, :] = k_ref[...].reshape(sb, tt, RW_W)
    xbv[:, SUBLANES:, :] = v_ref[...].reshape(sb, tt, RW_W)
    xbl[:, SUBLANES:, :] = l_ref[...].reshape(sb, tt, RW_LORA)

    def shift_mix(xb, lo, cols):
        x = xb[:, SUBLANES:, cols]
        prev = xb[:, SUBLANES - 1:SUBLANES - 1 + tt, cols]
        mu = mu_ref[:, lo + cols.start:lo + cols.stop]
        return (x + (prev - x) * mu).reshape(sb * tt, cols.stop - cols.start)

    xm_l = shift_mix(xbl, 3 * RW_W, slice(0, RW_LORA))
    lora_w = jnp.tanh(xm_l).astype(BF16)
    lora_a = xm_l.astype(BF16)
    lora_g = _sigmoid(xm_l).astype(BF16)

    ri, ci = _iota2(ru, ru)
    cum_mask = jnp.where((_group(ri, tc) == _group(ci, tc)) & (ri >= ci), 1.0, 0.0).astype(BF16)
    r2 = 2 * ru
    ri2, ci2 = _iota2(r2, r2)
    same2 = _group(ri2, tc) == _group(ci2, tc)
    incl2 = same2 & (ri2 >= ci2)
    strict2 = same2 & (ri2 > ci2)
    seqs = [slice(s * tc, (s + 1) * tc) for s in range(sb)]
    units = [slice(c * ru, (c + 1) * ru) for c in range(nc)]

    def stack(x):
        return jnp.concatenate([jnp.where(half, x, 0.0), jnp.where(half, 0.0, x)], axis=0)

    def unstack(x2):
        return x2[:ru] + x2[ru:]

    def unit_chain(rr, k2, vv, kk, bb, logw):
        lcum = _dot_mask(cum_mask, logw)
        yield
        ltot = jnp.broadcast_to(lcum.reshape(sb, tc, LANES)[:, tc - 1:tc, :],
                                (sb, tc, LANES)).reshape(ru, LANES)
        p_in = jnp.exp(lcum)
        p_inv = jnp.exp(-lcum)
        alpha = kk * jnp.exp(lcum - logw)
        beta = bb * p_inv
        kt = k2 * p_inv
        rt = rr * p_in
        e_rest = jnp.exp(ltot - lcum)
        alpha2, rt2, beta2, kt2, v2 = stack(alpha), stack(rt), stack(beta), stack(kt), stack(vv)
        l_raw, rb_raw = _dot_rows([alpha2, rt2], beta2, dot=_dot_nt)
        lk_raw, rk_raw = _dot_rows([alpha2, rt2], kt2, dot=_dot_nt)
        yield
        lmat = jnp.where(strict2, l_raw, 0.0)
        rb = jnp.where(incl2, rb_raw, 0.0)
        lk = jnp.where(strict2, lk_raw, 0.0)
        rk = jnp.where(incl2, rk_raw, 0.0)
        lkv, rkv = _dot_rows([lk, rk], v2)
        t_inv = yield from _unit_lower_inverse(lmat, ri2, ci2, tc)
        sol = _dot(t_inv, jnp.concatenate([alpha2, lkv], axis=1))
        yield
        return dict(wa2=sol[:, :LANES], uv2=sol[:, LANES:], rt=rt, rb=rb, rkv=rkv, p_in=p_in,
                    k_end=k2 * e_rest, b_end=bb * e_rest)

    def pair_chain(p):
        c = slice(p * LANES, (p + 1) * LANES)
        rr = shift_mix(xbr, 0, c)
        xk = shift_mix(xbk, RW_W, c)
        vv = shift_mix(xbv, 2 * RW_W, c)
        lw = jnp.dot(lora_w, w2_ref[:, c], preferred_element_type=F32)
        la = jnp.dot(lora_a, a2_ref[:, c], preferred_element_type=F32)
        gate = jnp.dot(lora_g, g2_ref[:, c], preferred_element_type=F32)
        yield
        w_log = -_softplus(-(w0_ref[:, c] + lw)) - 0.5
        logw = -jnp.exp(w_log)
        aa = _sigmoid(a0_ref[:, c] + la)
        kk = xk * kkw_ref[:, c]
        k2 = xk * (1.0 + (aa - 1.0) * kaw_ref[:, c])
        kk = kk * lax.rsqrt(seg_sum(kk * kk) + 1e-6)
        bb = kk * aa
        wy = yield from _lockstep(
            unit_chain(rr[rows], k2[rows], vv[rows], kk[rows], bb[rows], logw[rows]) for rows in units)
        for rows, w in zip(units, wy):
            st = [_dot_nt(jnp.concatenate([w["wa2"][sq], w["wa2"][ru + s * tc:ru + (s + 1) * tc], w["rt"][sq]],
                                          axis=0), s_scr[s, p]) for s, sq in enumerate(seqs)]
            yield
            u2 = w["uv2"] + _rows([x[:tc] for x in st] + [x[tc:2 * tc] for x in st])
            r_state = _rows([x[2 * tc:] for x in st])
            rbu = _dot(w["rb"], u2)
            u = unstack(u2)
            v_u = vv[rows]
            upd = [_dot_tn(jnp.concatenate([v_u[sq], -u[sq]], axis=0),
                           jnp.concatenate([w["k_end"][sq], w["b_end"][sq]], axis=0)) for sq in seqs]
            yield
            for s in range(sb):
                p_end = w["p_in"][(s + 1) * tc - 1:(s + 1) * tc, :]
                s_scr[s, p] = s_scr[s, p] * p_end + jnp.where(bd_mask, upd[s], 0.0)
            o = r_state + unstack(w["rkv"] - rbu)
            mean = seg_sum(o) * (1.0 / RW_HD)
            d = o - mean
            var = seg_sum(d * d) * (1.0 / RW_HD)
            bonus = seg_sum(rr[rows] * k2[rows] * rkw_ref[:, c])
            on = d * lax.rsqrt(var + RW_GN_EPS) * gnw_ref[:, c] + gnb_ref[:, c]
            o_rw = (on + bonus * v_u) * gate[rows]
            mix = _sigmoid(ga_ref[rows, c]) * odn_ref[rows, c] + _sigmoid(gb_ref[rows, c]) * o_rw
            o_ref[rows, c] = mix.astype(BF16)

    _run(_lockstep(pair_chain(p) for p in range(RW_PAIRS)))

    @pl.when(t == nt - 1)
    def _():
        shift_ref[:, :, 0:RW_W] = xbr[:, tt + SUBLANES - 1:tt + SUBLANES, :]
        shift_ref[:, :, RW_W:2 * RW_W] = xbk[:, tt + SUBLANES - 1:tt + SUBLANES, :]
        shift_ref[:, :, 2 * RW_W:3 * RW_W] = xbv[:, tt + SUBLANES - 1:tt + SUBLANES, :]
        shift_ref[:, :, 3 * RW_W:] = xbl[:, tt + SUBLANES - 1:tt + SUBLANES, :]
        for s in range(sb):
            for p in range(RW_PAIRS):
                sp = s_scr[s, p]
                sout_ref[s, 2 * p] = sp[0:RW_HD, 0:RW_HD]
                sout_ref[s, 2 * p + 1] = sp[RW_HD:, RW_HD:]


def _rw(p, o_dn, prev8, weights, s0, *, nb, seq, sb, tc, nc):
    nt = seq // (nc * tc)
    r = sb * nc * tc
    assert seq % (nc * tc) == 0 and nb % sb == 0 and (nt == 1 or sb == 1) and (nc == 1 or sb == 1)
    bcast = s0.shape[0] == 1
    assert bcast == (prev8.shape[0] == 1) and (not bcast or sb == 1)
    (mu, w0, a0, kkw, kaw, rkw, gnw, gnb, w2p, a2p, g2p) = weights

    def st_idx(i):
        return 0 if bcast else i

    def vec(width):
        return pl.BlockSpec((1, width), lambda i, t: (0, 0))

    def lora_w():
        return pl.BlockSpec((RW_LORA, RW_W), lambda i, t: (0, 0))

    def rows(col):
        return pl.BlockSpec((r, RW_W), lambda i, t: (i * nt + t, col))

    rcol = COL_RW // RW_W
    gcol = COL_GATE // D_MODEL
    return pl.pallas_call(
        functools.partial(_rw_kernel, sb=sb, tc=tc, nc=nc, nt=nt),
        out_shape=(jax.ShapeDtypeStruct((nb * seq, RW_W), BF16),
                   jax.ShapeDtypeStruct((nb, RW_HEADS, RW_HD, RW_HD), F32),
                   jax.ShapeDtypeStruct((nb, 1, RW_SHIFT_W), F32)),
        grid=(nb // sb, nt),
        in_specs=[
            rows(rcol), rows(rcol + 1), rows(rcol + 2),
            pl.BlockSpec((r, RW_LORA), lambda i, t: (i * nt + t, (COL_RW + 3 * RW_W) // RW_LORA)),
            pl.BlockSpec((sb, SUBLANES, RW_SHIFT_W), lambda i, t: (st_idx(i), 0, 0)),
            vec(RW_SHIFT_W), vec(RW_W), vec(RW_W), vec(RW_W), vec(RW_W), vec(RW_W), vec(RW_W), vec(RW_W),
            lora_w(), lora_w(), lora_w(),
            pl.BlockSpec((sb, RW_HEADS, RW_HD, RW_HD), lambda i, t: (st_idx(i), 0, 0, 0)),
            rows(0), rows(gcol), rows(gcol + 1),
        ],
        out_specs=(
            rows(0),
            pl.BlockSpec((sb, RW_HEADS, RW_HD, RW_HD), lambda i, t: (i, 0, 0, 0)),
            pl.BlockSpec((sb, 1, RW_SHIFT_W), lambda i, t: (i, 0, 0)),
        ),
        scratch_shapes=[
            pltpu.VMEM((sb, nc * tc + SUBLANES, RW_W), F32),
            pltpu.VMEM((sb, nc * tc + SUBLANES, RW_W), F32),
            pltpu.VMEM((sb, nc * tc + SUBLANES, RW_W), F32),
            pltpu.VMEM((sb, nc * tc + SUBLANES, RW_LORA), F32),
            pltpu.VMEM((sb, RW_PAIRS, LANES, LANES), F32),
        ],
        compiler_params=pltpu.CompilerParams(
            dimension_semantics=("arbitrary", "arbitrary"), vmem_limit_bytes=VMEM_LIMIT),
        name="rw",
    )(p, p, p, p, prev8, mu, w0, a0, kkw, kaw, rkw, gnw, gnb, w2p, a2p, g2p, s0, o_dn, p, p)


def _merge_kernel(x_ref, mix_ref, wout_ref, gffn_ref, w1_ref, w2_ref, gfin_ref,
                  y_ref, x1_scr, h2_scr, acc_scr, *, nf):
    f = pl.program_id(1)

    @pl.when(f == 0)
    def _():
        x1 = x_ref[...] + jnp.dot(mix_ref[...], wout_ref[...], preferred_element_type=F32)
        x1_scr[...] = x1
        h2 = x1 * lax.rsqrt(jnp.mean(x1 * x1, axis=-1, keepdims=True) + NORM_EPS) * gffn_ref[...]
        h2_scr[...] = h2.astype(BF16)
        acc_scr[...] = jnp.zeros_like(acc_scr)

    hid = jnp.dot(h2_scr[...], w1_ref[...], preferred_element_type=F32)
    hid = jnp.square(jnp.maximum(hid, 0.0))
    acc_scr[...] += jnp.dot(hid.astype(BF16), w2_ref[...], preferred_element_type=F32)

    @pl.when(f == nf - 1)
    def _():
        y = x1_scr[...] + acc_scr[...]
        y_ref[...] = y * lax.rsqrt(jnp.mean(y * y, axis=-1, keepdims=True) + NORM_EPS) * gfin_ref[...]


def _merge(x2d, mix, w_out, g_ffn, w1, w2, g_fin, *, tm, tf):
    n = x2d.shape[0]
    nf = D_FF // tf
    assert n % tm == 0 and D_FF % tf == 0
    row = lambda i, f: (i, 0)
    return pl.pallas_call(
        functools.partial(_merge_kernel, nf=nf),
        out_shape=jax.ShapeDtypeStruct((n, D_MODEL), F32),
        grid=(n // tm, nf),
        in_specs=[
            pl.BlockSpec((tm, D_MODEL), row),
            pl.BlockSpec((tm, D_MODEL), row),
            pl.BlockSpec((D_MODEL, D_MODEL), lambda i, f: (0, 0)),
            pl.BlockSpec((1, D_MODEL), lambda i, f: (0, 0)),
            pl.BlockSpec((D_MODEL, tf), lambda i, f: (0, f)),
            pl.BlockSpec((tf, D_MODEL), lambda i, f: (f, 0)),
            pl.BlockSpec((1, D_MODEL), lambda i, f: (0, 0)),
        ],
        out_specs=pl.BlockSpec((tm, D_MODEL), row),
        scratch_shapes=[
            pltpu.VMEM((tm, D_MODEL), F32),
            pltpu.VMEM((tm, D_MODEL), BF16),
            pltpu.VMEM((tm, D_MODEL), F32),
        ],
        compiler_params=pltpu.CompilerParams(
            dimension_semantics=("arbitrary", "arbitrary"), vmem_limit_bytes=VMEM_LIMIT),
        name="merge",
    )(x2d, mix, w_out, g_ffn, w1, w2, g_fin)


def _pad_rows_front(x, rows):
    b, n, w = x.shape
    return jnp.concatenate([jnp.zeros((b, rows - n, w), x.dtype), x], axis=1)


def _layer(x, conv_buf, dn_s, rw_prev, rw_s, wts, *, sb, tc, nc, tm, tm_mlp):
    nb, seq, _ = x.shape
    n = nb * seq
    x2d = x.reshape(n, D_MODEL)
    p, gb = _proj(x2d, wts["g_mix"], wts["w_main"], wts["w_ab_t"], wts["a_log"], wts["dt_bias"],
                  tm=tm, rc=sb * nc * tc)
    o_dn, dn_new, conv_new = _dn(p, gb, _pad_rows_front(conv_buf, SUBLANES), wts["conv_w"], wts["dn_norm_w"],
                                 dn_s, nb=nb, seq=seq, sb=sb, tc=tc, nc=nc)
    mix, rw_new, shift_new = _rw(p, o_dn, _pad_rows_front(rw_prev[:, None, :], SUBLANES), wts["rw"], rw_s,
                                 nb=nb, seq=seq, sb=sb, tc=tc, nc=nc)
    y = _merge(x2d, mix, wts["w_out"], wts["g_ffn"], wts["w_ff1"], wts["w_ff2"], wts["g_final"],
               tm=tm_mlp, tf=1024)
    return y.reshape(nb, seq, D_MODEL), conv_new, dn_new, shift_new[:, 0], rw_new


def kernel(x_prompt, x_sample, state_dn_conv, state_dn, state_rw_shift, state_rw, meta_tokens, g_mix_norm, w_in, dn_conv_w, dn_a_log, dn_dt_bias, dn_norm_w, rw_mu, rw_w0, rw_w2, rw_a0, rw_a2, rw_g2, rw_k_k, rw_k_a, rw_r_k, rw_gn_w, rw_gn_b, w_out, g_ffn_norm, w_ff1, w_ff2, g_final):
    assert g_mix_norm.shape[0] == 1, "single layer"
    w = w_in[0]
    o_a = DN_QKV + D_MODEL
    o_rwp = o_a + 2 * DN_HEADS
    o_gate = o_rwp + RW_SHIFT_W
    w_main = jnp.concatenate([w[:, :o_a], w[:, o_gate:], w[:, o_rwp:o_gate]], axis=1).astype(BF16)
    w_ab_t = w[:, o_a:o_rwp].T.astype(BF16)

    def lora_pad(wl, lo):
        return jnp.zeros((RW_LORA, RW_W), F32).at[lo:lo + wl.shape[0]].set(wl).astype(BF16)

    row = lambda v: v.reshape(1, -1).astype(F32)
    wts = {
        "g_mix": row(g_mix_norm[0]), "w_main": w_main, "w_ab_t": w_ab_t,
        "a_log": dn_a_log[0].reshape(DN_HEADS, 1), "dt_bias": dn_dt_bias[0].reshape(DN_HEADS, 1),
        "conv_w": dn_conv_w[0], "dn_norm_w": row(dn_norm_w[0]),
        "rw": (row(rw_mu[0]), row(rw_w0[0]), row(rw_a0[0]), row(rw_k_k[0]), row(rw_k_a[0]), row(rw_r_k[0]),
               row(rw_gn_w[0]), row(rw_gn_b[0]),
               lora_pad(rw_w2[0], 0), lora_pad(rw_a2[0], 64), lora_pad(rw_g2[0], 128)),
        "w_out": w_out[0].astype(BF16), "g_ffn": row(g_ffn_norm[0]),
        "w_ff1": w_ff1[0].astype(BF16), "w_ff2": w_ff2[0].astype(BF16), "g_final": row(g_final),
    }

    nbm = SUBLANES
    xm = jnp.broadcast_to(meta_tokens.astype(F32)[None], (nbm, N_META, D_MODEL))
    _, conv_m, dn_m, shift_m, rw_m = _layer(
        xm, jnp.zeros((nbm, 3, DN_QKV), F32), jnp.zeros((nbm, DN_HEADS, DN_D, DN_D), F32),
        jnp.zeros((nbm, RW_SHIFT_W), F32), jnp.zeros((nbm, RW_HEADS, RW_HD, RW_HD), F32), wts,
        sb=nbm, tc=N_META, nc=1, tm=nbm * N_META, tm_mlp=nbm * N_META)

    y_p, conv_p, dn_p, shift_p, rw_p = _layer(
        x_prompt, conv_m[:1], dn_m[:1], shift_m[:1], rw_m[:1], wts, sb=1, tc=64, nc=4, tm=512, tm_mlp=1024)

    dec_len = x_sample.shape[1]
    y_s, conv_s, dn_s, shift_s, rw_s = _layer(
        x_sample, state_dn_conv[0], state_dn[0], state_rw_shift[0], state_rw[0], wts,
        sb=SUBLANES, tc=dec_len, nc=1, tm=512, tm_mlp=1024)

    return (y_p, y_s, conv_p[None], dn_p[None], shift_p[None], rw_p[None],
            conv_s[None], dn_s[None], shift_s[None], rw_s[None])
```

```python
import functools
import math

import jax
import jax.numpy as jnp
from jax import lax
from jax.experimental import pallas as pl
from jax.experimental.pallas import tpu as pltpu

F32 = jnp.float32
BF16 = jnp.bfloat16

D_MODEL = 1024
N_META = 16
DN_HEADS = 8
DN_D = 128
DN_QKV = 3 * DN_HEADS * DN_D
RW_HEADS = 16
RW_HD = 64
RW_PAIRS = RW_HEADS // 2
RW_W = RW_HEADS * RW_HD
RW_LORA = 256
RW_SHIFT_W = 3 * RW_W + RW_LORA
D_FF = 4 * D_MODEL
NORM_EPS = 1e-6
RW_GN_EPS = 64e-5
LANES = 128
SUBLANES = 8
INV_BLOCK = 16

COL_QKV = 0
COL_Z = DN_QKV
COL_GATE = COL_Z + D_MODEL
COL_RW = COL_GATE + 2 * D_MODEL
P_W = COL_RW + RW_SHIFT_W
P_HALF = P_W // 2

VMEM_LIMIT = 56 * 1024 * 1024


def _sigmoid(x):
    return 1.0 / (1.0 + jnp.exp(-x))


def _silu(x):
    return x * _sigmoid(x)


def _softplus(x):
    return jnp.maximum(x, 0.0) + jnp.log(1.0 + jnp.exp(-jnp.abs(x)))


def _dot(a, b):
    return jnp.dot(a.astype(BF16), b.astype(BF16), preferred_element_type=F32)


def _dot_nt(a, b):
    return lax.dot_general(a.astype(BF16), b.astype(BF16), (((1,), (1,)), ((), ())),
                           preferred_element_type=F32)


def _dot_tn(a, b):
    return lax.dot_general(a.astype(BF16), b.astype(BF16), (((0,), (0,)), ((), ())),
                           preferred_element_type=F32)


def _rows(xs):
    return xs[0] if len(xs) == 1 else jnp.concatenate(xs, axis=0)


def _dot_rows(xs, b, dot=_dot):
    out = dot(_rows(xs), b)
    off, parts = 0, []
    for x in xs:
        parts.append(out[off:off + x.shape[0]])
        off += x.shape[0]
    return parts


def _dot_mask(mask_bf16, x):
    h1 = x.astype(BF16)
    r1 = x - h1.astype(F32)
    h2 = r1.astype(BF16)
    h3 = (r1 - h2.astype(F32)).astype(BF16)
    return (jnp.dot(mask_bf16, h1, preferred_element_type=F32)
            + (jnp.dot(mask_bf16, h2, preferred_element_type=F32)
               + jnp.dot(mask_bf16, h3, preferred_element_type=F32)))


def _iota2(n, m):
    return (lax.broadcasted_iota(jnp.int32, (n, m), 0),
            lax.broadcasted_iota(jnp.int32, (n, m), 1))


def _group(idx, size):
    return lax.shift_right_logical(idx, int(math.log2(size)))


def _lockstep(chains):
    chains = list(chains)
    results = [None] * len(chains)
    active = list(range(len(chains)))
    while active:
        for i in list(active):
            try:
                next(chains[i])
            except StopIteration as done:
                results[i] = done.value
                active.remove(i)
        if active:
            yield
    return results


def _run(chain):
    for _ in chain:
        pass


def _neumann_inverse(a, eye, nil):
    inv = eye - a
    if nil <= 2:
        return inv
    power = _dot(a, a)
    yield
    k = 4
    while k < nil:
        step, power = _dot_rows([inv, power], power)
        yield
        inv = inv + step
        k *= 2
    step = _dot(inv, power)
    yield
    return inv + step


def _unit_lower_inverse(a, ri, ci, tc):
    eye = jnp.where(ri == ci, 1.0, 0.0).astype(F32)
    if tc <= INV_BLOCK:
        return (yield from _neumann_inverse(a, eye, tc))
    assert tc // INV_BLOCK <= 4
    diag = _group(ri, INV_BLOCK) == _group(ci, INV_BLOCK)
    d = jnp.where(diag, a, 0.0)
    low = a - d
    dinv = yield from _neumann_inverse(d, eye, INV_BLOCK)
    b = _dot(low, dinv)
    yield
    db, bb = _dot_rows([dinv, b], b)
    yield
    p1 = dinv - db
    tail = _dot(p1, bb)
    yield
    return p1 + tail


def _proj_kernel(x_ref, g_ref, wt_ref, wab_ref, alog_ref, dtb_ref, cw_ref, mu_ref, cb_ref, prev_ref,
                 p_ref, gb_ref, cout_ref, shift_ref, hist, *, tm, rc, sbp, ttp, ntp):
    j = pl.program_id(0)
    i = pl.program_id(1)
    tpos = lax.rem(i, ntp)
    x = x_ref[...]
    h = x * lax.rsqrt(jnp.mean(x * x, axis=-1, keepdims=True) + NORM_EPS) * g_ref[...]
    hb = h.astype(BF16)
    nt_dims = (((1,), (1,)), ((), ()))

    def mm(c0, cw):
        return lax.dot_general(hb, wt_ref[c0:c0 + cw, :], nt_dims, preferred_element_type=F32)

    row = lax.broadcasted_iota(jnp.int32, (tm, 1), 0)
    seq_start = (row & (ttp - 1)) == 0

    def shift_rows(a, first):
        rolled = pltpu.roll(a, 1, axis=0)
        if sbp == 1:
            head = jnp.where(seq_start[0:SUBLANES], first[0], rolled[0:SUBLANES])
            return jnp.concatenate([head, rolled[SUBLANES:]], axis=0)
        fill = jnp.broadcast_to(first, (sbp, ttp, a.shape[1])).reshape(tm, a.shape[1])
        return jnp.where(seq_start, fill, rolled)

    def load_history(state_ref, width):
        @pl.when(tpos == 0)
        def _():
            hist[:, 0:SUBLANES, 0:width] = state_ref[...]

        @pl.when(tpos > 0)
        def _():
            hist[:, 0:SUBLANES, 0:width] = hist[:, ttp:ttp + SUBLANES, 0:width]

    @pl.when(j == 0)
    def _():
        load_history(cb_ref, DN_QKV)
        hw = DN_HEADS * DN_D
        for c0 in range(0, DN_QKV, 512):
            cols = slice(c0, c0 + 512)
            cur = mm(c0, 512)
            hist[:, SUBLANES:, cols] = cur.reshape(sbp, ttp, 512)
            old = hist[:, 0:SUBLANES, cols]
            w = [cw_ref[j:j + 1, cols] for j in range(4)]
            x1, x2, x3 = old[:, 7:8, :], old[:, 6:7, :], old[:, 5:6, :]
            acc = cur * w[0]
            acc = cur * w[1] + shift_rows(acc, x1 * w[0])
            acc = cur * w[2] + shift_rows(acc, x1 * w[1] + x2 * w[0])
            acc = cur * w[3] + shift_rows(acc, x1 * w[2] + x2 * w[1] + x3 * w[0])
            y = _silu(acc)
            if c0 >= 2 * hw:
                p_ref[:, cols] = y
            else:
                scale = DN_D ** -0.5 if c0 < hw else 1.0
                for c1 in range(0, 512, DN_D):
                    yh = y[:, c1:c1 + DN_D]
                    yh = yh * (lax.rsqrt(jnp.sum(yh * yh, axis=-1, keepdims=True) + 1e-6) * scale)
                    p_ref[:, c0 + c1:c0 + c1 + DN_D] = yh
        for c0 in range(DN_QKV, P_HALF, 512):
            cw = min(512, P_HALF - c0)
            p_ref[:, c0:c0 + cw] = mm(c0, cw)
        for c in range(tm // rc):
            ab = lax.dot_general(wab_ref[...], hb[c * rc:(c + 1) * rc], nt_dims,
                                 preferred_element_type=F32)
            g = -jnp.exp(alog_ref[...]) * _softplus(ab[0:DN_HEADS] + dtb_ref[...])
            gb_ref[c, 0:DN_HEADS, :] = g
            gb_ref[c, DN_HEADS:2 * DN_HEADS, :] = _sigmoid(ab[DN_HEADS:2 * DN_HEADS])

        @pl.when(tpos == ntp - 1)
        def _():
            cout_ref[...] = hist[:, ttp + SUBLANES - 3:ttp + SUBLANES, 0:DN_QKV]

    @pl.when(j == 1)
    def _():
        load_history(prev_ref, RW_SHIFT_W)
        rw0 = COL_RW - P_HALF
        bounds = list(range(0, rw0, 512)) + [rw0]
        for c0, c1 in zip(bounds[:-1], bounds[1:]):
            p_ref[:, c0:c1] = mm(c0, c1 - c0)
        for c0 in range(rw0, P_HALF, 512):
            cw = min(512, P_HALF - c0)
            cols = slice(c0 - rw0, c0 - rw0 + cw)
            cur = mm(c0, cw)
            hist[:, SUBLANES:, cols] = cur.reshape(sbp, ttp, cw)
            prev = shift_rows(cur, hist[:, SUBLANES - 1:SUBLANES, cols])
            p_ref[:, c0:c0 + cw] = cur + (prev - cur) * mu_ref[:, cols]

        @pl.when(tpos == ntp - 1)
        def _():
            shift_ref[...] = hist[:, ttp + SUBLANES - 1:ttp + SUBLANES, :]


def _proj(x2d, g_norm, w_main_t, w_ab_t, a_log, dt_bias, conv_w, mu, conv_buf8, prev8, *, nb, seq, tm, rc):
    n = nb * seq
    ttp = min(seq, tm)
    sbp = tm // ttp
    ntp = seq // ttp
    assert n % tm == 0 and tm % rc == 0 and tm % ttp == 0 and seq % ttp == 0
    bcast = conv_buf8.shape[0] == 1
    assert bcast == (prev8.shape[0] == 1) and (not bcast or sbp == 1)
    last = (n // tm - 1) // ntp

    def st_idx(i):
        return 0 if bcast else i // ntp

    return pl.pallas_call(
        functools.partial(_proj_kernel, tm=tm, rc=rc, sbp=sbp, ttp=ttp, ntp=ntp),
        out_shape=(jax.ShapeDtypeStruct((n, P_W), F32),
                   jax.ShapeDtypeStruct((n // rc, 2 * DN_HEADS, rc), F32),
                   jax.ShapeDtypeStruct((nb, 3, DN_QKV), F32),
                   jax.ShapeDtypeStruct((nb, 1, RW_SHIFT_W), F32)),
        grid=(2, n // tm),
        in_specs=[
            pl.BlockSpec((tm, D_MODEL), lambda j, i: (i, 0)),
            pl.BlockSpec((1, D_MODEL), lambda j, i: (0, 0)),
            pl.BlockSpec((P_HALF, D_MODEL), lambda j, i: (j, 0)),
            pl.BlockSpec((2 * DN_HEADS, D_MODEL), lambda j, i: (0, 0)),
            pl.BlockSpec((DN_HEADS, 1), lambda j, i: (0, 0)),
            pl.BlockSpec((DN_HEADS, 1), lambda j, i: (0, 0)),
            pl.BlockSpec((4, DN_QKV), lambda j, i: (0, 0)),
            pl.BlockSpec((1, RW_SHIFT_W), lambda j, i: (0, 0)),
            pl.BlockSpec((sbp, SUBLANES, DN_QKV), lambda j, i: (st_idx(i), 0, 0)),
            pl.BlockSpec((sbp, SUBLANES, RW_SHIFT_W), lambda j, i: (st_idx(i), 0, 0)),
        ],
        out_specs=(
            pl.BlockSpec((tm, P_HALF), lambda j, i: (i, j)),
            pl.BlockSpec((tm // rc, 2 * DN_HEADS, rc),
                         lambda j, i: (i * (1 - j) + (n // tm - 1) * j, 0, 0)),
            pl.BlockSpec((sbp, 3, DN_QKV), lambda j, i: ((i // ntp) * (1 - j) + last * j, 0, 0)),
            pl.BlockSpec((sbp, 1, RW_SHIFT_W), lambda j, i: ((i // ntp) * j, 0, 0)),
        ),
        scratch_shapes=[pltpu.VMEM((sbp, ttp + SUBLANES, RW_SHIFT_W), F32)],
        compiler_params=pltpu.CompilerParams(
            dimension_semantics=("arbitrary", "arbitrary"), vmem_limit_bytes=VMEM_LIMIT),
        name="proj",
    )(x2d, g_norm, w_main_t, w_ab_t, a_log, dt_bias, conv_w, mu, conv_buf8, prev8)


def _dn_kernel(qkv_ref, z_ref, gb_ref, nw_ref, s0_ref, o_ref, sout_ref, s_scr, *, sb, tc, nc, nt):
    t = pl.program_id(1)
    ru = sb * tc
    hw = DN_HEADS * DN_D

    @pl.when(t == 0)
    def _():
        s_scr[...] = s0_ref[...]

    ri, ci = _iota2(ru, ru)
    same = _group(ri, tc) == _group(ci, tc)
    incl = same & (ri >= ci)
    strict = same & (ri > ci)
    eye = ri == ci
    seqs = [slice(s * tc, (s + 1) * tc) for s in range(sb)]
    units = [slice(c * ru, (c + 1) * ru) for c in range(nc)]

    def unit_chain(h, rows, q, k, v):
        g_row = gb_ref[0, h:h + 1, :][:, rows]
        b_row = gb_ref[0, DN_HEADS + h:DN_HEADS + h + 1, :][:, rows]
        g_col = jnp.sum(jnp.where(incl, g_row, 0.0), axis=1, keepdims=True)
        g_cum_row = jnp.sum(jnp.where(eye, g_col, 0.0), axis=0, keepdims=True)
        g_tot = jnp.sum(jnp.where(same, g_row, 0.0), axis=1, keepdims=True)
        b_col = jnp.sum(jnp.where(eye, b_row, 0.0), axis=1, keepdims=True)
        decay = jnp.where(incl, jnp.exp(jnp.minimum(g_col - g_cum_row, 0.0)), 0.0)
        e_g = jnp.exp(g_col)
        e_rest = jnp.exp(g_tot - g_col)
        kb = k * b_col
        a_raw, qk_raw = _dot_rows([kb, q], k, dot=_dot_nt)
        yield
        a = jnp.where(strict, a_raw * decay, 0.0)
        aqk = qk_raw * decay
        t_inv = yield from _unit_lower_inverse(a, ri, ci, tc)
        rhs = jnp.concatenate([v * b_col, kb * e_g], axis=1)
        sol = _dot(t_inv, rhs)
        yield
        return dict(u_all=sol[:, :DN_D], w_all=sol[:, DN_D:], aqk=aqk, qg=q * e_g, kd=k * e_rest,
                    g_tot=g_tot)

    def head_chain(h):
        cols = slice(h * DN_D, (h + 1) * DN_D)
        q = qkv_ref[:, cols]
        k = qkv_ref[:, hw + h * DN_D:hw + (h + 1) * DN_D]
        v = qkv_ref[:, 2 * hw + h * DN_D:2 * hw + (h + 1) * DN_D]
        wy = yield from _lockstep(unit_chain(h, rows, q[rows], k[rows], v[rows]) for rows in units)
        for rows, w in zip(units, wy):
            ws = [_dot(jnp.concatenate([w["w_all"][sq], w["qg"][sq]], axis=0), s_scr[s, h])
                  for s, sq in enumerate(seqs)]
            yield
            u = _rows([w["u_all"][sq] - x[:tc] for sq, x in zip(seqs, ws)])
            q_s = _rows([x[tc:] for x in ws])
            intra = _dot(w["aqk"], u)
            upd = [_dot_tn(w["kd"][sq], u[sq]) for sq in seqs]
            yield
            for s in range(sb):
                gl = jnp.exp(w["g_tot"][s * tc:s * tc + 1, :])
                s_scr[s, h] = s_scr[s, h] * gl + upd[s]
            o = q_s + intra
            o = o * lax.rsqrt(jnp.mean(o * o, axis=-1, keepdims=True) + NORM_EPS) * nw_ref[...]
            o_ref[rows, cols] = o * _silu(z_ref[rows, cols])

    _run(_lockstep(head_chain(h) for h in range(DN_HEADS)))

    @pl.when(t == nt - 1)
    def _():
        sout_ref[...] = s_scr[...]


def _dn(p, gb, norm_w, s0, *, nb, seq, sb, tc, nc):
    nt = seq // (nc * tc)
    r = sb * nc * tc
    assert seq % (nc * tc) == 0 and nb % sb == 0 and (nt == 1 or sb == 1) and (nc == 1 or sb == 1)
    bcast = s0.shape[0] == 1
    assert not bcast or sb == 1
    hw = DN_HEADS * DN_D

    def st_idx(i):
        return 0 if bcast else i

    return pl.pallas_call(
        functools.partial(_dn_kernel, sb=sb, tc=tc, nc=nc, nt=nt),
        out_shape=(jax.ShapeDtypeStruct((nb * seq, hw), F32),
                   jax.ShapeDtypeStruct((nb, DN_HEADS, DN_D, DN_D), F32)),
        grid=(nb // sb, nt),
        in_specs=[
            pl.BlockSpec((r, DN_QKV), lambda i, t: (i * nt + t, COL_QKV // DN_QKV)),
            pl.BlockSpec((r, hw), lambda i, t: (i * nt + t, COL_Z // hw)),
            pl.BlockSpec((1, 2 * DN_HEADS, r), lambda i, t: (i * nt + t, 0, 0)),
            pl.BlockSpec((1, DN_D), lambda i, t: (0, 0)),
            pl.BlockSpec((sb, DN_HEADS, DN_D, DN_D), lambda i, t: (st_idx(i), 0, 0, 0)),
        ],
        out_specs=(
            pl.BlockSpec((r, hw), lambda i, t: (i * nt + t, 0)),
            pl.BlockSpec((sb, DN_HEADS, DN_D, DN_D), lambda i, t: (i, 0, 0, 0)),
        ),
        scratch_shapes=[pltpu.VMEM((sb, DN_HEADS, DN_D, DN_D), F32)],
        compiler_params=pltpu.CompilerParams(
            dimension_semantics=("arbitrary", "arbitrary"), vmem_limit_bytes=VMEM_LIMIT),
        name="dn",
    )(p, p, gb, norm_w, s0)


def _rw_kernel(r_ref, k_ref, v_ref, l_ref, w0_ref, a0_ref, kkw_ref, kaw_ref, rkw_ref,
               gnw_ref, gnb_ref, w2_ref, a2_ref, g2_ref, s0_ref, odn_ref, ga_ref, gb_ref,
               o_ref, sout_ref, s_scr, *, sb, tc, nc, nt):
    t = pl.program_id(1)
    ru = sb * tc
    half = lax.broadcasted_iota(jnp.int32, (1, LANES), 1) < RW_HD
    ri128, ci128 = _iota2(LANES, LANES)
    bd_mask = _group(ri128, RW_HD) == _group(ci128, RW_HD)

    def seg_sum(x):
        first = jnp.sum(jnp.where(half, x, 0.0), axis=-1, keepdims=True)
        second = jnp.sum(jnp.where(half, 0.0, x), axis=-1, keepdims=True)
        return jnp.where(half, first, second)

    @pl.when(t == 0)
    def _():
        zero = jnp.zeros((RW_HD, RW_HD), F32)
        for s in range(sb):
            for p in range(RW_PAIRS):
                top = jnp.concatenate([s0_ref[s, 2 * p], zero], axis=1)
                bot = jnp.concatenate([zero, s0_ref[s, 2 * p + 1]], axis=1)
                s_scr[s, p] = jnp.concatenate([top, bot], axis=0)

    xm_l = l_ref[...]
    lora_w = jnp.tanh(xm_l).astype(BF16)
    lora_a = xm_l.astype(BF16)
    lora_g = _sigmoid(xm_l).astype(BF16)

    ri, ci = _iota2(ru, ru)
    cum_mask = jnp.where((_group(ri, tc) == _group(ci, tc)) & (ri >= ci), 1.0, 0.0).astype(BF16)
    r2 = 2 * ru
    ri2, ci2 = _iota2(r2, r2)
    same2 = _group(ri2, tc) == _group(ci2, tc)
    incl2 = same2 & (ri2 >= ci2)
    strict2 = same2 & (ri2 > ci2)
    seqs = [slice(s * tc, (s + 1) * tc) for s in range(sb)]
    units = [slice(c * ru, (c + 1) * ru) for c in range(nc)]

    def stack(x):
        return jnp.concatenate([jnp.where(half, x, 0.0), jnp.where(half, 0.0, x)], axis=0)

    def unstack(x2):
        return x2[:ru] + x2[ru:]

    def unit_chain(rr, k2, vv, kk, bb, logw):
        lcum = _dot_mask(cum_mask, logw)
        yield
        ltot = jnp.broadcast_to(lcum.reshape(sb, tc, LANES)[:, tc - 1:tc, :],
                                (sb, tc, LANES)).reshape(ru, LANES)
        p_in = jnp.exp(lcum)
        p_inv = jnp.exp(-lcum)
        alpha = kk * jnp.exp(lcum - logw)
        beta = bb * p_inv
        kt = k2 * p_inv
        rt = rr * p_in
        e_rest = jnp.exp(ltot - lcum)
        alpha2, rt2, beta2, kt2, v2 = stack(alpha), stack(rt), stack(beta), stack(kt), stack(vv)
        l_raw, rb_raw = _dot_rows([alpha2, rt2], beta2, dot=_dot_nt)
        lk_raw, rk_raw = _dot_rows([alpha2, rt2], kt2, dot=_dot_nt)
        yield
        lmat = jnp.where(strict2, l_raw, 0.0)
        rb = jnp.where(incl2, rb_raw, 0.0)
        lk = jnp.where(strict2, lk_raw, 0.0)
        rk = jnp.where(incl2, rk_raw, 0.0)
        lkv, rkv = _dot_rows([lk, rk], v2)
        t_inv = yield from _unit_lower_inverse(lmat, ri2, ci2, tc)
        sol = _dot(t_inv, jnp.concatenate([alpha2, lkv], axis=1))
        yield
        return dict(wa2=sol[:, :LANES], uv2=sol[:, LANES:], rt=rt, rb=rb, rkv=rkv, p_in=p_in,
                    k_end=k2 * e_rest, b_end=bb * e_rest)

    def pair_chain(p):
        c = slice(p * LANES, (p + 1) * LANES)
        rr = r_ref[:, c]
        xk = k_ref[:, c]
        vv = v_ref[:, c]
        lw = jnp.dot(lora_w, w2_ref[:, c], preferred_element_type=F32)
        la = jnp.dot(lora_a, a2_ref[:, c], preferred_element_type=F32)
        gate = jnp.dot(lora_g, g2_ref[:, c], preferred_element_type=F32)
        yield
        w_log = -_softplus(-(w0_ref[:, c] + lw)) - 0.5
        logw = -jnp.exp(w_log)
        aa = _sigmoid(a0_ref[:, c] + la)
        kk = xk * kkw_ref[:, c]
        k2 = xk * (1.0 + (aa - 1.0) * kaw_ref[:, c])
        kk = kk * lax.rsqrt(seg_sum(kk * kk) + 1e-6)
        bb = kk * aa
        wy = yield from _lockstep(
            unit_chain(rr[rows], k2[rows], vv[rows], kk[rows], bb[rows], logw[rows]) for rows in units)
        for rows, w in zip(units, wy):
            st = [_dot_nt(jnp.concatenate([w["wa2"][sq], w["wa2"][ru + s * tc:ru + (s + 1) * tc], w["rt"][sq]],
                                          axis=0), s_scr[s, p]) for s, sq in enumerate(seqs)]
            yield
            u2 = w["uv2"] + _rows([x[:tc] for x in st] + [x[tc:2 * tc] for x in st])
            r_state = _rows([x[2 * tc:] for x in st])
            rbu = _dot(w["rb"], u2)
            u = unstack(u2)
            v_u = vv[rows]
            upd = [_dot_tn(jnp.concatenate([v_u[sq], -u[sq]], axis=0),
                           jnp.concatenate([w["k_end"][sq], w["b_end"][sq]], axis=0)) for sq in seqs]
            yield
            for s in range(sb):
                p_end = w["p_in"][(s + 1) * tc - 1:(s + 1) * tc, :]
                s_scr[s, p] = s_scr[s, p] * p_end + jnp.where(bd_mask, upd[s], 0.0)
            o = r_state + unstack(w["rkv"] - rbu)
            mean = seg_sum(o) * (1.0 / RW_HD)
            d = o - mean
            var = seg_sum(d * d) * (1.0 / RW_HD)
            bonus = seg_sum(rr[rows] * k2[rows] * rkw_ref[:, c])
            on = d * lax.rsqrt(var + RW_GN_EPS) * gnw_ref[:, c] + gnb_ref[:, c]
            o_rw = (on + bonus * v_u) * gate[rows]
            mix = _sigmoid(ga_ref[rows, c]) * odn_ref[rows, c] + _sigmoid(gb_ref[rows, c]) * o_rw
            o_ref[rows, c] = mix.astype(BF16)

    _run(_lockstep(pair_chain(p) for p in range(RW_PAIRS)))

    @pl.when(t == nt - 1)
    def _():
        for s in range(sb):
            for p in range(RW_PAIRS):
                sp = s_scr[s, p]
                sout_ref[s, 2 * p] = sp[0:RW_HD, 0:RW_HD]
                sout_ref[s, 2 * p + 1] = sp[RW_HD:, RW_HD:]


def _rw(p, o_dn, weights, s0, *, nb, seq, sb, tc, nc):
    nt = seq // (nc * tc)
    r = sb * nc * tc
    assert seq % (nc * tc) == 0 and nb % sb == 0 and (nt == 1 or sb == 1) and (nc == 1 or sb == 1)
    bcast = s0.shape[0] == 1
    assert not bcast or sb == 1
    (w0, a0, kkw, kaw, rkw, gnw, gnb, w2p, a2p, g2p) = weights

    def st_idx(i):
        return 0 if bcast else i

    def vec(width):
        return pl.BlockSpec((1, width), lambda i, t: (0, 0))

    def lora_w():
        return pl.BlockSpec((RW_LORA, RW_W), lambda i, t: (0, 0))

    def rows(col):
        return pl.BlockSpec((r, RW_W), lambda i, t: (i * nt + t, col))

    rcol = COL_RW // RW_W
    gcol = COL_GATE // D_MODEL
    return pl.pallas_call(
        functools.partial(_rw_kernel, sb=sb, tc=tc, nc=nc, nt=nt),
        out_shape=(jax.ShapeDtypeStruct((nb * seq, RW_W), BF16),
                   jax.ShapeDtypeStruct((nb, RW_HEADS, RW_HD, RW_HD), F32)),
        grid=(nb // sb, nt),
        in_specs=[
            rows(rcol), rows(rcol + 1), rows(rcol + 2),
            pl.BlockSpec((r, RW_LORA), lambda i, t: (i * nt + t, (COL_RW + 3 * RW_W) // RW_LORA)),
            vec(RW_W), vec(RW_W), vec(RW_W), vec(RW_W), vec(RW_W), vec(RW_W), vec(RW_W),
            lora_w(), lora_w(), lora_w(),
            pl.BlockSpec((sb, RW_HEADS, RW_HD, RW_HD), lambda i, t: (st_idx(i), 0, 0, 0)),
            rows(0), rows(gcol), rows(gcol + 1),
        ],
        out_specs=(
            rows(0),
            pl.BlockSpec((sb, RW_HEADS, RW_HD, RW_HD), lambda i, t: (i, 0, 0, 0)),
        ),
        scratch_shapes=[pltpu.VMEM((sb, RW_PAIRS, LANES, LANES), F32)],
        compiler_params=pltpu.CompilerParams(
            dimension_semantics=("arbitrary", "arbitrary"), vmem_limit_bytes=VMEM_LIMIT),
        name="rw",
    )(p, p, p, p, w0, a0, kkw, kaw, rkw, gnw, gnb, w2p, a2p, g2p, s0, o_dn, p, p)


def _merge_kernel(x_ref, mix_ref, wout_ref, gffn_ref, w1_ref, w2_ref, gfin_ref,
                  y_ref, x1_scr, h2_scr, acc_scr, *, nf):
    f = pl.program_id(1)

    @pl.when(f == 0)
    def _():
        x1 = x_ref[...] + jnp.dot(mix_ref[...], wout_ref[...], preferred_element_type=F32)
        x1_scr[...] = x1
        h2 = x1 * lax.rsqrt(jnp.mean(x1 * x1, axis=-1, keepdims=True) + NORM_EPS) * gffn_ref[...]
        h2_scr[...] = h2.astype(BF16)
        acc_scr[...] = jnp.zeros_like(acc_scr)

    hid = jnp.dot(h2_scr[...], w1_ref[...], preferred_element_type=F32)
    hid = jnp.square(jnp.maximum(hid, 0.0))
    acc_scr[...] += jnp.dot(hid.astype(BF16), w2_ref[...], preferred_element_type=F32)

    @pl.when(f == nf - 1)
    def _():
        y = x1_scr[...] + acc_scr[...]
        y_ref[...] = y * lax.rsqrt(jnp.mean(y * y, axis=-1, keepdims=True) + NORM_EPS) * gfin_ref[...]


def _merge(x2d, mix, w_out, g_ffn, w1, w2, g_fin, *, tm, tf):
    n = x2d.shape[0]
    nf = D_FF // tf
    assert n % tm == 0 and D_FF % tf == 0
    row = lambda i, f: (i, 0)
    return pl.pallas_call(
        functools.partial(_merge_kernel, nf=nf),
        out_shape=jax.ShapeDtypeStruct((n, D_MODEL), F32),
        grid=(n // tm, nf),
        in_specs=[
            pl.BlockSpec((tm, D_MODEL), row),
            pl.BlockSpec((tm, D_MODEL), row),
            pl.BlockSpec((D_MODEL, D_MODEL), lambda i, f: (0, 0)),
            pl.BlockSpec((1, D_MODEL), lambda i, f: (0, 0)),
            pl.BlockSpec((D_MODEL, tf), lambda i, f: (0, f)),
            pl.BlockSpec((tf, D_MODEL), lambda i, f: (f, 0)),
            pl.BlockSpec((1, D_MODEL), lambda i, f: (0, 0)),
        ],
        out_specs=pl.BlockSpec((tm, D_MODEL), row),
        scratch_shapes=[
            pltpu.VMEM((tm, D_MODEL), F32),
            pltpu.VMEM((tm, D_MODEL), BF16),
            pltpu.VMEM((tm, D_MODEL), F32),
        ],
        compiler_params=pltpu.CompilerParams(
            dimension_semantics=("arbitrary", "arbitrary"), vmem_limit_bytes=VMEM_LIMIT),
        name="merge",
    )(x2d, mix, w_out, g_ffn, w1, w2, g_fin)


def _pad_rows_front(x, rows):
    b, n, w = x.shape
    return jnp.concatenate([jnp.zeros((b, rows - n, w), x.dtype), x], axis=1)


def _layer(x, conv_buf, dn_s, rw_prev, rw_s, wts, *, sb, tc, nc, tm, tm_mlp):
    nb, seq, _ = x.shape
    n = nb * seq
    x2d = x.reshape(n, D_MODEL)
    p, gb, conv_new, shift_new = _proj(
        x2d, wts["g_mix"], wts["w_main_t"], wts["w_ab_t"], wts["a_log"], wts["dt_bias"], wts["conv_w"],
        wts["rw_mu"], _pad_rows_front(conv_buf, SUBLANES), _pad_rows_front(rw_prev[:, None, :], SUBLANES),
        nb=nb, seq=seq, tm=tm, rc=sb * nc * tc)
    o_dn, dn_new = _dn(p, gb, wts["dn_norm_w"], dn_s, nb=nb, seq=seq, sb=sb, tc=tc, nc=nc)
    mix, rw_new = _rw(p, o_dn, wts["rw"], rw_s, nb=nb, seq=seq, sb=sb, tc=tc, nc=nc)
    y = _merge(x2d, mix, wts["w_out"], wts["g_ffn"], wts["w_ff1"], wts["w_ff2"], wts["g_final"],
               tm=tm_mlp, tf=1024)
    return y.reshape(nb, seq, D_MODEL), conv_new, dn_new, shift_new[:, 0], rw_new


def kernel(x_prompt, x_sample, state_dn_conv, state_dn, state_rw_shift, state_rw, meta_tokens, g_mix_norm, w_in, dn_conv_w, dn_a_log, dn_dt_bias, dn_norm_w, rw_mu, rw_w0, rw_w2, rw_a0, rw_a2, rw_g2, rw_k_k, rw_k_a, rw_r_k, rw_gn_w, rw_gn_b, w_out, g_ffn_norm, w_ff1, w_ff2, g_final):
    assert g_mix_norm.shape[0] == 1, "single layer"
    w_t = w_in[0].T
    o_a = DN_QKV + D_MODEL
    o_rwp = o_a + 2 * DN_HEADS
    o_gate = o_rwp + RW_SHIFT_W
    w_main_t = jnp.concatenate([w_t[:o_a], w_t[o_gate:], w_t[o_rwp:o_gate]], axis=0).astype(BF16)
    w_ab_t = w_t[o_a:o_rwp].astype(BF16)

    def lora_pad(wl, lo):
        return jnp.zeros((RW_LORA, RW_W), F32).at[lo:lo + wl.shape[0]].set(wl).astype(BF16)

    row = lambda v: v.reshape(1, -1).astype(F32)
    wts = {
        "g_mix": row(g_mix_norm[0]), "w_main_t": w_main_t, "w_ab_t": w_ab_t,
        "a_log": dn_a_log[0].reshape(DN_HEADS, 1), "dt_bias": dn_dt_bias[0].reshape(DN_HEADS, 1),
        "conv_w": dn_conv_w[0], "dn_norm_w": row(dn_norm_w[0]), "rw_mu": row(rw_mu[0]),
        "rw": (row(rw_w0[0]), row(rw_a0[0]), row(rw_k_k[0]), row(rw_k_a[0]), row(rw_r_k[0]),
               row(rw_gn_w[0]), row(rw_gn_b[0]),
               lora_pad(rw_w2[0], 0), lora_pad(rw_a2[0], 64), lora_pad(rw_g2[0], 128)),
        "w_out": w_out[0].astype(BF16), "g_ffn": row(g_ffn_norm[0]),
        "w_ff1": w_ff1[0].astype(BF16), "w_ff2": w_ff2[0].astype(BF16), "g_final": row(g_final),
    }

    nbm = SUBLANES
    xm = jnp.broadcast_to(meta_tokens.astype(F32)[None], (nbm, N_META, D_MODEL))
    _, conv_m, dn_m, shift_m, rw_m = _layer(
        xm, jnp.zeros((nbm, 3, DN_QKV), F32), jnp.zeros((nbm, DN_HEADS, DN_D, DN_D), F32),
        jnp.zeros((nbm, RW_SHIFT_W), F32), jnp.zeros((nbm, RW_HEADS, RW_HD, RW_HD), F32), wts,
        sb=nbm, tc=N_META, nc=1, tm=nbm * N_META, tm_mlp=nbm * N_META)

    y_p, conv_p, dn_p, shift_p, rw_p = _layer(
        x_prompt, conv_m[:1], dn_m[:1], shift_m[:1], rw_m[:1], wts, sb=1, tc=64, nc=4, tm=512, tm_mlp=1024)

    dec_len = x_sample.shape[1]
    y_s, conv_s, dn_s, shift_s, rw_s = _layer(
        x_sample, state_dn_conv[0], state_dn[0], state_rw_shift[0], state_rw[0], wts,
        sb=SUBLANES, tc=dec_len, nc=1, tm=128, tm_mlp=1024)

    return (y_p, y_s, conv_p[None], dn_p[None], shift_p[None], rw_p[None],
            conv_s[None], dn_s[None], shift_s[None], rw_s[None])
```

```python
import functools
import math

import jax
import jax.numpy as jnp
from jax import lax
from jax.experimental import pallas as pl
from jax.experimental.pallas import tpu as pltpu

F32 = jnp.float32
BF16 = jnp.bfloat16

D_MODEL = 1024
N_META = 16
DN_HEADS = 8
DN_D = 128
DN_QKV = 3 * DN_HEADS * DN_D
RW_HEADS = 16
RW_HD = 64
RW_PAIRS = RW_HEADS // 2
RW_W = RW_HEADS * RW_HD
RW_LORA = 256
RW_SHIFT_W = 3 * RW_W + RW_LORA
D_FF = 4 * D_MODEL
NORM_EPS = 1e-6
RW_GN_EPS = 64e-5
LANES = 128
SUBLANES = 8
INV_BLOCK = 16

SEGMENTS = (
    ("q", D_MODEL, "conv", 0), ("k", D_MODEL, "conv", D_MODEL),
    ("r", RW_W, "shift", 0), ("kx", RW_W, "shift", RW_W),
    ("ga", D_MODEL, "raw", 0), ("gb", D_MODEL, "raw", 0),
    ("v", D_MODEL, "conv", 2 * D_MODEL), ("z", D_MODEL, "raw", 0),
    ("vx", RW_W, "shift", 2 * RW_W), ("lora", RW_LORA, "shift", 3 * RW_W),
)
COL = {}
P_W = 0
for _name, _width, _, _ in SEGMENTS:
    COL[_name] = P_W
    P_W += _width
P_HALF = P_W // 2
PASS0_CONV = 2 * D_MODEL
PASS0_SHIFT = 2 * RW_W


def _chunk_plan(half):
    lo, hi = half * P_HALF, (half + 1) * P_HALF
    plan = []
    for name, width, kind, off in SEGMENTS:
        start = COL[name]
        c = max(start, lo)
        while c < min(start + width, hi):
            w = min(512, min(start + width, hi) - c)
            plan.append((c - lo, w, kind, off + (c - start), name))
            c += w
    return plan

VMEM_LIMIT = 56 * 1024 * 1024


def _sigmoid(x):
    return 1.0 / (1.0 + jnp.exp(-x))


def _silu(x):
    return x * _sigmoid(x)


def _softplus(x):
    return jnp.maximum(x, 0.0) + jnp.log(1.0 + jnp.exp(-jnp.abs(x)))


def _dot(a, b):
    return jnp.dot(a.astype(BF16), b.astype(BF16), preferred_element_type=F32)


def _dot_nt(a, b):
    return lax.dot_general(a.astype(BF16), b.astype(BF16), (((1,), (1,)), ((), ())),
                           preferred_element_type=F32)


def _dot_tn(a, b):
    return lax.dot_general(a.astype(BF16), b.astype(BF16), (((0,), (0,)), ((), ())),
                           preferred_element_type=F32)


def _rows(xs):
    return xs[0] if len(xs) == 1 else jnp.concatenate(xs, axis=0)


def _dot_rows(xs, b, dot=_dot):
    out = dot(_rows(xs), b)
    off, parts = 0, []
    for x in xs:
        parts.append(out[off:off + x.shape[0]])
        off += x.shape[0]
    return parts


def _dot_mask(mask_bf16, x):
    h1 = x.astype(BF16)
    r1 = x - h1.astype(F32)
    h2 = r1.astype(BF16)
    h3 = (r1 - h2.astype(F32)).astype(BF16)
    return (jnp.dot(mask_bf16, h1, preferred_element_type=F32)
            + (jnp.dot(mask_bf16, h2, preferred_element_type=F32)
               + jnp.dot(mask_bf16, h3, preferred_element_type=F32)))


def _iota2(n, m):
    return (lax.broadcasted_iota(jnp.int32, (n, m), 0),
            lax.broadcasted_iota(jnp.int32, (n, m), 1))


def _group(idx, size):
    return lax.shift_right_logical(idx, int(math.log2(size)))


def _lockstep(chains):
    chains = list(chains)
    results = [None] * len(chains)
    active = list(range(len(chains)))
    while active:
        for i in list(active):
            try:
                next(chains[i])
            except StopIteration as done:
                results[i] = done.value
                active.remove(i)
        if active:
            yield
    return results


def _run(chain):
    for _ in chain:
        pass


def _neumann_inverse(a, eye, nil, dot):
    inv = eye - a
    if nil <= 2:
        return inv
    power = dot(a, a)
    yield
    k = 4
    while k < nil:
        step, power = _dot_rows([inv, power], power, dot=dot)
        yield
        inv = inv + step
        k *= 2
    step = dot(inv, power)
    yield
    return inv + step


def _unit_lower_inverse(a, ri, ci, tc, dot=_dot):
    eye = jnp.where(ri == ci, 1.0, 0.0).astype(F32)
    if tc <= INV_BLOCK:
        return (yield from _neumann_inverse(a, eye, tc, dot))
    assert tc // INV_BLOCK <= 4
    diag = _group(ri, INV_BLOCK) == _group(ci, INV_BLOCK)
    d = jnp.where(diag, a, 0.0)
    low = a - d
    dinv = yield from _neumann_inverse(d, eye, INV_BLOCK, dot)
    b = dot(low, dinv)
    yield
    db, bb = _dot_rows([dinv, b], b, dot=dot)
    yield
    p1 = dinv - db
    tail = dot(p1, bb)
    yield
    return p1 + tail


def _proj_kernel(x_ref, g_ref, wt_ref, wab_ref, alog_ref, dtb_ref, cw_ref, mu_ref, cb_ref, prev_ref,
                 p_ref, gb_ref, cout0_ref, shift0_ref, cout1_ref, shift1_ref, hist, *, tm, rc, sbp, ttp, ntp):
    j = pl.program_id(0)
    i = pl.program_id(1)
    tpos = lax.rem(i, ntp)
    x = x_ref[...]
    h = x * lax.rsqrt(jnp.mean(x * x, axis=-1, keepdims=True) + NORM_EPS) * g_ref[...]
    hb = h.astype(BF16)
    nt_dims = (((1,), (1,)), ((), ()))

    def mm(c0, cw):
        return lax.dot_general(hb, wt_ref[c0:c0 + cw, :], nt_dims, preferred_element_type=F32)

    row = lax.broadcasted_iota(jnp.int32, (tm, 1), 0)
    seq_start = (row & (ttp - 1)) == 0

    def shift_rows(a, first):
        rolled = pltpu.roll(a, 1, axis=0)
        if sbp == 1:
            head = jnp.where(seq_start[0:SUBLANES], first[0], rolled[0:SUBLANES])
            return jnp.concatenate([head, rolled[SUBLANES:]], axis=0)
        fill = jnp.broadcast_to(first, (sbp, ttp, a.shape[1])).reshape(tm, a.shape[1])
        return jnp.where(seq_start, fill, rolled)

    def run_pass(plan):
        @pl.when(tpos == 0)
        def _():
            hist[:, 0:SUBLANES, 0:DN_QKV] = cb_ref[...]
            hist[:, 0:SUBLANES, DN_QKV:] = prev_ref[...]

        @pl.when(tpos > 0)
        def _():
            hist[:, 0:SUBLANES, :] = hist[:, SUBLANES:, :]

        for c0, cw, kind, off, name in plan:
            cur = mm(c0, cw)
            if kind == "raw":
                p_ref[:, c0:c0 + cw] = cur
                continue
            hc = slice(off, off + cw) if kind == "conv" else slice(DN_QKV + off, DN_QKV + off + cw)
            old = hist[:, 0:SUBLANES, hc]
            hist[:, SUBLANES:, hc] = cur.reshape(sbp, ttp, cw)[:, ttp - SUBLANES:, :]
            x1 = old[:, 7:8, :]
            if kind == "shift":
                prev = shift_rows(cur, x1)
                p_ref[:, c0:c0 + cw] = cur + (prev - cur) * mu_ref[:, off:off + cw]
                continue
            w = [cw_ref[tap:tap + 1, off:off + cw] for tap in range(4)]
            x2, x3 = old[:, 6:7, :], old[:, 5:6, :]
            acc = cur * w[0]
            acc = cur * w[1] + shift_rows(acc, x1 * w[0])
            acc = cur * w[2] + shift_rows(acc, x1 * w[1] + x2 * w[0])
            acc = cur * w[3] + shift_rows(acc, x1 * w[2] + x2 * w[1] + x3 * w[0])
            y = _silu(acc)
            if name == "v":
                p_ref[:, c0:c0 + cw] = y
            else:
                scale = DN_D ** -0.5 if name == "q" else 1.0
                for c1 in range(0, cw, DN_D):
                    yh = y[:, c1:c1 + DN_D]
                    yh = yh * (lax.rsqrt(jnp.sum(yh * yh, axis=-1, keepdims=True) + 1e-6) * scale)
                    p_ref[:, c0 + c1:c0 + c1 + DN_D] = yh

    @pl.when(j == 0)
    def _():
        run_pass(_chunk_plan(0))
        for c in range(tm // rc):
            ab = lax.dot_general(wab_ref[...], hb[c * rc:(c + 1) * rc], nt_dims,
                                 preferred_element_type=F32)
            g = -jnp.exp(alog_ref[...]) * _softplus(ab[0:DN_HEADS] + dtb_ref[...])
            gb_ref[c, 0:DN_HEADS, :] = g
            gb_ref[c, DN_HEADS:2 * DN_HEADS, :] = _sigmoid(ab[DN_HEADS:2 * DN_HEADS])

        @pl.when(tpos == ntp - 1)
        def _():
            cout0_ref[...] = hist[:, 2 * SUBLANES - 3:, 0:PASS0_CONV]
            shift0_ref[...] = hist[:, 2 * SUBLANES - 1:, DN_QKV:DN_QKV + PASS0_SHIFT]

    @pl.when(j == 1)
    def _():
        run_pass(_chunk_plan(1))

        @pl.when(tpos == ntp - 1)
        def _():
            cout1_ref[...] = hist[:, 2 * SUBLANES - 3:, PASS0_CONV:DN_QKV]
            shift1_ref[...] = hist[:, 2 * SUBLANES - 1:, DN_QKV + PASS0_SHIFT:]


def _proj(x2d, g_norm, w_main_t, w_ab_t, a_log, dt_bias, conv_w, mu, conv_buf8, prev8, *, nb, seq, tm, rc):
    n = nb * seq
    ttp = min(seq, tm)
    sbp = tm // ttp
    ntp = seq // ttp
    assert n % tm == 0 and tm % rc == 0 and tm % ttp == 0 and seq % ttp == 0
    bcast = conv_buf8.shape[0] == 1
    assert bcast == (prev8.shape[0] == 1) and (not bcast or sbp == 1)
    last = (n // tm - 1) // ntp

    def st_idx(i):
        return 0 if bcast else i // ntp

    assert ttp % SUBLANES == 0 and ttp & (ttp - 1) == 0
    first_pass = lambda j, i: ((i // ntp) * (1 - j) + last * j, 0, 0)
    second_pass = lambda j, i: ((i // ntp) * j, 0, 0)
    p, gb, cout0, shift0, cout1, shift1 = pl.pallas_call(
        functools.partial(_proj_kernel, tm=tm, rc=rc, sbp=sbp, ttp=ttp, ntp=ntp),
        out_shape=(jax.ShapeDtypeStruct((n, P_W), F32),
                   jax.ShapeDtypeStruct((n // rc, 2 * DN_HEADS, rc), F32),
                   jax.ShapeDtypeStruct((nb, 3, PASS0_CONV), F32),
                   jax.ShapeDtypeStruct((nb, 1, PASS0_SHIFT), F32),
                   jax.ShapeDtypeStruct((nb, 3, DN_QKV - PASS0_CONV), F32),
                   jax.ShapeDtypeStruct((nb, 1, RW_SHIFT_W - PASS0_SHIFT), F32)),
        grid=(2, n // tm),
        in_specs=[
            pl.BlockSpec((tm, D_MODEL), lambda j, i: (i, 0)),
            pl.BlockSpec((1, D_MODEL), lambda j, i: (0, 0)),
            pl.BlockSpec((P_HALF, D_MODEL), lambda j, i: (j, 0)),
            pl.BlockSpec((2 * DN_HEADS, D_MODEL), lambda j, i: (0, 0)),
            pl.BlockSpec((DN_HEADS, 1), lambda j, i: (0, 0)),
            pl.BlockSpec((DN_HEADS, 1), lambda j, i: (0, 0)),
            pl.BlockSpec((4, DN_QKV), lambda j, i: (0, 0)),
            pl.BlockSpec((1, RW_SHIFT_W), lambda j, i: (0, 0)),
            pl.BlockSpec((sbp, SUBLANES, DN_QKV), lambda j, i: (st_idx(i), 0, 0)),
            pl.BlockSpec((sbp, SUBLANES, RW_SHIFT_W), lambda j, i: (st_idx(i), 0, 0)),
        ],
        out_specs=(
            pl.BlockSpec((tm, P_HALF), lambda j, i: (i, j)),
            pl.BlockSpec((tm // rc, 2 * DN_HEADS, rc),
                         lambda j, i: (i * (1 - j) + (n // tm - 1) * j, 0, 0)),
            pl.BlockSpec((sbp, 3, PASS0_CONV), first_pass),
            pl.BlockSpec((sbp, 1, PASS0_SHIFT), first_pass),
            pl.BlockSpec((sbp, 3, DN_QKV - PASS0_CONV), second_pass),
            pl.BlockSpec((sbp, 1, RW_SHIFT_W - PASS0_SHIFT), second_pass),
        ),
        scratch_shapes=[pltpu.VMEM((sbp, 2 * SUBLANES, DN_QKV + RW_SHIFT_W), F32)],
        compiler_params=pltpu.CompilerParams(
            dimension_semantics=("arbitrary", "arbitrary"), vmem_limit_bytes=VMEM_LIMIT),
        name="proj",
    )(x2d, g_norm, w_main_t, w_ab_t, a_log, dt_bias, conv_w, mu, conv_buf8, prev8)
    conv_new = jnp.concatenate([cout0, cout1], axis=-1)
    shift_new = jnp.concatenate([shift0, shift1], axis=-1)
    return p, gb, conv_new, shift_new


def _dn_kernel(q_ref, k_ref, v_ref, z_ref, gb_ref, nw_ref, s0_ref, o_ref, sout_ref, s_scr, *, sb, tc, nc, nt):
    t = pl.program_id(1)
    ru = sb * tc
    hw = DN_HEADS * DN_D

    @pl.when(t == 0)
    def _():
        s_scr[...] = s0_ref[...]

    ri, ci = _iota2(ru, ru)
    same = _group(ri, tc) == _group(ci, tc)
    incl = same & (ri >= ci)
    strict = same & (ri > ci)
    eye = ri == ci
    seqs = [slice(s * tc, (s + 1) * tc) for s in range(sb)]
    units = [slice(c * ru, (c + 1) * ru) for c in range(nc)]

    def unit_chain(h, rows, q, k, v):
        g_row = gb_ref[0, h:h + 1, :][:, rows]
        b_row = gb_ref[0, DN_HEADS + h:DN_HEADS + h + 1, :][:, rows]
        g_col = jnp.sum(jnp.where(incl, g_row, 0.0), axis=1, keepdims=True)
        g_cum_row = jnp.sum(jnp.where(eye, g_col, 0.0), axis=0, keepdims=True)
        g_tot = jnp.sum(jnp.where(same, g_row, 0.0), axis=1, keepdims=True)
        b_col = jnp.sum(jnp.where(eye, b_row, 0.0), axis=1, keepdims=True)
        decay = jnp.where(incl, jnp.exp(jnp.minimum(g_col - g_cum_row, 0.0)), 0.0)
        e_g = jnp.exp(g_col)
        e_rest = jnp.exp(g_tot - g_col)
        kb = k * b_col
        a_raw, qk_raw = _dot_rows([kb, q], k, dot=_dot_nt)
        yield
        a = jnp.where(strict, a_raw * decay, 0.0)
        aqk = qk_raw * decay
        t_inv = yield from _unit_lower_inverse(a, ri, ci, tc)
        rhs = jnp.concatenate([v * b_col, kb * e_g], axis=1)
        sol = _dot(t_inv, rhs)
        yield
        return dict(u_all=sol[:, :DN_D], w_all=sol[:, DN_D:], aqk=aqk, qg=q * e_g, kd=k * e_rest,
                    g_tot=g_tot)

    def head_chain(h):
        cols = slice(h * DN_D, (h + 1) * DN_D)
        q = q_ref[:, cols]
        k = k_ref[:, cols]
        v = v_ref[:, cols]
        wy = yield from _lockstep(unit_chain(h, rows, q[rows], k[rows], v[rows]) for rows in units)
        for rows, w in zip(units, wy):
            ws = [_dot(jnp.concatenate([w["w_all"][sq], w["qg"][sq]], axis=0), s_scr[s, h])
                  for s, sq in enumerate(seqs)]
            yield
            u = _rows([w["u_all"][sq] - x[:tc] for sq, x in zip(seqs, ws)])
            q_s = _rows([x[tc:] for x in ws])
            intra = _dot(w["aqk"], u)
            upd = [_dot_tn(w["kd"][sq], u[sq]) for sq in seqs]
            yield
            for s in range(sb):
                gl = jnp.exp(w["g_tot"][s * tc:s * tc + 1, :])
                s_scr[s, h] = s_scr[s, h] * gl + upd[s]
            o = q_s + intra
            o = o * lax.rsqrt(jnp.mean(o * o, axis=-1, keepdims=True) + NORM_EPS) * nw_ref[...]
            o_ref[rows, cols] = o * _silu(z_ref[rows, cols])

    _run(_lockstep(head_chain(h) for h in range(DN_HEADS)))

    @pl.when(t == nt - 1)
    def _():
        sout_ref[...] = s_scr[...]


def _dn(p, gb, norm_w, s0, *, nb, seq, sb, tc, nc):
    nt = seq // (nc * tc)
    r = sb * nc * tc
    assert seq % (nc * tc) == 0 and nb % sb == 0 and (nt == 1 or sb == 1) and (nc == 1 or sb == 1)
    bcast = s0.shape[0] == 1
    assert not bcast or sb == 1
    hw = DN_HEADS * DN_D

    def st_idx(i):
        return 0 if bcast else i

    def rows(name):
        return pl.BlockSpec((r, hw), lambda i, t: (i * nt + t, COL[name] // hw))

    return pl.pallas_call(
        functools.partial(_dn_kernel, sb=sb, tc=tc, nc=nc, nt=nt),
        out_shape=(jax.ShapeDtypeStruct((nb * seq, hw), F32),
                   jax.ShapeDtypeStruct((nb, DN_HEADS, DN_D, DN_D), F32)),
        grid=(nb // sb, nt),
        in_specs=[
            rows("q"), rows("k"), rows("v"), rows("z"),
            pl.BlockSpec((1, 2 * DN_HEADS, r), lambda i, t: (i * nt + t, 0, 0)),
            pl.BlockSpec((1, DN_D), lambda i, t: (0, 0)),
            pl.BlockSpec((sb, DN_HEADS, DN_D, DN_D), lambda i, t: (st_idx(i), 0, 0, 0)),
        ],
        out_specs=(
            pl.BlockSpec((r, hw), lambda i, t: (i * nt + t, 0)),
            pl.BlockSpec((sb, DN_HEADS, DN_D, DN_D), lambda i, t: (i, 0, 0, 0)),
        ),
        scratch_shapes=[pltpu.VMEM((sb, DN_HEADS, DN_D, DN_D), F32)],
        compiler_params=pltpu.CompilerParams(
            dimension_semantics=("arbitrary", "arbitrary"), vmem_limit_bytes=VMEM_LIMIT),
        name="dn",
    )(p, p, p, p, gb, norm_w, s0)


def _rw_kernel(r_ref, k_ref, v_ref, l_ref, w0_ref, a0_ref, kkw_ref, kaw_ref, rkw_ref,
               gnw_ref, gnb_ref, w2_ref, a2_ref, g2_ref, s0_ref, odn_ref, ga_ref, gb_ref,
               o_ref, sout_ref, s_scr, *, sb, tc, nc, nt):
    t = pl.program_id(1)
    ru = sb * tc
    half = lax.broadcasted_iota(jnp.int32, (1, LANES), 1) < RW_HD
    ri128, ci128 = _iota2(LANES, LANES)
    bd_mask = _group(ri128, RW_HD) == _group(ci128, RW_HD)

    def seg_sum(x):
        first = jnp.sum(jnp.where(half, x, 0.0), axis=-1, keepdims=True)
        second = jnp.sum(jnp.where(half, 0.0, x), axis=-1, keepdims=True)
        return jnp.where(half, first, second)

    @pl.when(t == 0)
    def _():
        zero = jnp.zeros((RW_HD, RW_HD), F32)
        for s in range(sb):
            for p in range(RW_PAIRS):
                top = jnp.concatenate([s0_ref[s, 2 * p], zero], axis=1)
                bot = jnp.concatenate([zero, s0_ref[s, 2 * p + 1]], axis=1)
                s_scr[s, p] = jnp.concatenate([top, bot], axis=0)

    xm_l = l_ref[...]
    lora_w = jnp.tanh(xm_l).astype(BF16)
    lora_a = xm_l.astype(BF16)
    lora_g = _sigmoid(xm_l).astype(BF16)

    ri, ci = _iota2(ru, ru)
    cum_mask = jnp.where((_group(ri, tc) == _group(ci, tc)) & (ri >= ci), 1.0, 0.0).astype(BF16)
    rip, cip = _iota2(ru, 2 * ru)
    cc = cip & (ru - 1)
    same_p = _group(rip, tc) == _group(cc, tc)
    incl_p = same_p & (rip >= cc)
    strict_p = same_p & (rip > cc)
    first_cols = lax.broadcasted_iota(jnp.int32, (1, 2 * ru), 1) < ru
    seqs = [slice(s * tc, (s + 1) * tc) for s in range(sb)]
    units = [slice(c * ru, (c + 1) * ru) for c in range(nc)]

    def stack(x):
        return jnp.concatenate([jnp.where(half, x, 0.0), jnp.where(half, 0.0, x)], axis=0)

    def pdot(x, y):
        y_bd = jnp.concatenate([jnp.where(first_cols, y, 0.0), jnp.where(first_cols, 0.0, y)], axis=0)
        return _dot(x, y_bd)

    def unit_chain(rr, k2, vv, kk, bb, logw):
        lcum = _dot_mask(cum_mask, logw)
        yield
        ltot = jnp.broadcast_to(lcum.reshape(sb, tc, LANES)[:, tc - 1:tc, :],
                                (sb, tc, LANES)).reshape(ru, LANES)
        p_in = jnp.exp(lcum)
        p_inv = jnp.exp(-lcum)
        alpha = kk * jnp.exp(lcum - logw)
        beta = bb * p_inv
        kt = k2 * p_inv
        rt = rr * p_in
        e_rest = jnp.exp(ltot - lcum)
        beta2, kt2, v2 = stack(beta), stack(kt), stack(vv)
        l_raw, rb_raw = _dot_rows([alpha, rt], beta2, dot=_dot_nt)
        lk_raw, rk_raw = _dot_rows([alpha, rt], kt2, dot=_dot_nt)
        yield
        lmat = jnp.where(strict_p, l_raw, 0.0)
        rb = jnp.where(incl_p, rb_raw, 0.0)
        lk = jnp.where(strict_p, lk_raw, 0.0)
        rk = jnp.where(incl_p, rk_raw, 0.0)
        lkv, rkv = _dot_rows([lk, rk], v2)
        t_inv = yield from _unit_lower_inverse(lmat, rip, cc, tc, dot=pdot)
        sol = _dot(t_inv, jnp.concatenate([stack(alpha), stack(lkv)], axis=1))
        yield
        return dict(wa=sol[:, :LANES], uv=sol[:, LANES:], rt=rt, rb=rb, rkv=rkv, p_in=p_in,
                    k_end=k2 * e_rest, b_end=bb * e_rest)

    def pair_chain(p):
        c = slice(p * LANES, (p + 1) * LANES)
        rr = r_ref[:, c]
        xk = k_ref[:, c]
        vv = v_ref[:, c]
        lw = jnp.dot(lora_w, w2_ref[:, c], preferred_element_type=F32)
        la = jnp.dot(lora_a, a2_ref[:, c], preferred_element_type=F32)
        gate = jnp.dot(lora_g, g2_ref[:, c], preferred_element_type=F32)
        yield
        w_log = -_softplus(-(w0_ref[:, c] + lw)) - 0.5
        logw = -jnp.exp(w_log)
        aa = _sigmoid(a0_ref[:, c] + la)
        kk = xk * kkw_ref[:, c]
        k2 = xk * (1.0 + (aa - 1.0) * kaw_ref[:, c])
        kk = kk * lax.rsqrt(seg_sum(kk * kk) + 1e-6)
        bb = kk * aa
        wy = yield from _lockstep(
            unit_chain(rr[rows], k2[rows], vv[rows], kk[rows], bb[rows], logw[rows]) for rows in units)
        for rows, w in zip(units, wy):
            st = [_dot_nt(jnp.concatenate([w["wa"][sq], w["rt"][sq]], axis=0), s_scr[s, p])
                  for s, sq in enumerate(seqs)]
            yield
            u = w["uv"] + _rows([x[:tc] for x in st])
            r_state = _rows([x[tc:] for x in st])
            rbu = _dot(w["rb"], stack(u))
            v_u = vv[rows]
            upd = [_dot_tn(jnp.concatenate([v_u[sq], -u[sq]], axis=0),
                           jnp.concatenate([w["k_end"][sq], w["b_end"][sq]], axis=0)) for sq in seqs]
            yield
            for s in range(sb):
                p_end = w["p_in"][(s + 1) * tc - 1:(s + 1) * tc, :]
                s_scr[s, p] = s_scr[s, p] * p_end + jnp.where(bd_mask, upd[s], 0.0)
            o = r_state + (w["rkv"] - rbu)
            mean = seg_sum(o) * (1.0 / RW_HD)
            d = o - mean
            var = seg_sum(d * d) * (1.0 / RW_HD)
            bonus = seg_sum(rr[rows] * k2[rows] * rkw_ref[:, c])
            on = d * lax.rsqrt(var + RW_GN_EPS) * gnw_ref[:, c] + gnb_ref[:, c]
            o_rw = (on + bonus * v_u) * gate[rows]
            mix = _sigmoid(ga_ref[rows, c]) * odn_ref[rows, c] + _sigmoid(gb_ref[rows, c]) * o_rw
            o_ref[rows, c] = mix.astype(BF16)

    _run(_lockstep(pair_chain(p) for p in range(RW_PAIRS)))

    @pl.when(t == nt - 1)
    def _():
        for s in range(sb):
            for p in range(RW_PAIRS):
                sp = s_scr[s, p]
                sout_ref[s, 2 * p] = sp[0:RW_HD, 0:RW_HD]
                sout_ref[s, 2 * p + 1] = sp[RW_HD:, RW_HD:]


def _rw(p, o_dn, weights, s0, *, nb, seq, sb, tc, nc):
    nt = seq // (nc * tc)
    r = sb * nc * tc
    assert seq % (nc * tc) == 0 and nb % sb == 0 and (nt == 1 or sb == 1) and (nc == 1 or sb == 1)
    bcast = s0.shape[0] == 1
    assert not bcast or sb == 1
    (w0, a0, kkw, kaw, rkw, gnw, gnb, w2p, a2p, g2p) = weights

    def st_idx(i):
        return 0 if bcast else i

    def vec(width):
        return pl.BlockSpec((1, width), lambda i, t: (0, 0))

    def lora_w():
        return pl.BlockSpec((RW_LORA, RW_W), lambda i, t: (0, 0))

    def rows(col):
        return pl.BlockSpec((r, RW_W), lambda i, t: (i * nt + t, col))

    def named(name):
        return rows(COL[name] // RW_W)

    return pl.pallas_call(
        functools.partial(_rw_kernel, sb=sb, tc=tc, nc=nc, nt=nt),
        out_shape=(jax.ShapeDtypeStruct((nb * seq, RW_W), BF16),
                   jax.ShapeDtypeStruct((nb, RW_HEADS, RW_HD, RW_HD), F32)),
        grid=(nb // sb, nt),
        in_specs=[
            named("r"), named("kx"), named("vx"),
            pl.BlockSpec((r, RW_LORA), lambda i, t: (i * nt + t, COL["lora"] // RW_LORA)),
            vec(RW_W), vec(RW_W), vec(RW_W), vec(RW_W), vec(RW_W), vec(RW_W), vec(RW_W),
            lora_w(), lora_w(), lora_w(),
            pl.BlockSpec((sb, RW_HEADS, RW_HD, RW_HD), lambda i, t: (st_idx(i), 0, 0, 0)),
            rows(0), named("ga"), named("gb"),
        ],
        out_specs=(
            rows(0),
            pl.BlockSpec((sb, RW_HEADS, RW_HD, RW_HD), lambda i, t: (i, 0, 0, 0)),
        ),
        scratch_shapes=[pltpu.VMEM((sb, RW_PAIRS, LANES, LANES), F32)],
        compiler_params=pltpu.CompilerParams(
            dimension_semantics=("arbitrary", "arbitrary"), vmem_limit_bytes=VMEM_LIMIT),
        name="rw",
    )(p, p, p, p, w0, a0, kkw, kaw, rkw, gnw, gnb, w2p, a2p, g2p, s0, o_dn, p, p)


def _merge_kernel(x_ref, mix_ref, wout_ref, gffn_ref, w1_ref, w2_ref, gfin_ref,
                  y_ref, x1_scr, h2_scr, acc_scr, *, nf):
    f = pl.program_id(1)

    @pl.when(f == 0)
    def _():
        x1 = x_ref[...] + jnp.dot(mix_ref[...], wout_ref[...], preferred_element_type=F32)
        x1_scr[...] = x1
        h2 = x1 * lax.rsqrt(jnp.mean(x1 * x1, axis=-1, keepdims=True) + NORM_EPS) * gffn_ref[...]
        h2_scr[...] = h2.astype(BF16)
        acc_scr[...] = jnp.zeros_like(acc_scr)

    hid = jnp.dot(h2_scr[...], w1_ref[...], preferred_element_type=F32)
    hid = jnp.square(jnp.maximum(hid, 0.0))
    acc_scr[...] += jnp.dot(hid.astype(BF16), w2_ref[...], preferred_element_type=F32)

    @pl.when(f == nf - 1)
    def _():
        y = x1_scr[...] + acc_scr[...]
        y_ref[...] = y * lax.rsqrt(jnp.mean(y * y, axis=-1, keepdims=True) + NORM_EPS) * gfin_ref[...]


def _merge(x2d, mix, w_out, g_ffn, w1, w2, g_fin, *, tm, tf):
    n = x2d.shape[0]
    nf = D_FF // tf
    assert n % tm == 0 and D_FF % tf == 0
    row = lambda i, f: (i, 0)
    return pl.pallas_call(
        functools.partial(_merge_kernel, nf=nf),
        out_shape=jax.ShapeDtypeStruct((n, D_MODEL), F32),
        grid=(n // tm, nf),
        in_specs=[
            pl.BlockSpec((tm, D_MODEL), row),
            pl.BlockSpec((tm, D_MODEL), row),
            pl.BlockSpec((D_MODEL, D_MODEL), lambda i, f: (0, 0)),
            pl.BlockSpec((1, D_MODEL), lambda i, f: (0, 0)),
            pl.BlockSpec((D_MODEL, tf), lambda i, f: (0, f)),
            pl.BlockSpec((tf, D_MODEL), lambda i, f: (f, 0)),
            pl.BlockSpec((1, D_MODEL), lambda i, f: (0, 0)),
        ],
        out_specs=pl.BlockSpec((tm, D_MODEL), row),
        scratch_shapes=[
            pltpu.VMEM((tm, D_MODEL), F32),
            pltpu.VMEM((tm, D_MODEL), BF16),
            pltpu.VMEM((tm, D_MODEL), F32),
        ],
        compiler_params=pltpu.CompilerParams(
            dimension_semantics=("arbitrary", "arbitrary"), vmem_limit_bytes=VMEM_LIMIT),
        name="merge",
    )(x2d, mix, w_out, g_ffn, w1, w2, g_fin)


def _pad_rows_front(x, rows):
    b, n, w = x.shape
    return jnp.concatenate([jnp.zeros((b, rows - n, w), x.dtype), x], axis=1)


def _layer(x, conv_buf, dn_s, rw_prev, rw_s, wts, *, sb, tc, nc, tm, tm_mlp):
    nb, seq, _ = x.shape
    n = nb * seq
    x2d = x.reshape(n, D_MODEL)
    p, gb, conv_new, shift_new = _proj(
        x2d, wts["g_mix"], wts["w_main_t"], wts["w_ab_t"], wts["a_log"], wts["dt_bias"], wts["conv_w"],
        wts["rw_mu"], _pad_rows_front(conv_buf, SUBLANES), _pad_rows_front(rw_prev[:, None, :], SUBLANES),
        nb=nb, seq=seq, tm=tm, rc=sb * nc * tc)
    o_dn, dn_new = _dn(p, gb, wts["dn_norm_w"], dn_s, nb=nb, seq=seq, sb=sb, tc=tc, nc=nc)
    mix, rw_new = _rw(p, o_dn, wts["rw"], rw_s, nb=nb, seq=seq, sb=sb, tc=tc, nc=nc)
    y = _merge(x2d, mix, wts["w_out"], wts["g_ffn"], wts["w_ff1"], wts["w_ff2"], wts["g_final"],
               tm=tm_mlp, tf=1024)
    return y.reshape(nb, seq, D_MODEL), conv_new, dn_new, shift_new[:, 0], rw_new


def kernel(x_prompt, x_sample, state_dn_conv, state_dn, state_rw_shift, state_rw, meta_tokens, g_mix_norm, w_in, dn_conv_w, dn_a_log, dn_dt_bias, dn_norm_w, rw_mu, rw_w0, rw_w2, rw_a0, rw_a2, rw_g2, rw_k_k, rw_k_a, rw_r_k, rw_gn_w, rw_gn_b, w_out, g_ffn_norm, w_ff1, w_ff2, g_final):
    assert g_mix_norm.shape[0] == 1, "single layer"
    w_t = w_in[0].T
    o_a = DN_QKV + D_MODEL
    o_rwp = o_a + 2 * DN_HEADS
    o_gate = o_rwp + RW_SHIFT_W
    src = {"q": 0, "k": D_MODEL, "v": 2 * D_MODEL, "z": DN_QKV, "r": o_rwp, "kx": o_rwp + RW_W,
           "vx": o_rwp + 2 * RW_W, "lora": o_rwp + 3 * RW_W, "ga": o_gate, "gb": o_gate + D_MODEL}
    w_main_t = jnp.concatenate([w_t[src[name]:src[name] + width] for name, width, _, _ in SEGMENTS],
                               axis=0).astype(BF16)
    w_ab_t = w_t[o_a:o_rwp].astype(BF16)

    def lora_pad(wl, lo):
        return jnp.zeros((RW_LORA, RW_W), F32).at[lo:lo + wl.shape[0]].set(wl).astype(BF16)

    row = lambda v: v.reshape(1, -1).astype(F32)
    wts = {
        "g_mix": row(g_mix_norm[0]), "w_main_t": w_main_t, "w_ab_t": w_ab_t,
        "a_log": dn_a_log[0].reshape(DN_HEADS, 1), "dt_bias": dn_dt_bias[0].reshape(DN_HEADS, 1),
        "conv_w": dn_conv_w[0], "dn_norm_w": row(dn_norm_w[0]), "rw_mu": row(rw_mu[0]),
        "rw": (row(rw_w0[0]), row(rw_a0[0]), row(rw_k_k[0]), row(rw_k_a[0]), row(rw_r_k[0]),
               row(rw_gn_w[0]), row(rw_gn_b[0]),
               lora_pad(rw_w2[0], 0), lora_pad(rw_a2[0], 64), lora_pad(rw_g2[0], 128)),
        "w_out": w_out[0].astype(BF16), "g_ffn": row(g_ffn_norm[0]),
        "w_ff1": w_ff1[0].astype(BF16), "w_ff2": w_ff2[0].astype(BF16), "g_final": row(g_final),
    }

    nbm = SUBLANES
    xm = jnp.broadcast_to(meta_tokens.astype(F32)[None], (nbm, N_META, D_MODEL))
    _, conv_m, dn_m, shift_m, rw_m = _layer(
        xm, jnp.zeros((nbm, 3, DN_QKV), F32), jnp.zeros((nbm, DN_HEADS, DN_D, DN_D), F32),
        jnp.zeros((nbm, RW_SHIFT_W), F32), jnp.zeros((nbm, RW_HEADS, RW_HD, RW_HD), F32), wts,
        sb=nbm, tc=N_META, nc=1, tm=nbm * N_META, tm_mlp=nbm * N_META)

    y_p, conv_p, dn_p, shift_p, rw_p = _layer(
        x_prompt, conv_m[:1], dn_m[:1], shift_m[:1], rw_m[:1], wts, sb=1, tc=64, nc=4, tm=512, tm_mlp=1024)

    dec_len = x_sample.shape[1]
    y_s, conv_s, dn_s, shift_s, rw_s = _layer(
        x_sample, state_dn_conv[0], state_dn[0], state_rw_shift[0], state_rw[0], wts,
        sb=SUBLANES, tc=dec_len, nc=1, tm=128, tm_mlp=1024)

    return (y_p, y_s, conv_p[None], dn_p[None], shift_p[None], rw_p[None],
            conv_s[None], dn_s[None], shift_s[None], rw_s[None])
```

```python
import functools
import math

import jax
import jax.numpy as jnp
from jax import lax
from jax.experimental import pallas as pl
from jax.experimental.pallas import tpu as pltpu

F32 = jnp.float32
BF16 = jnp.bfloat16

D_MODEL = 1024
N_META = 16
DN_HEADS = 8
DN_D = 128
DN_QKV = 3 * DN_HEADS * DN_D
RW_HEADS = 16
RW_HD = 64
RW_PAIRS = RW_HEADS // 2
RW_W = RW_HEADS * RW_HD
RW_LORA = 256
RW_SHIFT_W = 3 * RW_W + RW_LORA
D_FF = 4 * D_MODEL
NORM_EPS = 1e-6
RW_GN_EPS = 64e-5
LANES = 128
SUBLANES = 8
INV_BLOCK = 16

SEGMENTS = (
    ("q", D_MODEL, "conv", 0), ("k", D_MODEL, "conv", D_MODEL),
    ("r", RW_W, "shift", 0), ("kx", RW_W, "shift", RW_W),
    ("ga", D_MODEL, "raw", 0), ("gb", D_MODEL, "raw", 0),
    ("v", D_MODEL, "conv", 2 * D_MODEL), ("z", D_MODEL, "raw", 0),
    ("vx", RW_W, "shift", 2 * RW_W), ("lora", RW_LORA, "shift", 3 * RW_W),
)
COL = {}
P_W = 0
for _name, _width, _, _ in SEGMENTS:
    COL[_name] = P_W
    P_W += _width
P_HALF = P_W // 2
PASS0_CONV = 2 * D_MODEL
PASS0_SHIFT = 2 * RW_W


def _chunk_plan(half):
    lo, hi = half * P_HALF, (half + 1) * P_HALF
    plan = []
    for name, width, kind, off in SEGMENTS:
        start = COL[name]
        c = max(start, lo)
        while c < min(start + width, hi):
            w = min(512, min(start + width, hi) - c)
            plan.append((c - lo, w, kind, off + (c - start), name))
            c += w
    return plan

VMEM_LIMIT = 56 * 1024 * 1024


def _sigmoid(x):
    return 1.0 / (1.0 + jnp.exp(-x))


def _silu(x):
    return x * _sigmoid(x)


def _softplus(x):
    return jnp.maximum(x, 0.0) + jnp.log(1.0 + jnp.exp(-jnp.abs(x)))


def _dot(a, b):
    return jnp.dot(a.astype(BF16), b.astype(BF16), preferred_element_type=F32)


def _dot_nt(a, b):
    return lax.dot_general(a.astype(BF16), b.astype(BF16), (((1,), (1,)), ((), ())),
                           preferred_element_type=F32)


def _dot_tn(a, b):
    return lax.dot_general(a.astype(BF16), b.astype(BF16), (((0,), (0,)), ((), ())),
                           preferred_element_type=F32)


def _rows(xs):
    return xs[0] if len(xs) == 1 else jnp.concatenate(xs, axis=0)


def _dot_rows(xs, b, dot=_dot):
    out = dot(_rows(xs), b)
    off, parts = 0, []
    for x in xs:
        parts.append(out[off:off + x.shape[0]])
        off += x.shape[0]
    return parts


def _dot_mask(mask_bf16, x):
    h1 = x.astype(BF16)
    r1 = x - h1.astype(F32)
    h2 = r1.astype(BF16)
    h3 = (r1 - h2.astype(F32)).astype(BF16)
    return (jnp.dot(mask_bf16, h1, preferred_element_type=F32)
            + (jnp.dot(mask_bf16, h2, preferred_element_type=F32)
               + jnp.dot(mask_bf16, h3, preferred_element_type=F32)))


def _iota2(n, m):
    return (lax.broadcasted_iota(jnp.int32, (n, m), 0),
            lax.broadcasted_iota(jnp.int32, (n, m), 1))


def _group(idx, size):
    return lax.shift_right_logical(idx, int(math.log2(size)))


def _column_packing(ru, tc):
    ri, ci = _iota2(ru, 2 * ru)
    cc = ci & (ru - 1)
    same = _group(ri, tc) == _group(cc, tc)
    first = lax.broadcasted_iota(jnp.int32, (1, 2 * ru), 1) < ru

    def pdot(x, y):
        y_bd = jnp.concatenate([jnp.where(first, y, 0.0), jnp.where(first, 0.0, y)], axis=0)
        return _dot(x, y_bd)

    return ri, cc, same, first, pdot


def _lockstep(chains):
    chains = list(chains)
    results = [None] * len(chains)
    active = list(range(len(chains)))
    while active:
        for i in list(active):
            try:
                next(chains[i])
            except StopIteration as done:
                results[i] = done.value
                active.remove(i)
        if active:
            yield
    return results


def _run(chain):
    for _ in chain:
        pass


def _neumann_inverse(a, eye, nil, dot):
    inv = eye - a
    if nil <= 2:
        return inv
    power = dot(a, a)
    yield
    k = 4
    while k < nil:
        step, power = _dot_rows([inv, power], power, dot=dot)
        yield
        inv = inv + step
        k *= 2
    step = dot(inv, power)
    yield
    return inv + step


def _unit_lower_inverse(a, ri, ci, tc, dot=_dot):
    eye = jnp.where(ri == ci, 1.0, 0.0).astype(F32)
    if tc <= INV_BLOCK:
        return (yield from _neumann_inverse(a, eye, tc, dot))
    assert tc // INV_BLOCK <= 4
    diag = _group(ri, INV_BLOCK) == _group(ci, INV_BLOCK)
    d = jnp.where(diag, a, 0.0)
    low = a - d
    dinv = yield from _neumann_inverse(d, eye, INV_BLOCK, dot)
    b = dot(low, dinv)
    yield
    db, bb = _dot_rows([dinv, b], b, dot=dot)
    yield
    p1 = dinv - db
    tail = dot(p1, bb)
    yield
    return p1 + tail


def _proj_kernel(x_ref, g_ref, wt_ref, wab_ref, alog_ref, dtb_ref, cw_ref, mu_ref, cb_ref, prev_ref,
                 p_ref, gb_ref, cout0_ref, shift0_ref, cout1_ref, shift1_ref, hist, *, tm, rc, sbp, ttp, ntp):
    j = pl.program_id(0)
    i = pl.program_id(1)
    tpos = lax.rem(i, ntp)
    x = x_ref[...]
    h = x * lax.rsqrt(jnp.mean(x * x, axis=-1, keepdims=True) + NORM_EPS) * g_ref[...]
    hb = h.astype(BF16)
    nt_dims = (((1,), (1,)), ((), ()))

    def mm(c0, cw):
        return lax.dot_general(hb, wt_ref[c0:c0 + cw, :], nt_dims, preferred_element_type=F32)

    seq_pos = lax.broadcasted_iota(jnp.int32, (tm, 1), 0) & (ttp - 1)

    def shift_rows(a, first):
        rolled = pltpu.roll(a, len(first), axis=0)
        if sbp == 1:
            head = rolled[0:SUBLANES]
            for i, f in enumerate(first):
                head = jnp.where(seq_pos[0:SUBLANES] == i, f[0], head)
            return jnp.concatenate([head, rolled[SUBLANES:]], axis=0)
        for i, f in enumerate(first):
            fill = jnp.broadcast_to(f, (sbp, ttp, a.shape[1])).reshape(tm, a.shape[1])
            rolled = jnp.where(seq_pos == i, fill, rolled)
        return rolled

    def run_pass(plan):
        @pl.when(tpos == 0)
        def _():
            hist[:, 0:SUBLANES, 0:DN_QKV] = cb_ref[...]
            hist[:, 0:SUBLANES, DN_QKV:] = prev_ref[...]

        @pl.when(tpos > 0)
        def _():
            hist[:, 0:SUBLANES, :] = hist[:, SUBLANES:, :]

        for c0, cw, kind, off, name in plan:
            cur = mm(c0, cw)
            if kind == "raw":
                p_ref[:, c0:c0 + cw] = cur
                continue
            hc = slice(off, off + cw) if kind == "conv" else slice(DN_QKV + off, DN_QKV + off + cw)
            old = hist[:, 0:SUBLANES, hc]
            hist[:, SUBLANES:, hc] = cur.reshape(sbp, ttp, cw)[:, ttp - SUBLANES:, :]
            x1 = old[:, 7:8, :]
            if kind == "shift":
                prev = shift_rows(cur, [x1])
                p_ref[:, c0:c0 + cw] = cur + (prev - cur) * mu_ref[:, off:off + cw]
                continue
            w = [cw_ref[tap:tap + 1, off:off + cw] for tap in range(4)]
            x2, x3 = old[:, 6:7, :], old[:, 5:6, :]
            y1 = shift_rows(cur, [x1])
            far = cur * w[1] + y1 * w[0]
            acc = cur * w[3] + y1 * w[2] + shift_rows(far, [x2 * w[1] + x3 * w[0], x1 * w[1] + x2 * w[0]])
            y = _silu(acc)
            if name == "v":
                p_ref[:, c0:c0 + cw] = y
            else:
                scale = DN_D ** -0.5 if name == "q" else 1.0
                for c1 in range(0, cw, DN_D):
                    yh = y[:, c1:c1 + DN_D]
                    yh = yh * (lax.rsqrt(jnp.sum(yh * yh, axis=-1, keepdims=True) + 1e-6) * scale)
                    p_ref[:, c0 + c1:c0 + c1 + DN_D] = yh

    @pl.when(j == 0)
    def _():
        run_pass(_chunk_plan(0))
        for c in range(tm // rc):
            ab = lax.dot_general(wab_ref[...], hb[c * rc:(c + 1) * rc], nt_dims,
                                 preferred_element_type=F32)
            g = -jnp.exp(alog_ref[...]) * _softplus(ab[0:DN_HEADS] + dtb_ref[...])
            gb_ref[c, 0:DN_HEADS, :] = g
            gb_ref[c, DN_HEADS:2 * DN_HEADS, :] = _sigmoid(ab[DN_HEADS:2 * DN_HEADS])

        @pl.when(tpos == ntp - 1)
        def _():
            cout0_ref[...] = hist[:, 2 * SUBLANES - 3:, 0:PASS0_CONV]
            shift0_ref[...] = hist[:, 2 * SUBLANES - 1:, DN_QKV:DN_QKV + PASS0_SHIFT]

    @pl.when(j == 1)
    def _():
        run_pass(_chunk_plan(1))

        @pl.when(tpos == ntp - 1)
        def _():
            cout1_ref[...] = hist[:, 2 * SUBLANES - 3:, PASS0_CONV:DN_QKV]
            shift1_ref[...] = hist[:, 2 * SUBLANES - 1:, DN_QKV + PASS0_SHIFT:]


def _proj(x2d, g_norm, w_main_t, w_ab_t, a_log, dt_bias, conv_w, mu, conv_buf8, prev8, *, nb, seq, tm, rc):
    n = nb * seq
    ttp = min(seq, tm)
    sbp = tm // ttp
    ntp = seq // ttp
    assert n % tm == 0 and tm % rc == 0 and tm % ttp == 0 and seq % ttp == 0
    bcast = conv_buf8.shape[0] == 1
    assert bcast == (prev8.shape[0] == 1) and (not bcast or sbp == 1)
    last = (n // tm - 1) // ntp

    def st_idx(i):
        return 0 if bcast else i // ntp

    assert ttp % SUBLANES == 0 and ttp & (ttp - 1) == 0
    first_pass = lambda j, i: ((i // ntp) * (1 - j) + last * j, 0, 0)
    second_pass = lambda j, i: ((i // ntp) * j, 0, 0)
    p, gb, cout0, shift0, cout1, shift1 = pl.pallas_call(
        functools.partial(_proj_kernel, tm=tm, rc=rc, sbp=sbp, ttp=ttp, ntp=ntp),
        out_shape=(jax.ShapeDtypeStruct((n, P_W), F32),
                   jax.ShapeDtypeStruct((n // rc, 2 * DN_HEADS, rc), F32),
                   jax.ShapeDtypeStruct((nb, 3, PASS0_CONV), F32),
                   jax.ShapeDtypeStruct((nb, 1, PASS0_SHIFT), F32),
                   jax.ShapeDtypeStruct((nb, 3, DN_QKV - PASS0_CONV), F32),
                   jax.ShapeDtypeStruct((nb, 1, RW_SHIFT_W - PASS0_SHIFT), F32)),
        grid=(2, n // tm),
        in_specs=[
            pl.BlockSpec((tm, D_MODEL), lambda j, i: (i, 0)),
            pl.BlockSpec((1, D_MODEL), lambda j, i: (0, 0)),
            pl.BlockSpec((P_HALF, D_MODEL), lambda j, i: (j, 0)),
            pl.BlockSpec((2 * DN_HEADS, D_MODEL), lambda j, i: (0, 0)),
            pl.BlockSpec((DN_HEADS, 1), lambda j, i: (0, 0)),
            pl.BlockSpec((DN_HEADS, 1), lambda j, i: (0, 0)),
            pl.BlockSpec((4, DN_QKV), lambda j, i: (0, 0)),
            pl.BlockSpec((1, RW_SHIFT_W), lambda j, i: (0, 0)),
            pl.BlockSpec((sbp, SUBLANES, DN_QKV), lambda j, i: (st_idx(i), 0, 0)),
            pl.BlockSpec((sbp, SUBLANES, RW_SHIFT_W), lambda j, i: (st_idx(i), 0, 0)),
        ],
        out_specs=(
            pl.BlockSpec((tm, P_HALF), lambda j, i: (i, j)),
            pl.BlockSpec((tm // rc, 2 * DN_HEADS, rc),
                         lambda j, i: (i * (1 - j) + (n // tm - 1) * j, 0, 0)),
            pl.BlockSpec((sbp, 3, PASS0_CONV), first_pass),
            pl.BlockSpec((sbp, 1, PASS0_SHIFT), first_pass),
            pl.BlockSpec((sbp, 3, DN_QKV - PASS0_CONV), second_pass),
            pl.BlockSpec((sbp, 1, RW_SHIFT_W - PASS0_SHIFT), second_pass),
        ),
        scratch_shapes=[pltpu.VMEM((sbp, 2 * SUBLANES, DN_QKV + RW_SHIFT_W), F32)],
        compiler_params=pltpu.CompilerParams(
            dimension_semantics=("arbitrary", "arbitrary"), vmem_limit_bytes=VMEM_LIMIT),
        name="proj",
    )(x2d, g_norm, w_main_t, w_ab_t, a_log, dt_bias, conv_w, mu, conv_buf8, prev8)
    conv_new = jnp.concatenate([cout0, cout1], axis=-1)
    shift_new = jnp.concatenate([shift0, shift1], axis=-1)
    return p, gb, conv_new, shift_new


def _dn_kernel(q_ref, k_ref, v_ref, z_ref, gb_ref, nw_ref, s0_ref, o_ref, sout_ref, s_scr, *, sb, tc, nc, nt):
    t = pl.program_id(1)
    ru = sb * tc
    hw = DN_HEADS * DN_D

    @pl.when(t == 0)
    def _():
        s_scr[...] = s0_ref[...]

    ri, ci = _iota2(ru, ru)
    same = _group(ri, tc) == _group(ci, tc)
    incl = same & (ri >= ci)
    eye = ri == ci
    rip, cc, same_p, first_cols, pdot = _column_packing(ru, tc)
    incl_p = same_p & (rip >= cc)
    strict_p = same_p & (rip > cc)
    seqs = [slice(s * tc, (s + 1) * tc) for s in range(sb)]
    units = [slice(c * ru, (c + 1) * ru) for c in range(nc)]

    def side_by_side(x0, x1):
        return jnp.concatenate([x0, x1], axis=1)

    def block_diag(x0, x1):
        z = jnp.zeros_like(x0)
        return jnp.concatenate([side_by_side(x0, z), side_by_side(z, x1)], axis=0)

    def unit_chain(heads, rows, q, k, v):
        per = []
        for h, kh in zip(heads, k):
            g_row = gb_ref[0, h:h + 1, :][:, rows]
            b_row = gb_ref[0, DN_HEADS + h:DN_HEADS + h + 1, :][:, rows]
            g_col = jnp.sum(jnp.where(incl, g_row, 0.0), axis=1, keepdims=True)
            g_cum_row = jnp.sum(jnp.where(eye, g_col, 0.0), axis=0, keepdims=True)
            g_tot = jnp.sum(jnp.where(same, g_row, 0.0), axis=1, keepdims=True)
            b_col = jnp.sum(jnp.where(eye, b_row, 0.0), axis=1, keepdims=True)
            per.append(dict(g_col=g_col, g_cum_row=g_cum_row, g_tot=g_tot, b_col=b_col,
                            e_g=jnp.exp(g_col), e_rest=jnp.exp(g_tot - g_col), kb=kh * b_col))
        g_cols = jnp.where(first_cols, per[0]["g_col"], per[1]["g_col"])
        g_rows = side_by_side(per[0]["g_cum_row"], per[1]["g_cum_row"])
        decay = jnp.where(incl_p, jnp.exp(jnp.minimum(g_cols - g_rows, 0.0)), 0.0)
        a_raw, qk_raw = _dot_rows([side_by_side(per[0]["kb"], per[1]["kb"]), side_by_side(q[0], q[1])],
                                  block_diag(k[0], k[1]), dot=_dot_nt)
        yield
        a = jnp.where(strict_p, a_raw * decay, 0.0)
        aqk = qk_raw * decay
        t_inv = yield from _unit_lower_inverse(a, rip, cc, tc, dot=pdot)
        rhs = [side_by_side(vh * w["b_col"], w["kb"] * w["e_g"]) for vh, w in zip(v, per)]
        sol = _dot(t_inv, block_diag(rhs[0], rhs[1]))
        yield
        out = []
        for i, w in enumerate(per):
            base = 2 * DN_D * i
            out.append(dict(u_all=sol[:, base:base + DN_D], w_all=sol[:, base + DN_D:base + 2 * DN_D],
                            qg=q[i] * w["e_g"], kd=k[i] * w["e_rest"], g_tot=w["g_tot"]))
        return aqk, out

    def pair_chain(pair):
        heads = (2 * pair, 2 * pair + 1)
        cols = [slice(h * DN_D, (h + 1) * DN_D) for h in heads]
        q = [q_ref[:, c] for c in cols]
        k = [k_ref[:, c] for c in cols]
        v = [v_ref[:, c] for c in cols]
        wy = yield from _lockstep(
            unit_chain(heads, rows, [x[rows] for x in q], [x[rows] for x in k], [x[rows] for x in v])
            for rows in units)
        for rows, (aqk, per) in zip(units, wy):
            ws = [[_dot(jnp.concatenate([w["w_all"][sq], w["qg"][sq]], axis=0), s_scr[s, h])
                   for s, sq in enumerate(seqs)] for h, w in zip(heads, per)]
            yield
            u = [_rows([w["u_all"][sq] - x[:tc] for sq, x in zip(seqs, wsh)]) for w, wsh in zip(per, ws)]
            q_s = [_rows([x[tc:] for x in wsh]) for wsh in ws]
            intra = _dot(aqk, block_diag(u[0], u[1]))
            upd = [[_dot_tn(w["kd"][sq], uh[sq]) for sq in seqs] for w, uh in zip(per, u)]
            yield
            for i, (h, w) in enumerate(zip(heads, per)):
                for s in range(sb):
                    gl = jnp.exp(w["g_tot"][s * tc:s * tc + 1, :])
                    s_scr[s, h] = s_scr[s, h] * gl + upd[i][s]
                o = q_s[i] + intra[:, i * DN_D:(i + 1) * DN_D]
                o = o * lax.rsqrt(jnp.mean(o * o, axis=-1, keepdims=True) + NORM_EPS) * nw_ref[...]
                o_ref[rows, cols[i]] = o * _silu(z_ref[rows, cols[i]])

    _run(_lockstep(pair_chain(pair) for pair in range(DN_HEADS // 2)))

    @pl.when(t == nt - 1)
    def _():
        sout_ref[...] = s_scr[...]


def _dn(p, gb, norm_w, s0, *, nb, seq, sb, tc, nc):
    nt = seq // (nc * tc)
    r = sb * nc * tc
    assert seq % (nc * tc) == 0 and nb % sb == 0 and (nt == 1 or sb == 1) and (nc == 1 or sb == 1)
    bcast = s0.shape[0] == 1
    assert not bcast or sb == 1
    hw = DN_HEADS * DN_D

    def st_idx(i):
        return 0 if bcast else i

    def rows(name):
        return pl.BlockSpec((r, hw), lambda i, t: (i * nt + t, COL[name] // hw))

    return pl.pallas_call(
        functools.partial(_dn_kernel, sb=sb, tc=tc, nc=nc, nt=nt),
        out_shape=(jax.ShapeDtypeStruct((nb * seq, hw), F32),
                   jax.ShapeDtypeStruct((nb, DN_HEADS, DN_D, DN_D), F32)),
        grid=(nb // sb, nt),
        in_specs=[
            rows("q"), rows("k"), rows("v"), rows("z"),
            pl.BlockSpec((1, 2 * DN_HEADS, r), lambda i, t: (i * nt + t, 0, 0)),
            pl.BlockSpec((1, DN_D), lambda i, t: (0, 0)),
            pl.BlockSpec((sb, DN_HEADS, DN_D, DN_D), lambda i, t: (st_idx(i), 0, 0, 0)),
        ],
        out_specs=(
            pl.BlockSpec((r, hw), lambda i, t: (i * nt + t, 0)),
            pl.BlockSpec((sb, DN_HEADS, DN_D, DN_D), lambda i, t: (i, 0, 0, 0)),
        ),
        scratch_shapes=[pltpu.VMEM((sb, DN_HEADS, DN_D, DN_D), F32)],
        compiler_params=pltpu.CompilerParams(
            dimension_semantics=("arbitrary", "arbitrary"), vmem_limit_bytes=VMEM_LIMIT),
        name="dn",
    )(p, p, p, p, gb, norm_w, s0)


def _rw_kernel(r_ref, k_ref, v_ref, l_ref, w0_ref, a0_ref, kkw_ref, kaw_ref, rkw_ref,
               gnw_ref, gnb_ref, w2_ref, a2_ref, g2_ref, s0_ref, odn_ref, ga_ref, gb_ref,
               o_ref, sout_ref, s_scr, *, sb, tc, nc, nt):
    t = pl.program_id(1)
    ru = sb * tc
    half = lax.broadcasted_iota(jnp.int32, (1, LANES), 1) < RW_HD
    ri128, ci128 = _iota2(LANES, LANES)
    bd_mask = _group(ri128, RW_HD) == _group(ci128, RW_HD)

    def seg_sum(x):
        first = jnp.sum(jnp.where(half, x, 0.0), axis=-1, keepdims=True)
        second = jnp.sum(jnp.where(half, 0.0, x), axis=-1, keepdims=True)
        return jnp.where(half, first, second)

    @pl.when(t == 0)
    def _():
        zero = jnp.zeros((RW_HD, RW_HD), F32)
        for s in range(sb):
            for p in range(RW_PAIRS):
                top = jnp.concatenate([s0_ref[s, 2 * p], zero], axis=1)
                bot = jnp.concatenate([zero, s0_ref[s, 2 * p + 1]], axis=1)
                s_scr[s, p] = jnp.concatenate([top, bot], axis=0)

    xm_l = l_ref[...]
    lora_w = jnp.tanh(xm_l).astype(BF16)
    lora_a = xm_l.astype(BF16)
    lora_g = _sigmoid(xm_l).astype(BF16)

    ri, ci = _iota2(ru, ru)
    cum_mask = jnp.where((_group(ri, tc) == _group(ci, tc)) & (ri >= ci), 1.0, 0.0).astype(BF16)
    rip, cc, same_p, _, pdot = _column_packing(ru, tc)
    incl_p = same_p & (rip >= cc)
    strict_p = same_p & (rip > cc)
    seqs = [slice(s * tc, (s + 1) * tc) for s in range(sb)]
    units = [slice(c * ru, (c + 1) * ru) for c in range(nc)]

    def stack(x):
        return jnp.concatenate([jnp.where(half, x, 0.0), jnp.where(half, 0.0, x)], axis=0)

    def unit_chain(rr, k2, vv, kk, bb, logw):
        lcum = _dot_mask(cum_mask, logw)
        yield
        ltot = jnp.broadcast_to(lcum.reshape(sb, tc, LANES)[:, tc - 1:tc, :],
                                (sb, tc, LANES)).reshape(ru, LANES)
        p_in = jnp.exp(lcum)
        p_inv = jnp.exp(-lcum)
        alpha = kk * jnp.exp(lcum - logw)
        beta = bb * p_inv
        kt = k2 * p_inv
        rt = rr * p_in
        e_rest = jnp.exp(ltot - lcum)
        beta2, kt2, v2 = stack(beta), stack(kt), stack(vv)
        l_raw, rb_raw = _dot_rows([alpha, rt], beta2, dot=_dot_nt)
        lk_raw, rk_raw = _dot_rows([alpha, rt], kt2, dot=_dot_nt)
        yield
        lmat = jnp.where(strict_p, l_raw, 0.0)
        rb = jnp.where(incl_p, rb_raw, 0.0)
        lk = jnp.where(strict_p, lk_raw, 0.0)
        rk = jnp.where(incl_p, rk_raw, 0.0)
        lkv, rkv = _dot_rows([lk, rk], v2)
        t_inv = yield from _unit_lower_inverse(lmat, rip, cc, tc, dot=pdot)
        sol = _dot(t_inv, jnp.concatenate([stack(alpha), stack(lkv)], axis=1))
        yield
        return dict(wa=sol[:, :LANES], uv=sol[:, LANES:], rt=rt, rb=rb, rkv=rkv, p_in=p_in,
                    k_end=k2 * e_rest, b_end=bb * e_rest)

    def pair_chain(p):
        c = slice(p * LANES, (p + 1) * LANES)
        rr = r_ref[:, c]
        xk = k_ref[:, c]
        vv = v_ref[:, c]
        lw = jnp.dot(lora_w, w2_ref[:, c], preferred_element_type=F32)
        la = jnp.dot(lora_a, a2_ref[:, c], preferred_element_type=F32)
        gate = jnp.dot(lora_g, g2_ref[:, c], preferred_element_type=F32)
        yield
        w_log = -_softplus(-(w0_ref[:, c] + lw)) - 0.5
        logw = -jnp.exp(w_log)
        aa = _sigmoid(a0_ref[:, c] + la)
        kk = xk * kkw_ref[:, c]
        k2 = xk * (1.0 + (aa - 1.0) * kaw_ref[:, c])
        kk = kk * lax.rsqrt(seg_sum(kk * kk) + 1e-6)
        bb = kk * aa
        wy = yield from _lockstep(
            unit_chain(rr[rows], k2[rows], vv[rows], kk[rows], bb[rows], logw[rows]) for rows in units)
        for rows, w in zip(units, wy):
            st = [_dot_nt(jnp.concatenate([w["wa"][sq], w["rt"][sq]], axis=0), s_scr[s, p])
                  for s, sq in enumerate(seqs)]
            yield
            u = w["uv"] + _rows([x[:tc] for x in st])
            r_state = _rows([x[tc:] for x in st])
            rbu = _dot(w["rb"], stack(u))
            v_u = vv[rows]
            upd = [_dot_tn(jnp.concatenate([v_u[sq], -u[sq]], axis=0),
                           jnp.concatenate([w["k_end"][sq], w["b_end"][sq]], axis=0)) for sq in seqs]
            yield
            for s in range(sb):
                p_end = w["p_in"][(s + 1) * tc - 1:(s + 1) * tc, :]
                s_scr[s, p] = s_scr[s, p] * p_end + jnp.where(bd_mask, upd[s], 0.0)
            o = r_state + (w["rkv"] - rbu)
            mean = seg_sum(o) * (1.0 / RW_HD)
            d = o - mean
            var = seg_sum(d * d) * (1.0 / RW_HD)
            bonus = seg_sum(rr[rows] * k2[rows] * rkw_ref[:, c])
            on = d * lax.rsqrt(var + RW_GN_EPS) * gnw_ref[:, c] + gnb_ref[:, c]
            o_rw = (on + bonus * v_u) * gate[rows]
            mix = _sigmoid(ga_ref[rows, c]) * odn_ref[rows, c] + _sigmoid(gb_ref[rows, c]) * o_rw
            o_ref[rows, c] = mix.astype(BF16)

    _run(_lockstep(pair_chain(p) for p in range(RW_PAIRS)))

    @pl.when(t == nt - 1)
    def _():
        for s in range(sb):
            for p in range(RW_PAIRS):
                sp = s_scr[s, p]
                sout_ref[s, 2 * p] = sp[0:RW_HD, 0:RW_HD]
                sout_ref[s, 2 * p + 1] = sp[RW_HD:, RW_HD:]


def _rw(p, o_dn, weights, s0, *, nb, seq, sb, tc, nc):
    nt = seq // (nc * tc)
    r = sb * nc * tc
    assert seq % (nc * tc) == 0 and nb % sb == 0 and (nt == 1 or sb == 1) and (nc == 1 or sb == 1)
    bcast = s0.shape[0] == 1
    assert not bcast or sb == 1
    (w0, a0, kkw, kaw, rkw, gnw, gnb, w2p, a2p, g2p) = weights

    def st_idx(i):
        return 0 if bcast else i

    def vec(width):
        return pl.BlockSpec((1, width), lambda i, t: (0, 0))

    def lora_w():
        return pl.BlockSpec((RW_LORA, RW_W), lambda i, t: (0, 0))

    def rows(col):
        return pl.BlockSpec((r, RW_W), lambda i, t: (i * nt + t, col))

    def named(name):
        return rows(COL[name] // RW_W)

    return pl.pallas_call(
        functools.partial(_rw_kernel, sb=sb, tc=tc, nc=nc, nt=nt),
        out_shape=(jax.ShapeDtypeStruct((nb * seq, RW_W), BF16),
                   jax.ShapeDtypeStruct((nb, RW_HEADS, RW_HD, RW_HD), F32)),
        grid=(nb // sb, nt),
        in_specs=[
            named("r"), named("kx"), named("vx"),
            pl.BlockSpec((r, RW_LORA), lambda i, t: (i * nt + t, COL["lora"] // RW_LORA)),
            vec(RW_W), vec(RW_W), vec(RW_W), vec(RW_W), vec(RW_W), vec(RW_W), vec(RW_W),
            lora_w(), lora_w(), lora_w(),
            pl.BlockSpec((sb, RW_HEADS, RW_HD, RW_HD), lambda i, t: (st_idx(i), 0, 0, 0)),
            rows(0), named("ga"), named("gb"),
        ],
        out_specs=(
            rows(0),
            pl.BlockSpec((sb, RW_HEADS, RW_HD, RW_HD), lambda i, t: (i, 0, 0, 0)),
        ),
        scratch_shapes=[pltpu.VMEM((sb, RW_PAIRS, LANES, LANES), F32)],
        compiler_params=pltpu.CompilerParams(
            dimension_semantics=("arbitrary", "arbitrary"), vmem_limit_bytes=VMEM_LIMIT),
        name="rw",
    )(p, p, p, p, w0, a0, kkw, kaw, rkw, gnw, gnb, w2p, a2p, g2p, s0, o_dn, p, p)


def _merge_kernel(x_ref, mix_ref, wout_ref, gffn_ref, w1_ref, w2_ref, gfin_ref,
                  y_ref, x1_scr, h2_scr, acc_scr, *, nf):
    f = pl.program_id(1)

    @pl.when(f == 0)
    def _():
        x1 = x_ref[...] + jnp.dot(mix_ref[...], wout_ref[...], preferred_element_type=F32)
        x1_scr[...] = x1
        h2 = x1 * lax.rsqrt(jnp.mean(x1 * x1, axis=-1, keepdims=True) + NORM_EPS) * gffn_ref[...]
        h2_scr[...] = h2.astype(BF16)
        acc_scr[...] = jnp.zeros_like(acc_scr)

    hid = jnp.dot(h2_scr[...], w1_ref[...], preferred_element_type=F32)
    hid = jnp.square(jnp.maximum(hid, 0.0))
    acc_scr[...] += jnp.dot(hid.astype(BF16), w2_ref[...], preferred_element_type=F32)

    @pl.when(f == nf - 1)
    def _():
        y = x1_scr[...] + acc_scr[...]
        y_ref[...] = y * lax.rsqrt(jnp.mean(y * y, axis=-1, keepdims=True) + NORM_EPS) * gfin_ref[...]


def _merge(x2d, mix, w_out, g_ffn, w1, w2, g_fin, *, tm, tf):
    n = x2d.shape[0]
    nf = D_FF // tf
    assert n % tm == 0 and D_FF % tf == 0
    row = lambda i, f: (i, 0)
    return pl.pallas_call(
        functools.partial(_merge_kernel, nf=nf),
        out_shape=jax.ShapeDtypeStruct((n, D_MODEL), F32),
        grid=(n // tm, nf),
        in_specs=[
            pl.BlockSpec((tm, D_MODEL), row),
            pl.BlockSpec((tm, D_MODEL), row),
            pl.BlockSpec((D_MODEL, D_MODEL), lambda i, f: (0, 0)),
            pl.BlockSpec((1, D_MODEL), lambda i, f: (0, 0)),
            pl.BlockSpec((D_MODEL, tf), lambda i, f: (0, f)),
            pl.BlockSpec((tf, D_MODEL), lambda i, f: (f, 0)),
            pl.BlockSpec((1, D_MODEL), lambda i, f: (0, 0)),
        ],
        out_specs=pl.BlockSpec((tm, D_MODEL), row),
        scratch_shapes=[
            pltpu.VMEM((tm, D_MODEL), F32),
            pltpu.VMEM((tm, D_MODEL), BF16),
            pltpu.VMEM((tm, D_MODEL), F32),
        ],
        compiler_params=pltpu.CompilerParams(
            dimension_semantics=("arbitrary", "arbitrary"), vmem_limit_bytes=VMEM_LIMIT),
        name="merge",
    )(x2d, mix, w_out, g_ffn, w1, w2, g_fin)


def _pad_rows_front(x, rows):
    b, n, w = x.shape
    return jnp.concatenate([jnp.zeros((b, rows - n, w), x.dtype), x], axis=1)


def _layer(x, conv_buf, dn_s, rw_prev, rw_s, wts, *, sb, tc, nc, tm, tm_mlp):
    nb, seq, _ = x.shape
    n = nb * seq
    x2d = x.reshape(n, D_MODEL)
    p, gb, conv_new, shift_new = _proj(
        x2d, wts["g_mix"], wts["w_main_t"], wts["w_ab_t"], wts["a_log"], wts["dt_bias"], wts["conv_w"],
        wts["rw_mu"], _pad_rows_front(conv_buf, SUBLANES), _pad_rows_front(rw_prev[:, None, :], SUBLANES),
        nb=nb, seq=seq, tm=tm, rc=sb * nc * tc)
    o_dn, dn_new = _dn(p, gb, wts["dn_norm_w"], dn_s, nb=nb, seq=seq, sb=sb, tc=tc, nc=nc)
    mix, rw_new = _rw(p, o_dn, wts["rw"], rw_s, nb=nb, seq=seq, sb=sb, tc=tc, nc=nc)
    y = _merge(x2d, mix, wts["w_out"], wts["g_ffn"], wts["w_ff1"], wts["w_ff2"], wts["g_final"],
               tm=tm_mlp, tf=1024)
    return y.reshape(nb, seq, D_MODEL), conv_new, dn_new, shift_new[:, 0], rw_new


def kernel(x_prompt, x_sample, state_dn_conv, state_dn, state_rw_shift, state_rw, meta_tokens, g_mix_norm, w_in, dn_conv_w, dn_a_log, dn_dt_bias, dn_norm_w, rw_mu, rw_w0, rw_w2, rw_a0, rw_a2, rw_g2, rw_k_k, rw_k_a, rw_r_k, rw_gn_w, rw_gn_b, w_out, g_ffn_norm, w_ff1, w_ff2, g_final):
    assert g_mix_norm.shape[0] == 1, "single layer"
    w_t = w_in[0].T
    o_a = DN_QKV + D_MODEL
    o_rwp = o_a + 2 * DN_HEADS
    o_gate = o_rwp + RW_SHIFT_W
    src = {"q": 0, "k": D_MODEL, "v": 2 * D_MODEL, "z": DN_QKV, "r": o_rwp, "kx": o_rwp + RW_W,
           "vx": o_rwp + 2 * RW_W, "lora": o_rwp + 3 * RW_W, "ga": o_gate, "gb": o_gate + D_MODEL}
    w_main_t = jnp.concatenate([w_t[src[name]:src[name] + width] for name, width, _, _ in SEGMENTS],
                               axis=0).astype(BF16)
    w_ab_t = w_t[o_a:o_rwp].astype(BF16)

    def lora_pad(wl, lo):
        return jnp.zeros((RW_LORA, RW_W), F32).at[lo:lo + wl.shape[0]].set(wl).astype(BF16)

    row = lambda v: v.reshape(1, -1).astype(F32)
    wts = {
        "g_mix": row(g_mix_norm[0]), "w_main_t": w_main_t, "w_ab_t": w_ab_t,
        "a_log": dn_a_log[0].reshape(DN_HEADS, 1), "dt_bias": dn_dt_bias[0].reshape(DN_HEADS, 1),
        "conv_w": dn_conv_w[0], "dn_norm_w": row(dn_norm_w[0]), "rw_mu": row(rw_mu[0]),
        "rw": (row(rw_w0[0]), row(rw_a0[0]), row(rw_k_k[0]), row(rw_k_a[0]), row(rw_r_k[0]),
               row(rw_gn_w[0]), row(rw_gn_b[0]),
               lora_pad(rw_w2[0], 0), lora_pad(rw_a2[0], 64), lora_pad(rw_g2[0], 128)),
        "w_out": w_out[0].astype(BF16), "g_ffn": row(g_ffn_norm[0]),
        "w_ff1": w_ff1[0].astype(BF16), "w_ff2": w_ff2[0].astype(BF16), "g_final": row(g_final),
    }

    nbm = 64 // N_META
    xm = jnp.broadcast_to(meta_tokens.astype(F32)[None], (nbm, N_META, D_MODEL))
    _, conv_m, dn_m, shift_m, rw_m = _layer(
        xm, jnp.zeros((nbm, 3, DN_QKV), F32), jnp.zeros((nbm, DN_HEADS, DN_D, DN_D), F32),
        jnp.zeros((nbm, RW_SHIFT_W), F32), jnp.zeros((nbm, RW_HEADS, RW_HD, RW_HD), F32), wts,
        sb=nbm, tc=N_META, nc=1, tm=nbm * N_META, tm_mlp=nbm * N_META)

    y_p, conv_p, dn_p, shift_p, rw_p = _layer(
        x_prompt, conv_m[:1], dn_m[:1], shift_m[:1], rw_m[:1], wts, sb=1, tc=64, nc=4, tm=512, tm_mlp=1024)

    dec_len = x_sample.shape[1]
    y_s, conv_s, dn_s, shift_s, rw_s = _layer(
        x_sample, state_dn_conv[0], state_dn[0], state_rw_shift[0], state_rw[0], wts,
        sb=SUBLANES, tc=dec_len, nc=1, tm=128, tm_mlp=1024)

    return (y_p, y_s, conv_p[None], dn_p[None], shift_p[None], rw_p[None],
            conv_s[None], dn_s[None], shift_s[None], rw_s[None])
```

```python
import functools
import math

import jax
import jax.numpy as jnp
from jax import lax
from jax.experimental import pallas as pl
from jax.experimental.pallas import tpu as pltpu

F32 = jnp.float32
BF16 = jnp.bfloat16

D_MODEL = 1024
N_META = 16
DN_HEADS = 8
DN_D = 128
DN_QKV = 3 * DN_HEADS * DN_D
RW_HEADS = 16
RW_HD = 64
RW_PAIRS = RW_HEADS // 2
RW_W = RW_HEADS * RW_HD
RW_LORA = 256
RW_SHIFT_W = 3 * RW_W + RW_LORA
D_FF = 4 * D_MODEL
NORM_EPS = 1e-6
RW_GN_EPS = 64e-5
LANES = 128
SUBLANES = 8
INV_BLOCK = 16

SEGMENTS = (
    ("q", D_MODEL, "conv", 0), ("k", D_MODEL, "conv", D_MODEL),
    ("r", RW_W, "shift", 0), ("kx", RW_W, "shift", RW_W),
    ("ga", D_MODEL, "raw", 0), ("gb", D_MODEL, "raw", 0),
    ("v", D_MODEL, "conv", 2 * D_MODEL), ("z", D_MODEL, "raw", 0),
    ("vx", RW_W, "shift", 2 * RW_W), ("lora", RW_LORA, "shift", 3 * RW_W),
)
COL = {}
P_W = 0
for _name, _width, _, _ in SEGMENTS:
    COL[_name] = P_W
    P_W += _width
P_HALF = P_W // 2
PASS0_CONV = 2 * D_MODEL
PASS0_SHIFT = 2 * RW_W


def _chunk_plan(half):
    lo, hi = half * P_HALF, (half + 1) * P_HALF
    plan = []
    for name, width, kind, off in SEGMENTS:
        start = COL[name]
        c = max(start, lo)
        while c < min(start + width, hi):
            w = min(512, min(start + width, hi) - c)
            plan.append((c - lo, w, kind, off + (c - start), name))
            c += w
    return plan

VMEM_LIMIT = 56 * 1024 * 1024


def _sigmoid(x):
    return 1.0 / (1.0 + jnp.exp(-x))


def _silu(x):
    return x * _sigmoid(x)


def _softplus(x):
    return jnp.maximum(x, 0.0) + jnp.log(1.0 + jnp.exp(-jnp.abs(x)))


def _dot(a, b):
    return jnp.dot(a.astype(BF16), b.astype(BF16), preferred_element_type=F32)


def _dot_nt(a, b):
    return lax.dot_general(a.astype(BF16), b.astype(BF16), (((1,), (1,)), ((), ())),
                           preferred_element_type=F32)


def _dot_tn(a, b):
    return lax.dot_general(a.astype(BF16), b.astype(BF16), (((0,), (0,)), ((), ())),
                           preferred_element_type=F32)


def _rows(xs):
    return xs[0] if len(xs) == 1 else jnp.concatenate(xs, axis=0)


def _dot_rows(xs, b, dot=_dot):
    out = dot(_rows(xs), b)
    off, parts = 0, []
    for x in xs:
        parts.append(out[off:off + x.shape[0]])
        off += x.shape[0]
    return parts


def _dot_mask(mask_bf16, x):
    h1 = x.astype(BF16)
    r1 = x - h1.astype(F32)
    h2 = r1.astype(BF16)
    h3 = (r1 - h2.astype(F32)).astype(BF16)
    return (jnp.dot(mask_bf16, h1, preferred_element_type=F32)
            + (jnp.dot(mask_bf16, h2, preferred_element_type=F32)
               + jnp.dot(mask_bf16, h3, preferred_element_type=F32)))


def _iota2(n, m):
    return (lax.broadcasted_iota(jnp.int32, (n, m), 0),
            lax.broadcasted_iota(jnp.int32, (n, m), 1))


def _group(idx, size):
    return lax.shift_right_logical(idx, int(math.log2(size)))


def _column_packing(ru, tc):
    ri, ci = _iota2(ru, 2 * ru)
    cc = ci & (ru - 1)
    same = _group(ri, tc) == _group(cc, tc)
    first = lax.broadcasted_iota(jnp.int32, (1, 2 * ru), 1) < ru

    def pdot(x, y):
        y_bd = jnp.concatenate([jnp.where(first, y, 0.0), jnp.where(first, 0.0, y)], axis=0)
        return _dot(x, y_bd)

    return ri, cc, same, first, pdot


def _lockstep(chains):
    chains = list(chains)
    results = [None] * len(chains)
    active = list(range(len(chains)))
    while active:
        for i in list(active):
            try:
                next(chains[i])
            except StopIteration as done:
                results[i] = done.value
                active.remove(i)
        if active:
            yield
    return results


def _run(chain):
    for _ in chain:
        pass


def _neumann_inverse(a, eye, nil, dot):
    inv = eye - a
    if nil <= 2:
        return inv
    power = dot(a, a)
    yield
    k = 4
    while k < nil:
        step, power = _dot_rows([inv, power], power, dot=dot)
        yield
        inv = inv + step
        k *= 2
    step = dot(inv, power)
    yield
    return inv + step


def _unit_lower_inverse(a, ri, ci, tc, dot=_dot):
    eye = jnp.where(ri == ci, 1.0, 0.0).astype(F32)
    if tc <= INV_BLOCK:
        return (yield from _neumann_inverse(a, eye, tc, dot))
    assert tc // INV_BLOCK <= 4
    diag = _group(ri, INV_BLOCK) == _group(ci, INV_BLOCK)
    d = jnp.where(diag, a, 0.0)
    low = a - d
    dinv = yield from _neumann_inverse(d, eye, INV_BLOCK, dot)
    b = dot(low, dinv)
    yield
    db, bb = _dot_rows([dinv, b], b, dot=dot)
    yield
    p1 = dinv - db
    tail = dot(p1, bb)
    yield
    return p1 + tail


def _proj_kernel(x_ref, g_ref, wt_ref, wab_ref, alog_ref, dtb_ref, cw_ref, mu_ref, cb_ref, prev_ref,
                 p_ref, gb_ref, cout0_ref, shift0_ref, cout1_ref, shift1_ref, *maybe_hist,
                 tm, rc, sbp, ttp, ntp):
    carry = ntp > 1
    hist = maybe_hist[0] if carry else None
    j = pl.program_id(0)
    i = pl.program_id(1)
    tpos = lax.rem(i, ntp)
    x = x_ref[...]
    h = x * lax.rsqrt(jnp.mean(x * x, axis=-1, keepdims=True) + NORM_EPS) * g_ref[...]
    hb = h.astype(BF16)
    nt_dims = (((1,), (1,)), ((), ()))

    def mm(c0, cw):
        return lax.dot_general(hb, wt_ref[c0:c0 + cw, :], nt_dims, preferred_element_type=F32)

    seq_pos = lax.broadcasted_iota(jnp.int32, (tm, 1), 0) & (ttp - 1)

    def shift_rows(a, first):
        rolled = pltpu.roll(a, len(first), axis=0)
        if sbp == 1:
            head = rolled[0:SUBLANES]
            for i, f in enumerate(first):
                head = jnp.where(seq_pos[0:SUBLANES] == i, f[0], head)
            return jnp.concatenate([head, rolled[SUBLANES:]], axis=0)
        for i, f in enumerate(first):
            fill = jnp.broadcast_to(f, (sbp, ttp, a.shape[1])).reshape(tm, a.shape[1])
            rolled = jnp.where(seq_pos == i, fill, rolled)
        return rolled

    def run_pass(plan):
        if carry:
            @pl.when(tpos == 0)
            def _():
                hist[:, 0:SUBLANES, 0:DN_QKV] = cb_ref[...]
                hist[:, 0:SUBLANES, DN_QKV:] = prev_ref[...]

            @pl.when(tpos > 0)
            def _():
                hist[:, 0:SUBLANES, :] = hist[:, SUBLANES:, :]

        for c0, cw, kind, off, name in plan:
            cur = mm(c0, cw)
            if kind == "raw":
                p_ref[:, c0:c0 + cw] = cur
                continue
            cur3 = cur.reshape(sbp, ttp, cw)
            if carry:
                hc = slice(off, off + cw) if kind == "conv" else slice(DN_QKV + off, DN_QKV + off + cw)
                old = hist[:, 0:SUBLANES, hc]
                hist[:, SUBLANES:, hc] = cur3[:, ttp - SUBLANES:, :]
            elif kind == "conv":
                old = cb_ref[:, :, off:off + cw]
                out_ref, base = (cout0_ref, 0) if off < PASS0_CONV else (cout1_ref, PASS0_CONV)
                out_ref[:, :, off - base:off - base + cw] = cur3[:, ttp - 3:, :]
            else:
                old = prev_ref[:, :, off:off + cw]
                out_ref, base = (shift0_ref, 0) if off < PASS0_SHIFT else (shift1_ref, PASS0_SHIFT)
                out_ref[:, :, off - base:off - base + cw] = cur3[:, ttp - 1:, :]
            x1 = old[:, 7:8, :]
            if kind == "shift":
                prev = shift_rows(cur, [x1])
                p_ref[:, c0:c0 + cw] = cur + (prev - cur) * mu_ref[:, off:off + cw]
                continue
            w = [cw_ref[tap:tap + 1, off:off + cw] for tap in range(4)]
            x2, x3 = old[:, 6:7, :], old[:, 5:6, :]
            y1 = shift_rows(cur, [x1])
            far = cur * w[1] + y1 * w[0]
            acc = cur * w[3] + y1 * w[2] + shift_rows(far, [x2 * w[1] + x3 * w[0], x1 * w[1] + x2 * w[0]])
            y = _silu(acc)
            if name == "v":
                p_ref[:, c0:c0 + cw] = y
            else:
                scale = DN_D ** -0.5 if name == "q" else 1.0
                for c1 in range(0, cw, DN_D):
                    yh = y[:, c1:c1 + DN_D]
                    yh = yh * (lax.rsqrt(jnp.sum(yh * yh, axis=-1, keepdims=True) + 1e-6) * scale)
                    p_ref[:, c0 + c1:c0 + c1 + DN_D] = yh

    @pl.when(j == 0)
    def _():
        run_pass(_chunk_plan(0))
        for c in range(tm // rc):
            ab = lax.dot_general(wab_ref[...], hb[c * rc:(c + 1) * rc], nt_dims,
                                 preferred_element_type=F32)
            g = -jnp.exp(alog_ref[...]) * _softplus(ab[0:DN_HEADS] + dtb_ref[...])
            gb_ref[c, 0:DN_HEADS, :] = g
            gb_ref[c, DN_HEADS:2 * DN_HEADS, :] = _sigmoid(ab[DN_HEADS:2 * DN_HEADS])

        if carry:
            @pl.when(tpos == ntp - 1)
            def _():
                cout0_ref[...] = hist[:, 2 * SUBLANES - 3:, 0:PASS0_CONV]
                shift0_ref[...] = hist[:, 2 * SUBLANES - 1:, DN_QKV:DN_QKV + PASS0_SHIFT]

    @pl.when(j == 1)
    def _():
        run_pass(_chunk_plan(1))

        if carry:
            @pl.when(tpos == ntp - 1)
            def _():
                cout1_ref[...] = hist[:, 2 * SUBLANES - 3:, PASS0_CONV:DN_QKV]
                shift1_ref[...] = hist[:, 2 * SUBLANES - 1:, DN_QKV + PASS0_SHIFT:]


def _proj(x2d, g_norm, w_main_t, w_ab_t, a_log, dt_bias, conv_w, mu, conv_buf8, prev8, *, nb, seq, tm, rc):
    n = nb * seq
    ttp = min(seq, tm)
    sbp = tm // ttp
    ntp = seq // ttp
    assert n % tm == 0 and tm % rc == 0 and tm % ttp == 0 and seq % ttp == 0
    bcast = conv_buf8.shape[0] == 1
    assert bcast == (prev8.shape[0] == 1) and (not bcast or sbp == 1)
    last = (n // tm - 1) // ntp

    def st_idx(i):
        return 0 if bcast else i // ntp

    assert ttp % SUBLANES == 0 and ttp & (ttp - 1) == 0
    first_pass = lambda j, i: ((i // ntp) * (1 - j) + last * j, 0, 0)
    second_pass = lambda j, i: ((i // ntp) * j, 0, 0)
    p, gb, cout0, shift0, cout1, shift1 = pl.pallas_call(
        functools.partial(_proj_kernel, tm=tm, rc=rc, sbp=sbp, ttp=ttp, ntp=ntp),
        out_shape=(jax.ShapeDtypeStruct((n, P_W), F32),
                   jax.ShapeDtypeStruct((n // rc, 2 * DN_HEADS, rc), F32),
                   jax.ShapeDtypeStruct((nb, 3, PASS0_CONV), F32),
                   jax.ShapeDtypeStruct((nb, 1, PASS0_SHIFT), F32),
                   jax.ShapeDtypeStruct((nb, 3, DN_QKV - PASS0_CONV), F32),
                   jax.ShapeDtypeStruct((nb, 1, RW_SHIFT_W - PASS0_SHIFT), F32)),
        grid=(2, n // tm),
        in_specs=[
            pl.BlockSpec((tm, D_MODEL), lambda j, i: (i, 0)),
            pl.BlockSpec((1, D_MODEL), lambda j, i: (0, 0)),
            pl.BlockSpec((P_HALF, D_MODEL), lambda j, i: (j, 0)),
            pl.BlockSpec((2 * DN_HEADS, D_MODEL), lambda j, i: (0, 0)),
            pl.BlockSpec((DN_HEADS, 1), lambda j, i: (0, 0)),
            pl.BlockSpec((DN_HEADS, 1), lambda j, i: (0, 0)),
            pl.BlockSpec((4, DN_QKV), lambda j, i: (0, 0)),
            pl.BlockSpec((1, RW_SHIFT_W), lambda j, i: (0, 0)),
            pl.BlockSpec((sbp, SUBLANES, DN_QKV), lambda j, i: (st_idx(i), 0, 0)),
            pl.BlockSpec((sbp, SUBLANES, RW_SHIFT_W), lambda j, i: (st_idx(i), 0, 0)),
        ],
        out_specs=(
            pl.BlockSpec((tm, P_HALF), lambda j, i: (i, j)),
            pl.BlockSpec((tm // rc, 2 * DN_HEADS, rc),
                         lambda j, i: (i * (1 - j) + (n // tm - 1) * j, 0, 0)),
            pl.BlockSpec((sbp, 3, PASS0_CONV), first_pass),
            pl.BlockSpec((sbp, 1, PASS0_SHIFT), first_pass),
            pl.BlockSpec((sbp, 3, DN_QKV - PASS0_CONV), second_pass),
            pl.BlockSpec((sbp, 1, RW_SHIFT_W - PASS0_SHIFT), second_pass),
        ),
        scratch_shapes=[pltpu.VMEM((sbp, 2 * SUBLANES, DN_QKV + RW_SHIFT_W), F32)] if ntp > 1 else [],
        compiler_params=pltpu.CompilerParams(
            dimension_semantics=("arbitrary", "arbitrary"), vmem_limit_bytes=VMEM_LIMIT),
        name="proj",
    )(x2d, g_norm, w_main_t, w_ab_t, a_log, dt_bias, conv_w, mu, conv_buf8, prev8)
    conv_new = jnp.concatenate([cout0, cout1], axis=-1)
    shift_new = jnp.concatenate([shift0, shift1], axis=-1)
    return p, gb, conv_new, shift_new


def _dn_kernel(q_ref, k_ref, v_ref, z_ref, gb_ref, nw_ref, s0_ref, o_ref, sout_ref, s_scr, *, sb, tc, nc, nt):
    t = pl.program_id(1)
    ru = sb * tc
    hw = DN_HEADS * DN_D

    @pl.when(t == 0)
    def _():
        s_scr[...] = s0_ref[...]

    ri, ci = _iota2(ru, ru)
    same = _group(ri, tc) == _group(ci, tc)
    incl = same & (ri >= ci)
    eye = ri == ci
    rip, cc, same_p, first_cols, pdot = _column_packing(ru, tc)
    incl_p = same_p & (rip >= cc)
    strict_p = same_p & (rip > cc)
    seqs = [slice(s * tc, (s + 1) * tc) for s in range(sb)]
    units = [slice(c * ru, (c + 1) * ru) for c in range(nc)]

    def side_by_side(x0, x1):
        return jnp.concatenate([x0, x1], axis=1)

    def block_diag(x0, x1):
        z = jnp.zeros_like(x0)
        return jnp.concatenate([side_by_side(x0, z), side_by_side(z, x1)], axis=0)

    def unit_chain(heads, rows, q, k, v):
        per = []
        for h, kh in zip(heads, k):
            g_row = gb_ref[0, h:h + 1, :][:, rows]
            b_row = gb_ref[0, DN_HEADS + h:DN_HEADS + h + 1, :][:, rows]
            g_col = jnp.sum(jnp.where(incl, g_row, 0.0), axis=1, keepdims=True)
            g_cum_row = jnp.sum(jnp.where(eye, g_col, 0.0), axis=0, keepdims=True)
            g_tot = jnp.sum(jnp.where(same, g_row, 0.0), axis=1, keepdims=True)
            b_col = jnp.sum(jnp.where(eye, b_row, 0.0), axis=1, keepdims=True)
            per.append(dict(g_col=g_col, g_cum_row=g_cum_row, g_tot=g_tot, b_col=b_col,
                            e_g=jnp.exp(g_col), e_rest=jnp.exp(g_tot - g_col), kb=kh * b_col))
        g_cols = jnp.where(first_cols, per[0]["g_col"], per[1]["g_col"])
        g_rows = side_by_side(per[0]["g_cum_row"], per[1]["g_cum_row"])
        decay = jnp.where(incl_p, jnp.exp(jnp.minimum(g_cols - g_rows, 0.0)), 0.0)
        a_raw, qk_raw = _dot_rows([side_by_side(per[0]["kb"], per[1]["kb"]), side_by_side(q[0], q[1])],
                                  block_diag(k[0], k[1]), dot=_dot_nt)
        yield
        a = jnp.where(strict_p, a_raw * decay, 0.0)
        aqk = qk_raw * decay
        t_inv = yield from _unit_lower_inverse(a, rip, cc, tc, dot=pdot)
        rhs = [side_by_side(vh * w["b_col"], w["kb"] * w["e_g"]) for vh, w in zip(v, per)]
        sol = _dot(t_inv, block_diag(rhs[0], rhs[1]))
        yield
        out = []
        for i, w in enumerate(per):
            base = 2 * DN_D * i
            out.append(dict(u_all=sol[:, base:base + DN_D], w_all=sol[:, base + DN_D:base + 2 * DN_D],
                            qg=q[i] * w["e_g"], kd=k[i] * w["e_rest"], g_tot=w["g_tot"]))
        return aqk, out

    def pair_chain(pair):
        heads = (2 * pair, 2 * pair + 1)
        cols = [slice(h * DN_D, (h + 1) * DN_D) for h in heads]
        q = [q_ref[:, c] for c in cols]
        k = [k_ref[:, c] for c in cols]
        v = [v_ref[:, c] for c in cols]
        wy = yield from _lockstep(
            unit_chain(heads, rows, [x[rows] for x in q], [x[rows] for x in k], [x[rows] for x in v])
            for rows in units)
        for rows, (aqk, per) in zip(units, wy):
            ws = [[_dot(jnp.concatenate([w["w_all"][sq], w["qg"][sq]], axis=0), s_scr[s, h])
                   for s, sq in enumerate(seqs)] for h, w in zip(heads, per)]
            yield
            u = [_rows([w["u_all"][sq] - x[:tc] for sq, x in zip(seqs, wsh)]) for w, wsh in zip(per, ws)]
            q_s = [_rows([x[tc:] for x in wsh]) for wsh in ws]
            intra = _dot(aqk, block_diag(u[0], u[1]))
            upd = [[_dot_tn(w["kd"][sq], uh[sq]) for sq in seqs] for w, uh in zip(per, u)]
            yield
            for i, (h, w) in enumerate(zip(heads, per)):
                for s in range(sb):
                    gl = jnp.exp(w["g_tot"][s * tc:s * tc + 1, :])
                    s_scr[s, h] = s_scr[s, h] * gl + upd[i][s]
                o = q_s[i] + intra[:, i * DN_D:(i + 1) * DN_D]
                o = o * lax.rsqrt(jnp.mean(o * o, axis=-1, keepdims=True) + NORM_EPS) * nw_ref[...]
                o_ref[rows, cols[i]] = o * _silu(z_ref[rows, cols[i]])

    _run(_lockstep(pair_chain(pair) for pair in range(DN_HEADS // 2)))

    @pl.when(t == nt - 1)
    def _():
        sout_ref[...] = s_scr[...]


def _dn(p, gb, norm_w, s0, *, nb, seq, sb, tc, nc):
    nt = seq // (nc * tc)
    r = sb * nc * tc
    assert seq % (nc * tc) == 0 and nb % sb == 0 and (nt == 1 or sb == 1) and (nc == 1 or sb == 1)
    bcast = s0.shape[0] == 1
    assert not bcast or sb == 1
    hw = DN_HEADS * DN_D

    def st_idx(i):
        return 0 if bcast else i

    def rows(name):
        return pl.BlockSpec((r, hw), lambda i, t: (i * nt + t, COL[name] // hw))

    return pl.pallas_call(
        functools.partial(_dn_kernel, sb=sb, tc=tc, nc=nc, nt=nt),
        out_shape=(jax.ShapeDtypeStruct((nb * seq, hw), F32),
                   jax.ShapeDtypeStruct((nb, DN_HEADS, DN_D, DN_D), F32)),
        grid=(nb // sb, nt),
        in_specs=[
            rows("q"), rows("k"), rows("v"), rows("z"),
            pl.BlockSpec((1, 2 * DN_HEADS, r), lambda i, t: (i * nt + t, 0, 0)),
            pl.BlockSpec((1, DN_D), lambda i, t: (0, 0)),
            pl.BlockSpec((sb, DN_HEADS, DN_D, DN_D), lambda i, t: (st_idx(i), 0, 0, 0)),
        ],
        out_specs=(
            pl.BlockSpec((r, hw), lambda i, t: (i * nt + t, 0)),
            pl.BlockSpec((sb, DN_HEADS, DN_D, DN_D), lambda i, t: (i, 0, 0, 0)),
        ),
        scratch_shapes=[pltpu.VMEM((sb, DN_HEADS, DN_D, DN_D), F32)],
        compiler_params=pltpu.CompilerParams(
            dimension_semantics=("arbitrary", "arbitrary"), vmem_limit_bytes=VMEM_LIMIT),
        name="dn",
    )(p, p, p, p, gb, norm_w, s0)


def _rw_kernel(r_ref, k_ref, v_ref, l_ref, w0_ref, a0_ref, kkw_ref, kaw_ref, rkw_ref,
               gnw_ref, gnb_ref, w2_ref, a2_ref, g2_ref, s0_ref, odn_ref, ga_ref, gb_ref,
               o_ref, sout_ref, s_scr, *, sb, tc, nc, nt):
    t = pl.program_id(1)
    ru = sb * tc
    half = lax.broadcasted_iota(jnp.int32, (1, LANES), 1) < RW_HD
    ri128, ci128 = _iota2(LANES, LANES)
    bd_mask = _group(ri128, RW_HD) == _group(ci128, RW_HD)

    def seg_sum(x):
        first = jnp.sum(jnp.where(half, x, 0.0), axis=-1, keepdims=True)
        second = jnp.sum(jnp.where(half, 0.0, x), axis=-1, keepdims=True)
        return jnp.where(half, first, second)

    @pl.when(t == 0)
    def _():
        zero = jnp.zeros((RW_HD, RW_HD), F32)
        for s in range(sb):
            for p in range(RW_PAIRS):
                top = jnp.concatenate([s0_ref[s, 2 * p], zero], axis=1)
                bot = jnp.concatenate([zero, s0_ref[s, 2 * p + 1]], axis=1)
                s_scr[s, p] = jnp.concatenate([top, bot], axis=0)

    xm_l = l_ref[...]
    lora_w = jnp.tanh(xm_l).astype(BF16)
    lora_a = xm_l.astype(BF16)
    lora_g = _sigmoid(xm_l).astype(BF16)

    ri, ci = _iota2(ru, ru)
    cum_mask = jnp.where((_group(ri, tc) == _group(ci, tc)) & (ri >= ci), 1.0, 0.0).astype(BF16)
    rip, cc, same_p, _, pdot = _column_packing(ru, tc)
    incl_p = same_p & (rip >= cc)
    strict_p = same_p & (rip > cc)
    seqs = [slice(s * tc, (s + 1) * tc) for s in range(sb)]
    units = [slice(c * ru, (c + 1) * ru) for c in range(nc)]

    def stack(x):
        return jnp.concatenate([jnp.where(half, x, 0.0), jnp.where(half, 0.0, x)], axis=0)

    def unit_chain(rr, k2, vv, kk, bb, logw):
        lcum = _dot_mask(cum_mask, logw)
        yield
        ltot = jnp.broadcast_to(lcum.reshape(sb, tc, LANES)[:, tc - 1:tc, :],
                                (sb, tc, LANES)).reshape(ru, LANES)
        p_in = jnp.exp(lcum)
        p_inv = jnp.exp(-lcum)
        alpha = kk * jnp.exp(lcum - logw)
        beta = bb * p_inv
        kt = k2 * p_inv
        rt = rr * p_in
        e_rest = jnp.exp(ltot - lcum)
        beta2, kt2, v2 = stack(beta), stack(kt), stack(vv)
        l_raw, rb_raw = _dot_rows([alpha, rt], beta2, dot=_dot_nt)
        lk_raw, rk_raw = _dot_rows([alpha, rt], kt2, dot=_dot_nt)
        yield
        lmat = jnp.where(strict_p, l_raw, 0.0)
        rb = jnp.where(incl_p, rb_raw, 0.0)
        lk = jnp.where(strict_p, lk_raw, 0.0)
        rk = jnp.where(incl_p, rk_raw, 0.0)
        lkv, rkv = _dot_rows([lk, rk], v2)
        t_inv = yield from _unit_lower_inverse(lmat, rip, cc, tc, dot=pdot)
        sol = _dot(t_inv, jnp.concatenate([stack(alpha), stack(lkv)], axis=1))
        yield
        return dict(wa=sol[:, :LANES], uv=sol[:, LANES:], rt=rt, rb=rb, rkv=rkv, p_in=p_in,
                    k_end=k2 * e_rest, b_end=bb * e_rest)

    def pair_chain(p):
        c = slice(p * LANES, (p + 1) * LANES)
        rr = r_ref[:, c]
        xk = k_ref[:, c]
        vv = v_ref[:, c]
        lw = jnp.dot(lora_w, w2_ref[:, c], preferred_element_type=F32)
        la = jnp.dot(lora_a, a2_ref[:, c], preferred_element_type=F32)
        gate = jnp.dot(lora_g, g2_ref[:, c], preferred_element_type=F32)
        yield
        logw = -math.exp(-0.5) * _sigmoid(w0_ref[:, c] + lw)
        aa = _sigmoid(a0_ref[:, c] + la)
        kk = xk * kkw_ref[:, c]
        k2 = xk * (1.0 + (aa - 1.0) * kaw_ref[:, c])
        kk = kk * lax.rsqrt(seg_sum(kk * kk) + 1e-6)
        bb = kk * aa
        wy = yield from _lockstep(
            unit_chain(rr[rows], k2[rows], vv[rows], kk[rows], bb[rows], logw[rows]) for rows in units)
        for rows, w in zip(units, wy):
            st = [_dot_nt(jnp.concatenate([w["wa"][sq], w["rt"][sq]], axis=0), s_scr[s, p])
                  for s, sq in enumerate(seqs)]
            yield
            u = w["uv"] + _rows([x[:tc] for x in st])
            r_state = _rows([x[tc:] for x in st])
            rbu = _dot(w["rb"], stack(u))
            v_u = vv[rows]
            upd = [_dot_tn(jnp.concatenate([v_u[sq], -u[sq]], axis=0),
                           jnp.concatenate([w["k_end"][sq], w["b_end"][sq]], axis=0)) for sq in seqs]
            yield
            for s in range(sb):
                p_end = w["p_in"][(s + 1) * tc - 1:(s + 1) * tc, :]
                s_scr[s, p] = s_scr[s, p] * p_end + jnp.where(bd_mask, upd[s], 0.0)
            o = r_state + (w["rkv"] - rbu)
            mean = seg_sum(o) * (1.0 / RW_HD)
            d = o - mean
            var = seg_sum(d * d) * (1.0 / RW_HD)
            bonus = seg_sum(rr[rows] * k2[rows] * rkw_ref[:, c])
            on = d * lax.rsqrt(var + RW_GN_EPS) * gnw_ref[:, c] + gnb_ref[:, c]
            o_rw = (on + bonus * v_u) * gate[rows]
            mix = _sigmoid(ga_ref[rows, c]) * odn_ref[rows, c] + _sigmoid(gb_ref[rows, c]) * o_rw
            o_ref[rows, c] = mix.astype(BF16)

    _run(_lockstep(pair_chain(p) for p in range(RW_PAIRS)))

    @pl.when(t == nt - 1)
    def _():
        for s in range(sb):
            for p in range(RW_PAIRS):
                sp = s_scr[s, p]
                sout_ref[s, 2 * p] = sp[0:RW_HD, 0:RW_HD]
                sout_ref[s, 2 * p + 1] = sp[RW_HD:, RW_HD:]


def _rw(p, o_dn, weights, s0, *, nb, seq, sb, tc, nc):
    nt = seq // (nc * tc)
    r = sb * nc * tc
    assert seq % (nc * tc) == 0 and nb % sb == 0 and (nt == 1 or sb == 1) and (nc == 1 or sb == 1)
    bcast = s0.shape[0] == 1
    assert not bcast or sb == 1
    (w0, a0, kkw, kaw, rkw, gnw, gnb, w2p, a2p, g2p) = weights

    def st_idx(i):
        return 0 if bcast else i

    def vec(width):
        return pl.BlockSpec((1, width), lambda i, t: (0, 0))

    def lora_w():
        return pl.BlockSpec((RW_LORA, RW_W), lambda i, t: (0, 0))

    def rows(col):
        return pl.BlockSpec((r, RW_W), lambda i, t: (i * nt + t, col))

    def named(name):
        return rows(COL[name] // RW_W)

    return pl.pallas_call(
        functools.partial(_rw_kernel, sb=sb, tc=tc, nc=nc, nt=nt),
        out_shape=(jax.ShapeDtypeStruct((nb * seq, RW_W), BF16),
                   jax.ShapeDtypeStruct((nb, RW_HEADS, RW_HD, RW_HD), F32)),
        grid=(nb // sb, nt),
        in_specs=[
            named("r"), named("kx"), named("vx"),
            pl.BlockSpec((r, RW_LORA), lambda i, t: (i * nt + t, COL["lora"] // RW_LORA)),
            vec(RW_W), vec(RW_W), vec(RW_W), vec(RW_W), vec(RW_W), vec(RW_W), vec(RW_W),
            lora_w(), lora_w(), lora_w(),
            pl.BlockSpec((sb, RW_HEADS, RW_HD, RW_HD), lambda i, t: (st_idx(i), 0, 0, 0)),
            rows(0), named("ga"), named("gb"),
        ],
        out_specs=(
            rows(0),
            pl.BlockSpec((sb, RW_HEADS, RW_HD, RW_HD), lambda i, t: (i, 0, 0, 0)),
        ),
        scratch_shapes=[pltpu.VMEM((sb, RW_PAIRS, LANES, LANES), F32)],
        compiler_params=pltpu.CompilerParams(
            dimension_semantics=("arbitrary", "arbitrary"), vmem_limit_bytes=VMEM_LIMIT),
        name="rw",
    )(p, p, p, p, w0, a0, kkw, kaw, rkw, gnw, gnb, w2p, a2p, g2p, s0, o_dn, p, p)


def _merge_kernel(x_ref, mix_ref, wout_ref, gffn_ref, w1_ref, w2_ref, gfin_ref,
                  y_ref, x1_scr, h2_scr, acc_scr, *, nf):
    f = pl.program_id(1)

    @pl.when(f == 0)
    def _():
        x1 = x_ref[...] + jnp.dot(mix_ref[...], wout_ref[...], preferred_element_type=F32)
        x1_scr[...] = x1
        h2 = x1 * lax.rsqrt(jnp.mean(x1 * x1, axis=-1, keepdims=True) + NORM_EPS) * gffn_ref[...]
        h2_scr[...] = h2.astype(BF16)
        acc_scr[...] = jnp.zeros_like(acc_scr)

    hid = jnp.dot(h2_scr[...], w1_ref[...], preferred_element_type=F32)
    hid = jnp.square(jnp.maximum(hid, 0.0))
    acc_scr[...] += jnp.dot(hid.astype(BF16), w2_ref[...], preferred_element_type=F32)

    @pl.when(f == nf - 1)
    def _():
        y = x1_scr[...] + acc_scr[...]
        y_ref[...] = y * lax.rsqrt(jnp.mean(y * y, axis=-1, keepdims=True) + NORM_EPS) * gfin_ref[...]


def _merge(x2d, mix, w_out, g_ffn, w1, w2, g_fin, *, tm, tf):
    n = x2d.shape[0]
    nf = D_FF // tf
    assert n % tm == 0 and D_FF % tf == 0
    row = lambda i, f: (i, 0)
    return pl.pallas_call(
        functools.partial(_merge_kernel, nf=nf),
        out_shape=jax.ShapeDtypeStruct((n, D_MODEL), F32),
        grid=(n // tm, nf),
        in_specs=[
            pl.BlockSpec((tm, D_MODEL), row),
            pl.BlockSpec((tm, D_MODEL), row),
            pl.BlockSpec((D_MODEL, D_MODEL), lambda i, f: (0, 0)),
            pl.BlockSpec((1, D_MODEL), lambda i, f: (0, 0)),
            pl.BlockSpec((D_MODEL, tf), lambda i, f: (0, f)),
            pl.BlockSpec((tf, D_MODEL), lambda i, f: (f, 0)),
            pl.BlockSpec((1, D_MODEL), lambda i, f: (0, 0)),
        ],
        out_specs=pl.BlockSpec((tm, D_MODEL), row),
        scratch_shapes=[
            pltpu.VMEM((tm, D_MODEL), F32),
            pltpu.VMEM((tm, D_MODEL), BF16),
            pltpu.VMEM((tm, D_MODEL), F32),
        ],
        compiler_params=pltpu.CompilerParams(
            dimension_semantics=("arbitrary", "arbitrary"), vmem_limit_bytes=VMEM_LIMIT),
        name="merge",
    )(x2d, mix, w_out, g_ffn, w1, w2, g_fin)


def _pad_rows_front(x, rows):
    b, n, w = x.shape
    return jnp.concatenate([jnp.zeros((b, rows - n, w), x.dtype), x], axis=1)


def _layer(x, conv_buf, dn_s, rw_prev, rw_s, wts, *, sb, tc, nc, tm, tm_mlp):
    nb, seq, _ = x.shape
    n = nb * seq
    x2d = x.reshape(n, D_MODEL)
    p, gb, conv_new, shift_new = _proj(
        x2d, wts["g_mix"], wts["w_main_t"], wts["w_ab_t"], wts["a_log"], wts["dt_bias"], wts["conv_w"],
        wts["rw_mu"], _pad_rows_front(conv_buf, SUBLANES), _pad_rows_front(rw_prev[:, None, :], SUBLANES),
        nb=nb, seq=seq, tm=tm, rc=sb * nc * tc)
    o_dn, dn_new = _dn(p, gb, wts["dn_norm_w"], dn_s, nb=nb, seq=seq, sb=sb, tc=tc, nc=nc)
    mix, rw_new = _rw(p, o_dn, wts["rw"], rw_s, nb=nb, seq=seq, sb=sb, tc=tc, nc=nc)
    y = _merge(x2d, mix, wts["w_out"], wts["g_ffn"], wts["w_ff1"], wts["w_ff2"], wts["g_final"],
               tm=tm_mlp, tf=1024)
    return y.reshape(nb, seq, D_MODEL), conv_new, dn_new, shift_new[:, 0], rw_new


def kernel(x_prompt, x_sample, state_dn_conv, state_dn, state_rw_shift, state_rw, meta_tokens, g_mix_norm, w_in, dn_conv_w, dn_a_log, dn_dt_bias, dn_norm_w, rw_mu, rw_w0, rw_w2, rw_a0, rw_a2, rw_g2, rw_k_k, rw_k_a, rw_r_k, rw_gn_w, rw_gn_b, w_out, g_ffn_norm, w_ff1, w_ff2, g_final):
    assert g_mix_norm.shape[0] == 1, "single layer"
    w_t = w_in[0].T
    o_a = DN_QKV + D_MODEL
    o_rwp = o_a + 2 * DN_HEADS
    o_gate = o_rwp + RW_SHIFT_W
    src = {"q": 0, "k": D_MODEL, "v": 2 * D_MODEL, "z": DN_QKV, "r": o_rwp, "kx": o_rwp + RW_W,
           "vx": o_rwp + 2 * RW_W, "lora": o_rwp + 3 * RW_W, "ga": o_gate, "gb": o_gate + D_MODEL}
    w_main_t = jnp.concatenate([w_t[src[name]:src[name] + width] for name, width, _, _ in SEGMENTS],
                               axis=0).astype(BF16)
    w_ab_t = w_t[o_a:o_rwp].astype(BF16)

    def lora_pad(wl, lo):
        return jnp.zeros((RW_LORA, RW_W), F32).at[lo:lo + wl.shape[0]].set(wl).astype(BF16)

    row = lambda v: v.reshape(1, -1).astype(F32)
    wts = {
        "g_mix": row(g_mix_norm[0]), "w_main_t": w_main_t, "w_ab_t": w_ab_t,
        "a_log": dn_a_log[0].reshape(DN_HEADS, 1), "dt_bias": dn_dt_bias[0].reshape(DN_HEADS, 1),
        "conv_w": dn_conv_w[0], "dn_norm_w": row(dn_norm_w[0]), "rw_mu": row(rw_mu[0]),
        "rw": (row(rw_w0[0]), row(rw_a0[0]), row(rw_k_k[0]), row(rw_k_a[0]), row(rw_r_k[0]),
               row(rw_gn_w[0]), row(rw_gn_b[0]),
               lora_pad(rw_w2[0], 0), lora_pad(rw_a2[0], 64), lora_pad(rw_g2[0], 128)),
        "w_out": w_out[0].astype(BF16), "g_ffn": row(g_ffn_norm[0]),
        "w_ff1": w_ff1[0].astype(BF16), "w_ff2": w_ff2[0].astype(BF16), "g_final": row(g_final),
    }

    nbm = 64 // N_META
    xm = jnp.broadcast_to(meta_tokens.astype(F32)[None], (nbm, N_META, D_MODEL))
    _, conv_m, dn_m, shift_m, rw_m = _layer(
        xm, jnp.zeros((nbm, 3, DN_QKV), F32), jnp.zeros((nbm, DN_HEADS, DN_D, DN_D), F32),
        jnp.zeros((nbm, RW_SHIFT_W), F32), jnp.zeros((nbm, RW_HEADS, RW_HD, RW_HD), F32), wts,
        sb=nbm, tc=N_META, nc=1, tm=nbm * N_META, tm_mlp=nbm * N_META)

    y_p, conv_p, dn_p, shift_p, rw_p = _layer(
        x_prompt, conv_m[:1], dn_m[:1], shift_m[:1], rw_m[:1], wts, sb=1, tc=64, nc=4, tm=512, tm_mlp=1024)

    dec_len = x_sample.shape[1]
    y_s, conv_s, dn_s, shift_s, rw_s = _layer(
        x_sample, state_dn_conv[0], state_dn[0], state_rw_shift[0], state_rw[0], wts,
        sb=SUBLANES, tc=dec_len, nc=1, tm=256, tm_mlp=1024)

    return (y_p, y_s, conv_p[None], dn_p[None], shift_p[None], rw_p[None],
            conv_s[None], dn_s[None], shift_s[None], rw_s[None])
```

```python
import functools
import math

import jax
import jax.numpy as jnp
from jax import lax
from jax.experimental import pallas as pl
from jax.experimental.pallas import tpu as pltpu

F32 = jnp.float32
BF16 = jnp.bfloat16

D_MODEL = 1024
N_META = 16
DN_HEADS = 8
DN_D = 128
DN_QKV = 3 * DN_HEADS * DN_D
RW_HEADS = 16
RW_HD = 64
RW_PAIRS = RW_HEADS // 2
RW_W = RW_HEADS * RW_HD
RW_LORA = 256
RW_SHIFT_W = 3 * RW_W + RW_LORA
D_FF = 4 * D_MODEL
NORM_EPS = 1e-6
RW_GN_EPS = 64e-5
LANES = 128
SUBLANES = 8
INV_BLOCK = 16

SEGMENTS = (
    ("q", D_MODEL, "conv", 0), ("k", D_MODEL, "conv", D_MODEL),
    ("r", RW_W, "shift", 0), ("kx", RW_W, "shift", RW_W),
    ("ga", D_MODEL, "raw", 0), ("gb", D_MODEL, "raw", 0),
    ("v", D_MODEL, "conv", 2 * D_MODEL), ("z", D_MODEL, "raw", 0),
    ("vx", RW_W, "shift", 2 * RW_W), ("lora", RW_LORA, "shift", 3 * RW_W),
)
COL = {}
P_W = 0
for _name, _width, _, _ in SEGMENTS:
    COL[_name] = P_W
    P_W += _width
P_HALF = P_W // 2
P_DTYPE = BF16
PASS0_CONV = 2 * D_MODEL
PASS0_SHIFT = 2 * RW_W


def _chunk_plan(half):
    lo, hi = half * P_HALF, (half + 1) * P_HALF
    plan = []
    for name, width, kind, off in SEGMENTS:
        start = COL[name]
        c = max(start, lo)
        while c < min(start + width, hi):
            w = min(512, min(start + width, hi) - c)
            plan.append((c - lo, w, kind, off + (c - start), name))
            c += w
    return plan

VMEM_LIMIT = 56 * 1024 * 1024


def _sigmoid(x):
    return 1.0 / (1.0 + jnp.exp(-x))


def _silu(x):
    return x * _sigmoid(x)


def _softplus(x):
    return jnp.maximum(x, 0.0) + jnp.log(1.0 + jnp.exp(-jnp.abs(x)))


def _dot(a, b):
    return jnp.dot(a.astype(BF16), b.astype(BF16), preferred_element_type=F32)


def _dot_nt(a, b):
    return lax.dot_general(a.astype(BF16), b.astype(BF16), (((1,), (1,)), ((), ())),
                           preferred_element_type=F32)


def _dot_tn(a, b):
    return lax.dot_general(a.astype(BF16), b.astype(BF16), (((0,), (0,)), ((), ())),
                           preferred_element_type=F32)


def _rows(xs):
    return xs[0] if len(xs) == 1 else jnp.concatenate(xs, axis=0)


def _dot_rows(xs, b, dot=_dot):
    out = dot(_rows(xs), b)
    off, parts = 0, []
    for x in xs:
        parts.append(out[off:off + x.shape[0]])
        off += x.shape[0]
    return parts


def _dot_mask(mask_bf16, x):
    h1 = x.astype(BF16)
    r1 = x - h1.astype(F32)
    h2 = r1.astype(BF16)
    h3 = (r1 - h2.astype(F32)).astype(BF16)
    return (jnp.dot(mask_bf16, h1, preferred_element_type=F32)
            + (jnp.dot(mask_bf16, h2, preferred_element_type=F32)
               + jnp.dot(mask_bf16, h3, preferred_element_type=F32)))


def _iota2(n, m):
    return (lax.broadcasted_iota(jnp.int32, (n, m), 0),
            lax.broadcasted_iota(jnp.int32, (n, m), 1))


def _group(idx, size):
    return lax.shift_right_logical(idx, int(math.log2(size)))


def _column_packing(ru, tc):
    ri, ci = _iota2(ru, 2 * ru)
    cc = ci & (ru - 1)
    same = _group(ri, tc) == _group(cc, tc)
    first = lax.broadcasted_iota(jnp.int32, (1, 2 * ru), 1) < ru

    def pdot(x, y):
        y_bd = jnp.concatenate([jnp.where(first, y, 0.0), jnp.where(first, 0.0, y)], axis=0)
        return _dot(x, y_bd)

    return ri, cc, same, first, pdot


def _lockstep(chains):
    chains = list(chains)
    results = [None] * len(chains)
    active = list(range(len(chains)))
    while active:
        for i in list(active):
            try:
                next(chains[i])
            except StopIteration as done:
                results[i] = done.value
                active.remove(i)
        if active:
            yield
    return results


def _run(chain):
    for _ in chain:
        pass


def _neumann_inverse(a, eye, nil, dot):
    inv = eye - a
    if nil <= 2:
        return inv
    power = dot(a, a)
    yield
    k = 4
    while k < nil:
        step, power = _dot_rows([inv, power], power, dot=dot)
        yield
        inv = inv + step
        k *= 2
    step = dot(inv, power)
    yield
    return inv + step


def _unit_lower_inverse(a, ri, ci, tc, dot=_dot):
    eye = jnp.where(ri == ci, 1.0, 0.0).astype(F32)
    if tc <= INV_BLOCK:
        return (yield from _neumann_inverse(a, eye, tc, dot))
    assert tc // INV_BLOCK <= 4
    diag = _group(ri, INV_BLOCK) == _group(ci, INV_BLOCK)
    d = jnp.where(diag, a, 0.0)
    low = a - d
    dinv = yield from _neumann_inverse(d, eye, INV_BLOCK, dot)
    b = dot(low, dinv)
    yield
    db, bb = _dot_rows([dinv, b], b, dot=dot)
    yield
    p1 = dinv - db
    tail = dot(p1, bb)
    yield
    return p1 + tail


def _proj_kernel(x_ref, g_ref, wt_ref, wab_ref, alog_ref, dtb_ref, cw_ref, mu_ref, cb_ref, prev_ref,
                 p_ref, gb_ref, cout0_ref, shift0_ref, cout1_ref, shift1_ref, *maybe_hist,
                 tm, rc, sbp, ttp, ntp):
    carry = ntp > 1
    hist = maybe_hist[0] if carry else None
    j = pl.program_id(0)
    i = pl.program_id(1)
    tpos = lax.rem(i, ntp)
    x = x_ref[...]
    h = x * lax.rsqrt(jnp.mean(x * x, axis=-1, keepdims=True) + NORM_EPS) * g_ref[...]
    hb = h.astype(BF16)
    nt_dims = (((1,), (1,)), ((), ()))

    def mm(c0, cw):
        return lax.dot_general(hb, wt_ref[c0:c0 + cw, :], nt_dims, preferred_element_type=F32)

    seq_pos = lax.broadcasted_iota(jnp.int32, (tm, 1), 0) & (ttp - 1)

    def shift_rows(a, first):
        rolled = pltpu.roll(a, len(first), axis=0)
        if sbp == 1:
            head = rolled[0:SUBLANES]
            for i, f in enumerate(first):
                head = jnp.where(seq_pos[0:SUBLANES] == i, f[0], head)
            return jnp.concatenate([head, rolled[SUBLANES:]], axis=0)
        for i, f in enumerate(first):
            fill = jnp.broadcast_to(f, (sbp, ttp, a.shape[1])).reshape(tm, a.shape[1])
            rolled = jnp.where(seq_pos == i, fill, rolled)
        return rolled

    def run_pass(plan):
        if carry:
            @pl.when(tpos == 0)
            def _():
                hist[:, 0:SUBLANES, 0:DN_QKV] = cb_ref[...]
                hist[:, 0:SUBLANES, DN_QKV:] = prev_ref[...]

            @pl.when(tpos > 0)
            def _():
                hist[:, 0:SUBLANES, :] = hist[:, SUBLANES:, :]

        for c0, cw, kind, off, name in plan:
            cur = mm(c0, cw)
            if kind == "raw":
                p_ref[:, c0:c0 + cw] = cur.astype(P_DTYPE)
                continue
            cur3 =cur.reshape(sbp, ttp, cw)
            if carry:
                hc = slice(off, off + cw) if kind == "conv" else slice(DN_QKV + off, DN_QKV + off + cw)
                old = hist[:, 0:SUBLANES, hc]
                hist[:, SUBLANES:, hc] = cur3[:, ttp - SUBLANES:, :]
            elif kind == "conv":
                old = cb_ref[:, :, off:off + cw]
                out_ref, base = (cout0_ref, 0) if off < PASS0_CONV else (cout1_ref, PASS0_CONV)
                out_ref[:, :, off - base:off - base + cw] = cur3[:, ttp - 3:, :]
            else:
                old = prev_ref[:, :, off:off + cw]
                out_ref, base = (shift0_ref, 0) if off < PASS0_SHIFT else (shift1_ref, PASS0_SHIFT)
                out_ref[:, :, off - base:off - base + cw] = cur3[:, ttp - 1:, :]
            x1 = old[:, 7:8, :]
            if kind == "shift":
                prev = shift_rows(cur, [x1])
                p_ref[:, c0:c0 + cw] = (cur + (prev - cur) * mu_ref[:, off:off + cw]).astype(P_DTYPE)
                continue
            w = [cw_ref[tap:tap + 1, off:off + cw] for tap in range(4)]
            x2, x3 = old[:, 6:7, :], old[:, 5:6, :]
            y1 = shift_rows(cur, [x1])
            far = cur * w[1] + y1 * w[0]
            acc = cur * w[3] + y1 * w[2] + shift_rows(far, [x2 * w[1] + x3 * w[0], x1 * w[1] + x2 * w[0]])
            y = _silu(acc)
            if name == "v":
                p_ref[:, c0:c0 + cw] = y.astype(P_DTYPE)
            else:
                scale = DN_D ** -0.5 if name == "q" else 1.0
                for c1 in range(0, cw, DN_D):
                    yh = y[:, c1:c1 + DN_D]
                    yh = yh * (lax.rsqrt(jnp.sum(yh * yh, axis=-1, keepdims=True) + 1e-6) * scale)
                    p_ref[:, c0 + c1:c0 + c1 + DN_D] = yh.astype(P_DTYPE)

    @pl.when(j == 0)
    def _():
        run_pass(_chunk_plan(0))
        for c in range(tm // rc):
            ab = lax.dot_general(wab_ref[...], hb[c * rc:(c + 1) * rc], nt_dims,
                                 preferred_element_type=F32)
            g = -jnp.exp(alog_ref[...]) * _softplus(ab[0:DN_HEADS] + dtb_ref[...])
            gb_ref[c, 0:DN_HEADS, :] = g
            gb_ref[c, DN_HEADS:2 * DN_HEADS, :] = _sigmoid(ab[DN_HEADS:2 * DN_HEADS])

        if carry:
            @pl.when(tpos == ntp - 1)
            def _():
                cout0_ref[...] = hist[:, 2 * SUBLANES - 3:, 0:PASS0_CONV]
                shift0_ref[...] = hist[:, 2 * SUBLANES - 1:, DN_QKV:DN_QKV + PASS0_SHIFT]

    @pl.when(j == 1)
    def _():
        run_pass(_chunk_plan(1))

        if carry:
            @pl.when(tpos == ntp - 1)
            def _():
                cout1_ref[...] = hist[:, 2 * SUBLANES - 3:, PASS0_CONV:DN_QKV]
                shift1_ref[...] = hist[:, 2 * SUBLANES - 1:, DN_QKV + PASS0_SHIFT:]


def _proj(x2d, g_norm, w_main_t, w_ab_t, a_log, dt_bias, conv_w, mu, conv_buf8, prev8, *, nb, seq, tm, rc):
    n = nb * seq
    ttp = min(seq, tm)
    sbp = tm // ttp
    ntp = seq // ttp
    assert n % tm == 0 and tm % rc == 0 and tm % ttp == 0 and seq % ttp == 0
    bcast = conv_buf8.shape[0] == 1
    assert bcast == (prev8.shape[0] == 1) and (not bcast or sbp == 1)
    last = (n // tm - 1) // ntp

    def st_idx(i):
        return 0 if bcast else i // ntp

    assert ttp % SUBLANES == 0 and ttp & (ttp - 1) == 0
    first_pass = lambda j, i: ((i // ntp) * (1 - j) + last * j, 0, 0)
    second_pass = lambda j, i: ((i // ntp) * j, 0, 0)
    p, gb, cout0, shift0, cout1, shift1 = pl.pallas_call(
        functools.partial(_proj_kernel, tm=tm, rc=rc, sbp=sbp, ttp=ttp, ntp=ntp),
        out_shape=(jax.ShapeDtypeStruct((n, P_W), P_DTYPE),
                   jax.ShapeDtypeStruct((n // rc, 2 * DN_HEADS, rc), F32),
                   jax.ShapeDtypeStruct((nb, 3, PASS0_CONV), F32),
                   jax.ShapeDtypeStruct((nb, 1, PASS0_SHIFT), F32),
                   jax.ShapeDtypeStruct((nb, 3, DN_QKV - PASS0_CONV), F32),
                   jax.ShapeDtypeStruct((nb, 1, RW_SHIFT_W - PASS0_SHIFT), F32)),
        grid=(2, n // tm),
        in_specs=[
            pl.BlockSpec((tm, D_MODEL), lambda j, i: (i, 0)),
            pl.BlockSpec((1, D_MODEL), lambda j, i: (0, 0)),
            pl.BlockSpec((P_HALF, D_MODEL), lambda j, i: (j, 0)),
            pl.BlockSpec((2 * DN_HEADS, D_MODEL), lambda j, i: (0, 0)),
            pl.BlockSpec((DN_HEADS, 1), lambda j, i: (0, 0)),
            pl.BlockSpec((DN_HEADS, 1), lambda j, i: (0, 0)),
            pl.BlockSpec((4, DN_QKV), lambda j, i: (0, 0)),
            pl.BlockSpec((1, RW_SHIFT_W), lambda j, i: (0, 0)),
            pl.BlockSpec((sbp, SUBLANES, DN_QKV), lambda j, i: (st_idx(i), 0, 0)),
            pl.BlockSpec((sbp, SUBLANES, RW_SHIFT_W), lambda j, i: (st_idx(i), 0, 0)),
        ],
        out_specs=(
            pl.BlockSpec((tm, P_HALF), lambda j, i: (i, j)),
            pl.BlockSpec((tm // rc, 2 * DN_HEADS, rc),
                         lambda j, i: (i * (1 - j) + (n // tm - 1) * j, 0, 0)),
            pl.BlockSpec((sbp, 3, PASS0_CONV), first_pass),
            pl.BlockSpec((sbp, 1, PASS0_SHIFT), first_pass),
            pl.BlockSpec((sbp, 3, DN_QKV - PASS0_CONV), second_pass),
            pl.BlockSpec((sbp, 1, RW_SHIFT_W - PASS0_SHIFT), second_pass),
        ),
        scratch_shapes=[pltpu.VMEM((sbp, 2 * SUBLANES, DN_QKV + RW_SHIFT_W), F32)] if ntp > 1 else [],
        compiler_params=pltpu.CompilerParams(
            dimension_semantics=("arbitrary", "arbitrary"), vmem_limit_bytes=VMEM_LIMIT),
        name="proj",
    )(x2d, g_norm, w_main_t, w_ab_t, a_log, dt_bias, conv_w, mu, conv_buf8, prev8)
    conv_new = jnp.concatenate([cout0, cout1], axis=-1)
    shift_new = jnp.concatenate([shift0, shift1], axis=-1)
    return p, gb, conv_new, shift_new


def _dn_kernel(q_ref, k_ref, v_ref, z_ref, gb_ref, nw_ref, s0_ref, o_ref, sout_ref, s_scr, *, sb, tc, nc, nt):
    t = pl.program_id(1)
    ru = sb * tc
    hw = DN_HEADS * DN_D

    @pl.when(t == 0)
    def _():
        s_scr[...] = s0_ref[...]

    ri, ci = _iota2(ru, ru)
    same = _group(ri, tc) == _group(ci, tc)
    incl = same & (ri >= ci)
    eye = ri == ci
    rip, cc, same_p, first_cols, pdot = _column_packing(ru, tc)
    incl_p = same_p & (rip >= cc)
    strict_p = same_p & (rip > cc)
    seqs = [slice(s * tc, (s + 1) * tc) for s in range(sb)]
    units = [slice(c * ru, (c + 1) * ru) for c in range(nc)]

    def side_by_side(x0, x1):
        return jnp.concatenate([x0, x1], axis=1)

    def block_diag(x0, x1):
        z = jnp.zeros_like(x0)
        return jnp.concatenate([side_by_side(x0, z), side_by_side(z, x1)], axis=0)

    def unit_chain(heads, rows, q, k, v):
        per = []
        for h, kh in zip(heads, k):
            g_row = gb_ref[0, h:h + 1, :][:, rows]
            b_row = gb_ref[0, DN_HEADS + h:DN_HEADS + h + 1, :][:, rows]
            g_col = jnp.sum(jnp.where(incl, g_row, 0.0), axis=1, keepdims=True)
            g_cum_row = jnp.sum(jnp.where(eye, g_col, 0.0), axis=0, keepdims=True)
            g_tot = jnp.sum(jnp.where(same, g_row, 0.0), axis=1, keepdims=True)
            b_col = jnp.sum(jnp.where(eye, b_row, 0.0), axis=1, keepdims=True)
            per.append(dict(g_col=g_col, g_cum_row=g_cum_row, g_tot=g_tot, b_col=b_col,
                            e_g=jnp.exp(g_col), e_rest=jnp.exp(g_tot - g_col), kb=kh * b_col))
        g_cols = jnp.where(first_cols, per[0]["g_col"], per[1]["g_col"])
        g_rows = side_by_side(per[0]["g_cum_row"], per[1]["g_cum_row"])
        decay = jnp.where(incl_p, jnp.exp(jnp.minimum(g_cols - g_rows, 0.0)), 0.0)
        a_raw, qk_raw = _dot_rows([side_by_side(per[0]["kb"], per[1]["kb"]), side_by_side(q[0], q[1])],
                                  block_diag(k[0], k[1]), dot=_dot_nt)
        yield
        a = jnp.where(strict_p, a_raw * decay, 0.0)
        aqk = qk_raw * decay
        t_inv = yield from _unit_lower_inverse(a, rip, cc, tc, dot=pdot)
        rhs = [side_by_side(vh * w["b_col"], w["kb"] * w["e_g"]) for vh, w in zip(v, per)]
        sol = _dot(t_inv, block_diag(rhs[0], rhs[1]))
        yield
        out = []
        for i, w in enumerate(per):
            base = 2 * DN_D * i
            out.append(dict(u_all=sol[:, base:base + DN_D], w_all=sol[:, base + DN_D:base + 2 * DN_D],
                            qg=q[i] * w["e_g"], kd=k[i] * w["e_rest"], g_tot=w["g_tot"]))
        return aqk, out

    def pair_chain(pair):
        heads = (2 * pair, 2 * pair + 1)
        cols = [slice(h * DN_D, (h + 1) * DN_D) for h in heads]
        q = [q_ref[:, c].astype(F32) for c in cols]
        k = [k_ref[:, c].astype(F32) for c in cols]
        v = [v_ref[:, c].astype(F32) for c in cols]
        wy = yield from _lockstep(
            unit_chain(heads, rows, [x[rows] for x in q], [x[rows] for x in k], [x[rows] for x in v])
            for rows in units)
        for rows, (aqk, per) in zip(units, wy):
            ws = [[_dot(jnp.concatenate([w["w_all"][sq], w["qg"][sq]], axis=0), s_scr[s, h])
                   for s, sq in enumerate(seqs)] for h, w in zip(heads, per)]
            yield
            u = [_rows([w["u_all"][sq] - x[:tc] for sq, x in zip(seqs, wsh)]) for w, wsh in zip(per, ws)]
            q_s = [_rows([x[tc:] for x in wsh]) for wsh in ws]
            intra = _dot(aqk, block_diag(u[0], u[1]))
            upd = [[_dot_tn(w["kd"][sq], uh[sq]) for sq in seqs] for w, uh in zip(per, u)]
            yield
            for i, (h, w) in enumerate(zip(heads, per)):
                for s in range(sb):
                    gl = jnp.exp(w["g_tot"][s * tc:s * tc + 1, :])
                    s_scr[s, h] = s_scr[s, h] * gl + upd[i][s]
                o = q_s[i] + intra[:, i * DN_D:(i + 1) * DN_D]
                o = o * lax.rsqrt(jnp.mean(o * o, axis=-1, keepdims=True) + NORM_EPS) * nw_ref[...]
                o_ref[rows, cols[i]] = o * _silu(z_ref[rows, cols[i]].astype(F32))

    _run(_lockstep(pair_chain(pair) for pair in range(DN_HEADS // 2)))

    @pl.when(t == nt - 1)
    def _():
        sout_ref[...] = s_scr[...]


def _dn(p, gb, norm_w, s0, *, nb, seq, sb, tc, nc):
    nt = seq // (nc * tc)
    r = sb * nc * tc
    assert seq % (nc * tc) == 0 and nb % sb == 0 and (nt == 1 or sb == 1) and (nc == 1 or sb == 1)
    bcast = s0.shape[0] == 1
    assert not bcast or sb == 1
    hw = DN_HEADS * DN_D

    def st_idx(i):
        return 0 if bcast else i

    def rows(name):
        return pl.BlockSpec((r, hw), lambda i, t: (i * nt + t, COL[name] // hw))

    return pl.pallas_call(
        functools.partial(_dn_kernel, sb=sb, tc=tc, nc=nc, nt=nt),
        out_shape=(jax.ShapeDtypeStruct((nb * seq, hw), F32),
                   jax.ShapeDtypeStruct((nb, DN_HEADS, DN_D, DN_D), F32)),
        grid=(nb // sb, nt),
        in_specs=[
            rows("q"), rows("k"), rows("v"), rows("z"),
            pl.BlockSpec((1, 2 * DN_HEADS, r), lambda i, t: (i * nt + t, 0, 0)),
            pl.BlockSpec((1, DN_D), lambda i, t: (0, 0)),
            pl.BlockSpec((sb, DN_HEADS, DN_D, DN_D), lambda i, t: (st_idx(i), 0, 0, 0)),
        ],
        out_specs=(
            pl.BlockSpec((r, hw), lambda i, t: (i * nt + t, 0)),
            pl.BlockSpec((sb, DN_HEADS, DN_D, DN_D), lambda i, t: (i, 0, 0, 0)),
        ),
        scratch_shapes=[pltpu.VMEM((sb, DN_HEADS, DN_D, DN_D), F32)],
        compiler_params=pltpu.CompilerParams(
            dimension_semantics=("arbitrary", "arbitrary"), vmem_limit_bytes=VMEM_LIMIT),
        name="dn",
    )(p, p, p, p, gb, norm_w, s0)


def _rw_kernel(r_ref, k_ref, v_ref, l_ref, w0_ref, a0_ref, kkw_ref, kaw_ref, rkw_ref,
               gnw_ref, gnb_ref, w2_ref, a2_ref, g2_ref, s0_ref, odn_ref, ga_ref, gb_ref,
               o_ref, sout_ref, s_scr, *, sb, tc, nc, nt):
    t = pl.program_id(1)
    ru = sb * tc
    half = lax.broadcasted_iota(jnp.int32, (1, LANES), 1) < RW_HD
    ri128, ci128 = _iota2(LANES, LANES)
    bd_mask = _group(ri128, RW_HD) == _group(ci128, RW_HD)

    def seg_sum(x):
        first = jnp.sum(jnp.where(half, x, 0.0), axis=-1, keepdims=True)
        second = jnp.sum(jnp.where(half, 0.0, x), axis=-1, keepdims=True)
        return jnp.where(half, first, second)

    @pl.when(t == 0)
    def _():
        zero = jnp.zeros((RW_HD, RW_HD), F32)
        for s in range(sb):
            for p in range(RW_PAIRS):
                top = jnp.concatenate([s0_ref[s, 2 * p], zero], axis=1)
                bot = jnp.concatenate([zero, s0_ref[s, 2 * p + 1]], axis=1)
                s_scr[s, p] = jnp.concatenate([top, bot], axis=0)

    xm_l = l_ref[...].astype(F32)
    lora_w = jnp.tanh(xm_l).astype(BF16)
    lora_a = xm_l.astype(BF16)
    lora_g = _sigmoid(xm_l).astype(BF16)

    ri, ci = _iota2(ru, ru)
    cum_mask = jnp.where((_group(ri, tc) == _group(ci, tc)) & (ri >= ci), 1.0, 0.0).astype(BF16)
    rip, cc, same_p, _, pdot = _column_packing(ru, tc)
    incl_p = same_p & (rip >= cc)
    strict_p = same_p & (rip > cc)
    seqs = [slice(s * tc, (s + 1) * tc) for s in range(sb)]
    units = [slice(c * ru, (c + 1) * ru) for c in range(nc)]

    def stack(x):
        return jnp.concatenate([jnp.where(half, x, 0.0), jnp.where(half, 0.0, x)], axis=0)

    def unit_chain(rr, k2, vv, kk, bb, logw):
        lcum = _dot_mask(cum_mask, logw)
        yield
        ltot = jnp.broadcast_to(lcum.reshape(sb, tc, LANES)[:, tc - 1:tc, :],
                                (sb, tc, LANES)).reshape(ru, LANES)
        p_in = jnp.exp(lcum)
        p_inv = jnp.exp(-lcum)
        alpha = kk * jnp.exp(lcum - logw)
        beta = bb * p_inv
        kt = k2 * p_inv
        rt = rr * p_in
        e_rest = jnp.exp(ltot - lcum)
        beta2, kt2, v2 = stack(beta), stack(kt), stack(vv)
        l_raw, rb_raw = _dot_rows([alpha, rt], beta2, dot=_dot_nt)
        lk_raw, rk_raw = _dot_rows([alpha, rt], kt2, dot=_dot_nt)
        yield
        lmat = jnp.where(strict_p, l_raw, 0.0)
        rb = jnp.where(incl_p, rb_raw, 0.0)
        lk = jnp.where(strict_p, lk_raw, 0.0)
        rk = jnp.where(incl_p, rk_raw, 0.0)
        lkv, rkv = _dot_rows([lk, rk], v2)
        t_inv = yield from _unit_lower_inverse(lmat, rip, cc, tc, dot=pdot)
        sol = _dot(t_inv, jnp.concatenate([stack(alpha), stack(lkv)], axis=1))
        yield
        return dict(wa=sol[:, :LANES], uv=sol[:, LANES:], rt=rt, rb=rb, rkv=rkv, p_in=p_in,
                    k_end=k2 * e_rest, b_end=bb * e_rest)

    def pair_chain(p):
        c = slice(p * LANES, (p + 1) * LANES)
        rr = r_ref[:, c].astype(F32)
        xk = k_ref[:, c].astype(F32)
        vv = v_ref[:, c].astype(F32)
        lw = jnp.dot(lora_w, w2_ref[:, c], preferred_element_type=F32)
        la = jnp.dot(lora_a, a2_ref[:, c], preferred_element_type=F32)
        gate = jnp.dot(lora_g, g2_ref[:, c], preferred_element_type=F32)
        yield
        logw = -math.exp(-0.5) * _sigmoid(w0_ref[:, c] + lw)
        aa = _sigmoid(a0_ref[:, c] + la)
        kk = xk * kkw_ref[:, c]
        k2 = xk * (1.0 + (aa - 1.0) * kaw_ref[:, c])
        kk = kk * lax.rsqrt(seg_sum(kk * kk) + 1e-6)
        bb = kk * aa
        wy = yield from _lockstep(
            unit_chain(rr[rows], k2[rows], vv[rows], kk[rows], bb[rows], logw[rows]) for rows in units)
        for rows, w in zip(units, wy):
            st = [_dot_nt(jnp.concatenate([w["wa"][sq], w["rt"][sq]], axis=0), s_scr[s, p])
                  for s, sq in enumerate(seqs)]
            yield
            u = w["uv"] + _rows([x[:tc] for x in st])
            r_state = _rows([x[tc:] for x in st])
            rbu = _dot(w["rb"], stack(u))
            v_u = vv[rows]
            upd = [_dot_tn(jnp.concatenate([v_u[sq], -u[sq]], axis=0),
                           jnp.concatenate([w["k_end"][sq], w["b_end"][sq]], axis=0)) for sq in seqs]
            yield
            for s in range(sb):
                p_end = w["p_in"][(s + 1) * tc - 1:(s + 1) * tc, :]
                s_scr[s, p] = s_scr[s, p] * p_end + jnp.where(bd_mask, upd[s], 0.0)
            o = r_state + (w["rkv"] - rbu)
            mean = seg_sum(o) * (1.0 / RW_HD)
            d = o - mean
            var = seg_sum(d * d) * (1.0 / RW_HD)
            bonus = seg_sum(rr[rows] * k2[rows] * rkw_ref[:, c])
            on = d * lax.rsqrt(var + RW_GN_EPS) * gnw_ref[:, c] + gnb_ref[:, c]
            o_rw = (on + bonus * v_u) * gate[rows]
            mix = (_sigmoid(ga_ref[rows, c].astype(F32)) * odn_ref[rows, c]
                   + _sigmoid(gb_ref[rows, c].astype(F32)) * o_rw)
            o_ref[rows, c] = mix.astype(BF16)

    _run(_lockstep(pair_chain(p) for p in range(RW_PAIRS)))

    @pl.when(t == nt - 1)
    def _():
        for s in range(sb):
            for p in range(RW_PAIRS):
                sp = s_scr[s, p]
                sout_ref[s, 2 * p] = sp[0:RW_HD, 0:RW_HD]
                sout_ref[s, 2 * p + 1] = sp[RW_HD:, RW_HD:]


def _rw(p, o_dn, weights, s0, *, nb, seq, sb, tc, nc):
    nt = seq // (nc * tc)
    r = sb * nc * tc
    assert seq % (nc * tc) == 0 and nb % sb == 0 and (nt == 1 or sb == 1) and (nc == 1 or sb == 1)
    bcast = s0.shape[0] == 1
    assert not bcast or sb == 1
    (w0, a0, kkw, kaw, rkw, gnw, gnb, w2p, a2p, g2p) = weights

    def st_idx(i):
        return 0 if bcast else i

    def vec(width):
        return pl.BlockSpec((1, width), lambda i, t: (0, 0))

    def lora_w():
        return pl.BlockSpec((RW_LORA, RW_W), lambda i, t: (0, 0))

    def rows(col):
        return pl.BlockSpec((r, RW_W), lambda i, t: (i * nt + t, col))

    def named(name):
        return rows(COL[name] // RW_W)

    return pl.pallas_call(
        functools.partial(_rw_kernel, sb=sb, tc=tc, nc=nc, nt=nt),
        out_shape=(jax.ShapeDtypeStruct((nb * seq, RW_W), BF16),
                   jax.ShapeDtypeStruct((nb, RW_HEADS, RW_HD, RW_HD), F32)),
        grid=(nb // sb, nt),
        in_specs=[
            named("r"), named("kx"), named("vx"),
            pl.BlockSpec((r, RW_LORA), lambda i, t: (i * nt + t, COL["lora"] // RW_LORA)),
            vec(RW_W), vec(RW_W), vec(RW_W), vec(RW_W), vec(RW_W), vec(RW_W), vec(RW_W),
            lora_w(), lora_w(), lora_w(),
            pl.BlockSpec((sb, RW_HEADS, RW_HD, RW_HD), lambda i, t: (st_idx(i), 0, 0, 0)),
            rows(0), named("ga"), named("gb"),
        ],
        out_specs=(
            rows(0),
            pl.BlockSpec((sb, RW_HEADS, RW_HD, RW_HD), lambda i, t: (i, 0, 0, 0)),
        ),
        scratch_shapes=[pltpu.VMEM((sb, RW_PAIRS, LANES, LANES), F32)],
        compiler_params=pltpu.CompilerParams(
            dimension_semantics=("arbitrary", "arbitrary"), vmem_limit_bytes=VMEM_LIMIT),
        name="rw",
    )(p, p, p, p, w0, a0, kkw, kaw, rkw, gnw, gnb, w2p, a2p, g2p, s0, o_dn, p, p)


def _merge_kernel(x_ref, mix_ref, wout_ref, gffn_ref, w1_ref, w2_ref, gfin_ref,
                  y_ref, x1_scr, h2_scr, acc_scr, *, nf):
    f = pl.program_id(1)

    @pl.when(f == 0)
    def _():
        x1 = x_ref[...] + jnp.dot(mix_ref[...], wout_ref[...], preferred_element_type=F32)
        x1_scr[...] = x1
        h2 = x1 * lax.rsqrt(jnp.mean(x1 * x1, axis=-1, keepdims=True) + NORM_EPS) * gffn_ref[...]
        h2_scr[...] = h2.astype(BF16)
        acc_scr[...] = jnp.zeros_like(acc_scr)

    hid = jnp.dot(h2_scr[...], w1_ref[...], preferred_element_type=F32)
    hid = jnp.square(jnp.maximum(hid, 0.0))
    acc_scr[...] += jnp.dot(hid.astype(BF16), w2_ref[...], preferred_element_type=F32)

    @pl.when(f == nf - 1)
    def _():
        y = x1_scr[...] + acc_scr[...]
        y_ref[...] = y * lax.rsqrt(jnp.mean(y * y, axis=-1, keepdims=True) + NORM_EPS) * gfin_ref[...]


def _merge(x2d, mix, w_out, g_ffn, w1, w2, g_fin, *, tm, tf):
    n = x2d.shape[0]
    nf = D_FF // tf
    assert n % tm == 0 and D_FF % tf == 0
    row = lambda i, f: (i, 0)
    return pl.pallas_call(
        functools.partial(_merge_kernel, nf=nf),
        out_shape=jax.ShapeDtypeStruct((n, D_MODEL), F32),
        grid=(n // tm, nf),
        in_specs=[
            pl.BlockSpec((tm, D_MODEL), row),
            pl.BlockSpec((tm, D_MODEL), row),
            pl.BlockSpec((D_MODEL, D_MODEL), lambda i, f: (0, 0)),
            pl.BlockSpec((1, D_MODEL), lambda i, f: (0, 0)),
            pl.BlockSpec((D_MODEL, tf), lambda i, f: (0, f)),
            pl.BlockSpec((tf, D_MODEL), lambda i, f: (f, 0)),
            pl.BlockSpec((1, D_MODEL), lambda i, f: (0, 0)),
        ],
        out_specs=pl.BlockSpec((tm, D_MODEL), row),
        scratch_shapes=[
            pltpu.VMEM((tm, D_MODEL), F32),
            pltpu.VMEM((tm, D_MODEL), BF16),
            pltpu.VMEM((tm, D_MODEL), F32),
        ],
        compiler_params=pltpu.CompilerParams(
            dimension_semantics=("arbitrary", "arbitrary"), vmem_limit_bytes=VMEM_LIMIT),
        name="merge",
    )(x2d, mix, w_out, g_ffn, w1, w2, g_fin)


def _pad_rows_front(x, rows):
    b, n, w = x.shape
    return jnp.concatenate([jnp.zeros((b, rows - n, w), x.dtype), x], axis=1)


def _layer(x, conv_buf, dn_s, rw_prev, rw_s, wts, *, sb, tc, nc, tm, tm_mlp):
    nb, seq, _ = x.shape
    n = nb * seq
    x2d = x.reshape(n, D_MODEL)
    p, gb, conv_new, shift_new = _proj(
        x2d, wts["g_mix"], wts["w_main_t"], wts["w_ab_t"], wts["a_log"], wts["dt_bias"], wts["conv_w"],
        wts["rw_mu"], _pad_rows_front(conv_buf, SUBLANES), _pad_rows_front(rw_prev[:, None, :], SUBLANES),
        nb=nb, seq=seq, tm=tm, rc=sb * nc * tc)
    o_dn, dn_new = _dn(p, gb, wts["dn_norm_w"], dn_s, nb=nb, seq=seq, sb=sb, tc=tc, nc=nc)
    mix, rw_new = _rw(p, o_dn, wts["rw"], rw_s, nb=nb, seq=seq, sb=sb, tc=tc, nc=nc)
    y = _merge(x2d, mix, wts["w_out"], wts["g_ffn"], wts["w_ff1"], wts["w_ff2"], wts["g_final"],
               tm=tm_mlp, tf=1024)
    return y.reshape(nb, seq, D_MODEL), conv_new, dn_new, shift_new[:, 0], rw_new


def kernel(x_prompt, x_sample, state_dn_conv, state_dn, state_rw_shift, state_rw, meta_tokens, g_mix_norm, w_in, dn_conv_w, dn_a_log, dn_dt_bias, dn_norm_w, rw_mu, rw_w0, rw_w2, rw_a0, rw_a2, rw_g2, rw_k_k, rw_k_a, rw_r_k, rw_gn_w, rw_gn_b, w_out, g_ffn_norm, w_ff1, w_ff2, g_final):
    assert g_mix_norm.shape[0] == 1, "single layer"
    w_t = w_in[0].T
    o_a = DN_QKV + D_MODEL
    o_rwp = o_a + 2 * DN_HEADS
    o_gate = o_rwp + RW_SHIFT_W
    src = {"q": 0, "k": D_MODEL, "v": 2 * D_MODEL, "z": DN_QKV, "r": o_rwp, "kx": o_rwp + RW_W,
           "vx": o_rwp + 2 * RW_W, "lora": o_rwp + 3 * RW_W, "ga": o_gate, "gb": o_gate + D_MODEL}
    w_main_t = jnp.concatenate([w_t[src[name]:src[name] + width] for name, width, _, _ in SEGMENTS],
                               axis=0).astype(BF16)
    w_ab_t = w_t[o_a:o_rwp].astype(BF16)

    def lora_pad(wl, lo):
        return jnp.zeros((RW_LORA, RW_W), F32).at[lo:lo + wl.shape[0]].set(wl).astype(BF16)

    row = lambda v: v.reshape(1, -1).astype(F32)
    wts = {
        "g_mix": row(g_mix_norm[0]), "w_main_t": w_main_t, "w_ab_t": w_ab_t,
        "a_log": dn_a_log[0].reshape(DN_HEADS, 1), "dt_bias": dn_dt_bias[0].reshape(DN_HEADS, 1),
        "conv_w": dn_conv_w[0], "dn_norm_w": row(dn_norm_w[0]), "rw_mu": row(rw_mu[0]),
        "rw": (row(rw_w0[0]), row(rw_a0[0]), row(rw_k_k[0]), row(rw_k_a[0]), row(rw_r_k[0]),
               row(rw_gn_w[0]), row(rw_gn_b[0]),
               lora_pad(rw_w2[0], 0), lora_pad(rw_a2[0], 64), lora_pad(rw_g2[0], 128)),
        "w_out": w_out[0].astype(BF16), "g_ffn": row(g_ffn_norm[0]),
        "w_ff1": w_ff1[0].astype(BF16), "w_ff2": w_ff2[0].astype(BF16), "g_final": row(g_final),
    }

    nbm = 64 // N_META
    xm = jnp.broadcast_to(meta_tokens.astype(F32)[None], (nbm, N_META, D_MODEL))
    _, conv_m, dn_m, shift_m, rw_m = _layer(
        xm, jnp.zeros((nbm, 3, DN_QKV), F32), jnp.zeros((nbm, DN_HEADS, DN_D, DN_D), F32),
        jnp.zeros((nbm, RW_SHIFT_W), F32), jnp.zeros((nbm, RW_HEADS, RW_HD, RW_HD), F32), wts,
        sb=nbm, tc=N_META, nc=1, tm=nbm * N_META, tm_mlp=nbm * N_META)

    y_p, conv_p, dn_p, shift_p, rw_p = _layer(
        x_prompt, conv_m[:1], dn_m[:1], shift_m[:1], rw_m[:1], wts, sb=1, tc=64, nc=4, tm=1024, tm_mlp=1024)

    dec_len = x_sample.shape[1]
    y_s, conv_s, dn_s, shift_s, rw_s = _layer(
        x_sample, state_dn_conv[0], state_dn[0], state_rw_shift[0], state_rw[0], wts,
        sb=SUBLANES, tc=dec_len, nc=1, tm=256, tm_mlp=1024)

    return (y_p, y_s, conv_p[None], dn_p[None], shift_p[None], rw_p[None],
            conv_s[None], dn_s[None], shift_s[None], rw_s[None])
```

```python
import functools
import math

import jax
import jax.numpy as jnp
from jax import lax
from jax.experimental import pallas as pl
from jax.experimental.pallas import tpu as pltpu

F32 = jnp.float32
BF16 = jnp.bfloat16

D_MODEL = 1024
N_META = 16
DN_HEADS = 8
DN_D = 128
DN_QKV = 3 * DN_HEADS * DN_D
RW_HEADS = 16
RW_HD = 64
RW_PAIRS = RW_HEADS // 2
RW_W = RW_HEADS * RW_HD
RW_LORA = 256
RW_SHIFT_W = 3 * RW_W + RW_LORA
D_FF = 4 * D_MODEL
NORM_EPS = 1e-6
RW_GN_EPS = 64e-5
LANES = 128
SUBLANES = 8
INV_BLOCK = 16
CHUNK = 64
UNIT_ROWS = 64
CHUNKS_PER_STEP = 4
PROJ_TILE = 512
PROJ_TILE_SHORT = 256
MLP_TILE = 1024
FF_TILE = 1024

SEGMENTS = (
    ("q", D_MODEL, "conv", 0), ("k", D_MODEL, "conv", D_MODEL),
    ("r", RW_W, "shift", 0), ("kx", RW_W, "shift", RW_W),
    ("ga", D_MODEL, "raw", 0), ("gb", D_MODEL, "raw", 0),
    ("v", D_MODEL, "conv", 2 * D_MODEL), ("z", D_MODEL, "raw", 0),
    ("vx", RW_W, "shift", 2 * RW_W), ("lora", RW_LORA, "shift", 3 * RW_W),
)
COL = {}
P_W = 0
for _name, _width, _, _ in SEGMENTS:
    COL[_name] = P_W
    P_W += _width
P_HALF = P_W // 2
PASS0_CONV = 2 * D_MODEL
PASS0_SHIFT = 2 * RW_W
PIECE_W = 512


def _chunk_plan(half):
    lo, hi = half * P_HALF, (half + 1) * P_HALF
    plan = []
    for name, width, kind, off in SEGMENTS:
        start = COL[name]
        c = max(start, lo)
        while c < min(start + width, hi):
            w = min(PIECE_W, min(start + width, hi) - c)
            plan.append((c - lo, w, kind, off + (c - start), name))
            c += w
    return plan

VMEM_LIMIT = 56 * 1024 * 1024


def _sigmoid(x):
    return 1.0 / (1.0 + jnp.exp(-x))


def _silu(x):
    return x * _sigmoid(x)


def _softplus(x):
    return jnp.maximum(x, 0.0) + jnp.log(1.0 + jnp.exp(-jnp.abs(x)))


def _dot(a, b):
    return jnp.dot(a.astype(BF16), b.astype(BF16), preferred_element_type=F32)


def _dot_nt(a, b):
    return lax.dot_general(a.astype(BF16), b.astype(BF16), (((1,), (1,)), ((), ())),
                           preferred_element_type=F32)


def _dot_tn(a, b):
    return lax.dot_general(a.astype(BF16), b.astype(BF16), (((0,), (0,)), ((), ())),
                           preferred_element_type=F32)


def _rows(xs):
    return xs[0] if len(xs) == 1 else jnp.concatenate(xs, axis=0)


def _dot_rows(xs, b, dot=_dot):
    out = dot(_rows(xs), b)
    off, parts = 0, []
    for x in xs:
        parts.append(out[off:off + x.shape[0]])
        off += x.shape[0]
    return parts


def _dot_mask(mask_bf16, x):
    h1 = x.astype(BF16)
    r1 = x - h1.astype(F32)
    h2 = r1.astype(BF16)
    h3 = (r1 - h2.astype(F32)).astype(BF16)
    return (jnp.dot(mask_bf16, h1, preferred_element_type=F32)
            + (jnp.dot(mask_bf16, h2, preferred_element_type=F32)
               + jnp.dot(mask_bf16, h3, preferred_element_type=F32)))


def _iota2(n, m):
    return (lax.broadcasted_iota(jnp.int32, (n, m), 0),
            lax.broadcasted_iota(jnp.int32, (n, m), 1))


def _group(idx, size):
    return lax.shift_right_logical(idx, int(math.log2(size)))


def _column_packing(ru, tc):
    ri, ci = _iota2(ru, 2 * ru)
    cc = ci & (ru - 1)
    same = _group(ri, tc) == _group(cc, tc)
    first = lax.broadcasted_iota(jnp.int32, (1, 2 * ru), 1) < ru

    def pdot(x, y):
        y_bd = jnp.concatenate([jnp.where(first, y, 0.0), jnp.where(first, 0.0, y)], axis=0)
        return _dot(x, y_bd)

    return ri, cc, same, first, pdot


def _lockstep(chains):
    chains = list(chains)
    results = [None] * len(chains)
    active = list(range(len(chains)))
    while active:
        for i in list(active):
            try:
                next(chains[i])
            except StopIteration as done:
                results[i] = done.value
                active.remove(i)
        if active:
            yield
    return results


def _run(chain):
    for _ in chain:
        pass


def _neumann_inverse(a, eye, nil, dot):
    inv = eye - a
    if nil <= 2:
        return inv
    power = dot(a, a)
    yield
    k = 4
    while k < nil:
        step, power = _dot_rows([inv, power], power, dot=dot)
        yield
        inv = inv + step
        k *= 2
    step = dot(inv, power)
    yield
    return inv + step


def _unit_lower_inverse(a, ri, ci, tc, dot=_dot):
    eye = jnp.where(ri == ci, 1.0, 0.0).astype(F32)
    if tc <= INV_BLOCK:
        return (yield from _neumann_inverse(a, eye, tc, dot))
    assert tc // INV_BLOCK <= 4
    diag = _group(ri, INV_BLOCK) == _group(ci, INV_BLOCK)
    d = jnp.where(diag, a, 0.0)
    low = a - d
    dinv = yield from _neumann_inverse(d, eye, INV_BLOCK, dot)
    b = dot(low, dinv)
    yield
    db, bb = _dot_rows([dinv, b], b, dot=dot)
    yield
    p1 = dinv - db
    tail = dot(p1, bb)
    yield
    return p1 + tail


def _proj_kernel(x_ref, g_ref, wt_ref, wab_ref, alog_ref, dtb_ref, cw_ref, mu_ref, cb_ref, prev_ref,
                 p_ref, gb_ref, cout0_ref, shift0_ref, cout1_ref, shift1_ref, *maybe_hist,
                 tm, rc, sbp, ttp, ntp):
    carry = ntp > 1
    hist = maybe_hist[0] if carry else None
    j = pl.program_id(0)
    i = pl.program_id(1)
    tpos = lax.rem(i, ntp)
    x = x_ref[...]
    h = x * lax.rsqrt(jnp.mean(x * x, axis=-1, keepdims=True) + NORM_EPS) * g_ref[...]
    hb = h.astype(BF16)
    nt_dims = (((1,), (1,)), ((), ()))

    def mm(c0, cw):
        return lax.dot_general(hb, wt_ref[c0:c0 + cw, :], nt_dims, preferred_element_type=F32)

    seq_pos = lax.broadcasted_iota(jnp.int32, (tm, 1), 0) & (ttp - 1)

    def shift_rows(a, first):
        rolled = pltpu.roll(a, len(first), axis=0)
        if sbp == 1:
            head = rolled[0:SUBLANES]
            for i, f in enumerate(first):
                head = jnp.where(seq_pos[0:SUBLANES] == i, f[0], head)
            return jnp.concatenate([head, rolled[SUBLANES:]], axis=0)
        for i, f in enumerate(first):
            fill = jnp.broadcast_to(f, (sbp, ttp, a.shape[1])).reshape(tm, a.shape[1])
            rolled = jnp.where(seq_pos == i, fill, rolled)
        return rolled

    def run_pass(plan):
        if carry:
            @pl.when(tpos == 0)
            def _():
                hist[:, 0:SUBLANES, 0:DN_QKV] = cb_ref[...]
                hist[:, 0:SUBLANES, DN_QKV:] = prev_ref[...]

            @pl.when(tpos > 0)
            def _():
                hist[:, 0:SUBLANES, :] = hist[:, SUBLANES:, :]

        for c0, cw, kind, off, name in plan:
            cur = mm(c0, cw)
            if kind == "raw":
                p_ref[:, c0:c0 + cw] = cur
                continue
            cur3 = cur.reshape(sbp, ttp, cw)
            if carry:
                hc = slice(off, off + cw) if kind == "conv" else slice(DN_QKV + off, DN_QKV + off + cw)
                old = hist[:, 0:SUBLANES, hc]
                hist[:, SUBLANES:, hc] = cur3[:, ttp - SUBLANES:, :]
            elif kind == "conv":
                old = cb_ref[:, :, off:off + cw]
                out_ref, base = (cout0_ref, 0) if off < PASS0_CONV else (cout1_ref, PASS0_CONV)
                out_ref[:, :, off - base:off - base + cw] = cur3[:, ttp - 3:, :]
            else:
                old = prev_ref[:, :, off:off + cw]
                out_ref, base = (shift0_ref, 0) if off < PASS0_SHIFT else (shift1_ref, PASS0_SHIFT)
                out_ref[:, :, off - base:off - base + cw] = cur3[:, ttp - 1:, :]
            x1 = old[:, 7:8, :]
            if kind == "shift":
                prev = shift_rows(cur, [x1])
                p_ref[:, c0:c0 + cw] = cur + (prev - cur) * mu_ref[:, off:off + cw]
                continue
            w = [cw_ref[tap:tap + 1, off:off + cw] for tap in range(4)]
            x2, x3 = old[:, 6:7, :], old[:, 5:6, :]
            y1 = shift_rows(cur, [x1])
            far = cur * w[1] + y1 * w[0]
            acc = cur * w[3] + y1 * w[2] + shift_rows(far, [x2 * w[1] + x3 * w[0], x1 * w[1] + x2 * w[0]])
            y = _silu(acc)
            if name == "v":
                p_ref[:, c0:c0 + cw] = y
            else:
                scale = DN_D ** -0.5 if name == "q" else 1.0
                for c1 in range(0, cw, DN_D):
                    yh = y[:, c1:c1 + DN_D]
                    yh = yh * (lax.rsqrt(jnp.sum(yh * yh, axis=-1, keepdims=True) + 1e-6) * scale)
                    p_ref[:, c0 + c1:c0 + c1 + DN_D] = yh

    @pl.when(j == 0)
    def _():
        run_pass(_chunk_plan(0))
        for c in range(tm // rc):
            ab = lax.dot_general(wab_ref[...], hb[c * rc:(c + 1) * rc], nt_dims,
                                 preferred_element_type=F32)
            g = -jnp.exp(alog_ref[...]) * _softplus(ab[0:DN_HEADS] + dtb_ref[...])
            gb_ref[c, 0:DN_HEADS, :] = g
            gb_ref[c, DN_HEADS:2 * DN_HEADS, :] = _sigmoid(ab[DN_HEADS:2 * DN_HEADS])

        if carry:
            @pl.when(tpos == ntp - 1)
            def _():
                cout0_ref[...] = hist[:, 2 * SUBLANES - 3:, 0:PASS0_CONV]
                shift0_ref[...] = hist[:, 2 * SUBLANES - 1:, DN_QKV:DN_QKV + PASS0_SHIFT]

    @pl.when(j == 1)
    def _():
        run_pass(_chunk_plan(1))

        if carry:
            @pl.when(tpos == ntp - 1)
            def _():
                cout1_ref[...] = hist[:, 2 * SUBLANES - 3:, PASS0_CONV:DN_QKV]
                shift1_ref[...] = hist[:, 2 * SUBLANES - 1:, DN_QKV + PASS0_SHIFT:]


def _proj(x2d, g_norm, w_main_t, w_ab_t, a_log, dt_bias, conv_w, mu, conv_buf8, prev8, *, nb, seq, tm, rc):
    n = nb * seq
    ttp = min(seq, tm)
    sbp = tm // ttp
    ntp = seq // ttp
    assert n % tm == 0 and tm % rc == 0 and tm % ttp == 0 and seq % ttp == 0
    bcast = conv_buf8.shape[0] == 1
    assert bcast == (prev8.shape[0] == 1) and (not bcast or sbp == 1)
    last = (n // tm - 1) // ntp

    def st_idx(i):
        return 0 if bcast else i // ntp

    assert ttp % SUBLANES == 0 and ttp & (ttp - 1) == 0
    first_pass = lambda j, i: ((i // ntp) * (1 - j) + last * j, 0, 0)
    second_pass = lambda j, i: ((i // ntp) * j, 0, 0)
    p, gb, cout0, shift0, cout1, shift1 = pl.pallas_call(
        functools.partial(_proj_kernel, tm=tm, rc=rc, sbp=sbp, ttp=ttp, ntp=ntp),
        out_shape=(jax.ShapeDtypeStruct((n, P_W), F32),
                   jax.ShapeDtypeStruct((n // rc, 2 * DN_HEADS, rc), F32),
                   jax.ShapeDtypeStruct((nb, 3, PASS0_CONV), F32),
                   jax.ShapeDtypeStruct((nb, 1, PASS0_SHIFT), F32),
                   jax.ShapeDtypeStruct((nb, 3, DN_QKV - PASS0_CONV), F32),
                   jax.ShapeDtypeStruct((nb, 1, RW_SHIFT_W - PASS0_SHIFT), F32)),
        grid=(2, n // tm),
        in_specs=[
            pl.BlockSpec((tm, D_MODEL), lambda j, i: (i, 0)),
            pl.BlockSpec((1, D_MODEL), lambda j, i: (0, 0)),
            pl.BlockSpec((P_HALF, D_MODEL), lambda j, i: (j, 0)),
            pl.BlockSpec((2 * DN_HEADS, D_MODEL), lambda j, i: (0, 0)),
            pl.BlockSpec((DN_HEADS, 1), lambda j, i: (0, 0)),
            pl.BlockSpec((DN_HEADS, 1), lambda j, i: (0, 0)),
            pl.BlockSpec((4, DN_QKV), lambda j, i: (0, 0)),
            pl.BlockSpec((1, RW_SHIFT_W), lambda j, i: (0, 0)),
            pl.BlockSpec((sbp, SUBLANES, DN_QKV), lambda j, i: (st_idx(i), 0, 0)),
            pl.BlockSpec((sbp, SUBLANES, RW_SHIFT_W), lambda j, i: (st_idx(i), 0, 0)),
        ],
        out_specs=(
            pl.BlockSpec((tm, P_HALF), lambda j, i: (i, j)),
            pl.BlockSpec((tm // rc, 2 * DN_HEADS, rc),
                         lambda j, i: (i * (1 - j) + (n // tm - 1) * j, 0, 0)),
            pl.BlockSpec((sbp, 3, PASS0_CONV), first_pass),
            pl.BlockSpec((sbp, 1, PASS0_SHIFT), first_pass),
            pl.BlockSpec((sbp, 3, DN_QKV - PASS0_CONV), second_pass),
            pl.BlockSpec((sbp, 1, RW_SHIFT_W - PASS0_SHIFT), second_pass),
        ),
        scratch_shapes=[pltpu.VMEM((sbp, 2 * SUBLANES, DN_QKV + RW_SHIFT_W), F32)] if ntp > 1 else [],
        compiler_params=pltpu.CompilerParams(
            dimension_semantics=("arbitrary", "arbitrary"), vmem_limit_bytes=VMEM_LIMIT),
        name="proj",
    )(x2d, g_norm, w_main_t, w_ab_t, a_log, dt_bias, conv_w, mu, conv_buf8, prev8)
    conv_new = jnp.concatenate([cout0, cout1], axis=-1)
    shift_new = jnp.concatenate([shift0, shift1], axis=-1)
    return p, gb, conv_new, shift_new


def _dn_kernel(q_ref, k_ref, v_ref, z_ref, gb_ref, nw_ref, s0_ref, o_ref, sout_ref, s_scr, *, sb, tc, nc, nt):
    t = pl.program_id(1)
    ru = sb * tc
    hw = DN_HEADS * DN_D

    @pl.when(t == 0)
    def _():
        s_scr[...] = s0_ref[...]

    ri, ci = _iota2(ru, ru)
    same = _group(ri, tc) == _group(ci, tc)
    incl = same & (ri >= ci)
    eye = ri == ci
    rip, cc, same_p, first_cols, pdot = _column_packing(ru, tc)
    incl_p = same_p & (rip >= cc)
    strict_p = same_p & (rip > cc)
    seqs = [slice(s * tc, (s + 1) * tc) for s in range(sb)]
    units = [slice(c * ru, (c + 1) * ru) for c in range(nc)]

    def side_by_side(x0, x1):
        return jnp.concatenate([x0, x1], axis=1)

    def block_diag(x0, x1):
        z = jnp.zeros_like(x0)
        return jnp.concatenate([side_by_side(x0, z), side_by_side(z, x1)], axis=0)

    def unit_chain(heads, rows, q, k, v):
        per = []
        for h, kh in zip(heads, k):
            g_row = gb_ref[0, h:h + 1, :][:, rows]
            b_row = gb_ref[0, DN_HEADS + h:DN_HEADS + h + 1, :][:, rows]
            g_col = jnp.sum(jnp.where(incl, g_row, 0.0), axis=1, keepdims=True)
            g_cum_row = jnp.sum(jnp.where(eye, g_col, 0.0), axis=0, keepdims=True)
            g_tot = jnp.sum(jnp.where(same, g_row, 0.0), axis=1, keepdims=True)
            b_col = jnp.sum(jnp.where(eye, b_row, 0.0), axis=1, keepdims=True)
            per.append(dict(g_col=g_col, g_cum_row=g_cum_row, g_tot=g_tot, b_col=b_col,
                            e_g=jnp.exp(g_col), e_rest=jnp.exp(g_tot - g_col), kb=kh * b_col))
        g_cols = jnp.where(first_cols, per[0]["g_col"], per[1]["g_col"])
        g_rows = side_by_side(per[0]["g_cum_row"], per[1]["g_cum_row"])
        decay = jnp.where(incl_p, jnp.exp(jnp.minimum(g_cols - g_rows, 0.0)), 0.0)
        a_raw, qk_raw = _dot_rows([side_by_side(per[0]["kb"], per[1]["kb"]), side_by_side(q[0], q[1])],
                                  block_diag(k[0], k[1]), dot=_dot_nt)
        yield
        a = jnp.where(strict_p, a_raw * decay, 0.0)
        aqk = qk_raw * decay
        t_inv = yield from _unit_lower_inverse(a, rip, cc, tc, dot=pdot)
        rhs = [side_by_side(vh * w["b_col"], w["kb"] * w["e_g"]) for vh, w in zip(v, per)]
        sol = _dot(t_inv, block_diag(rhs[0], rhs[1]))
        yield
        out = []
        for i, w in enumerate(per):
            base = 2 * DN_D * i
            out.append(dict(u_all=sol[:, base:base + DN_D], w_all=sol[:, base + DN_D:base + 2 * DN_D],
                            qg=q[i] * w["e_g"], kd=k[i] * w["e_rest"], g_tot=w["g_tot"]))
        return aqk, out

    def pair_chain(pair):
        heads = (2 * pair, 2 * pair + 1)
        cols = [slice(h * DN_D, (h + 1) * DN_D) for h in heads]
        q = [q_ref[:, c] for c in cols]
        k = [k_ref[:, c] for c in cols]
        v = [v_ref[:, c] for c in cols]
        wy = yield from _lockstep(
            unit_chain(heads, rows, [x[rows] for x in q], [x[rows] for x in k], [x[rows] for x in v])
            for rows in units)
        for rows, (aqk, per) in zip(units, wy):
            ws = [[_dot(jnp.concatenate([w["w_all"][sq], w["qg"][sq]], axis=0), s_scr[s, h])
                   for s, sq in enumerate(seqs)] for h, w in zip(heads, per)]
            yield
            u = [_rows([w["u_all"][sq] - x[:tc] for sq, x in zip(seqs, wsh)]) for w, wsh in zip(per, ws)]
            q_s = [_rows([x[tc:] for x in wsh]) for wsh in ws]
            intra = _dot(aqk, block_diag(u[0], u[1]))
            upd = [[_dot_tn(w["kd"][sq], uh[sq]) for sq in seqs] for w, uh in zip(per, u)]
            yield
            for i, (h, w) in enumerate(zip(heads, per)):
                for s in range(sb):
                    gl = jnp.exp(w["g_tot"][s * tc:s * tc + 1, :])
                    s_scr[s, h] = s_scr[s, h] * gl + upd[i][s]
                o = q_s[i] + intra[:, i * DN_D:(i + 1) * DN_D]
                o = o * lax.rsqrt(jnp.mean(o * o, axis=-1, keepdims=True) + NORM_EPS) * nw_ref[...]
                o_ref[rows, cols[i]] = o * _silu(z_ref[rows, cols[i]])

    _run(_lockstep(pair_chain(pair) for pair in range(DN_HEADS // 2)))

    @pl.when(t == nt - 1)
    def _():
        sout_ref[...] = s_scr[...]


def _dn(p, gb, norm_w, s0, *, nb, seq, sb, tc, nc):
    nt = seq // (nc * tc)
    r = sb * nc * tc
    assert seq % (nc * tc) == 0 and nb % sb == 0 and (nt == 1 or sb == 1) and (nc == 1 or sb == 1)
    bcast = s0.shape[0] == 1
    assert not bcast or sb == 1
    hw = DN_HEADS * DN_D

    def st_idx(i):
        return 0 if bcast else i

    def rows(name):
        return pl.BlockSpec((r, hw), lambda i, t: (i * nt + t, COL[name] // hw))

    return pl.pallas_call(
        functools.partial(_dn_kernel, sb=sb, tc=tc, nc=nc, nt=nt),
        out_shape=(jax.ShapeDtypeStruct((nb * seq, hw), F32),
                   jax.ShapeDtypeStruct((nb, DN_HEADS, DN_D, DN_D), F32)),
        grid=(nb // sb, nt),
        in_specs=[
            rows("q"), rows("k"), rows("v"), rows("z"),
            pl.BlockSpec((1, 2 * DN_HEADS, r), lambda i, t: (i * nt + t, 0, 0)),
            pl.BlockSpec((1, DN_D), lambda i, t: (0, 0)),
            pl.BlockSpec((sb, DN_HEADS, DN_D, DN_D), lambda i, t: (st_idx(i), 0, 0, 0)),
        ],
        out_specs=(
            pl.BlockSpec((r, hw), lambda i, t: (i * nt + t, 0)),
            pl.BlockSpec((sb, DN_HEADS, DN_D, DN_D), lambda i, t: (i, 0, 0, 0)),
        ),
        scratch_shapes=[pltpu.VMEM((sb, DN_HEADS, DN_D, DN_D), F32)],
        compiler_params=pltpu.CompilerParams(
            dimension_semantics=("arbitrary", "arbitrary"), vmem_limit_bytes=VMEM_LIMIT),
        name="dn",
    )(p, p, p, p, gb, norm_w, s0)


def _rw_kernel(r_ref, k_ref, v_ref, l_ref, w0_ref, a0_ref, kkw_ref, kaw_ref, rkw_ref,
               gnw_ref, gnb_ref, w2_ref, a2_ref, g2_ref, s0_ref, odn_ref, ga_ref, gb_ref,
               o_ref, sout_ref, s_scr, *, sb, tc, nc, nt):
    t = pl.program_id(1)
    ru = sb * tc
    half = lax.broadcasted_iota(jnp.int32, (1, LANES), 1) < RW_HD
    ri128, ci128 = _iota2(LANES, LANES)
    bd_mask = _group(ri128, RW_HD) == _group(ci128, RW_HD)

    def seg_sum(x):
        first = jnp.sum(jnp.where(half, x, 0.0), axis=-1, keepdims=True)
        second = jnp.sum(jnp.where(half, 0.0, x), axis=-1, keepdims=True)
        return jnp.where(half, first, second)

    @pl.when(t == 0)
    def _():
        zero = jnp.zeros((RW_HD, RW_HD), F32)
        for s in range(sb):
            for p in range(RW_PAIRS):
                top = jnp.concatenate([s0_ref[s, 2 * p], zero], axis=1)
                bot = jnp.concatenate([zero, s0_ref[s, 2 * p + 1]], axis=1)
                s_scr[s, p] = jnp.concatenate([top, bot], axis=0)

    xm_l = l_ref[...]
    lora_w = jnp.tanh(xm_l).astype(BF16)
    lora_a = xm_l.astype(BF16)
    lora_g = _sigmoid(xm_l).astype(BF16)

    ri, ci = _iota2(ru, ru)
    cum_mask = jnp.where((_group(ri, tc) == _group(ci, tc)) & (ri >= ci), 1.0, 0.0).astype(BF16)
    rip, cc, same_p, _, pdot = _column_packing(ru, tc)
    incl_p = same_p & (rip >= cc)
    strict_p = same_p & (rip > cc)
    seqs = [slice(s * tc, (s + 1) * tc) for s in range(sb)]
    units = [slice(c * ru, (c + 1) * ru) for c in range(nc)]

    def stack(x):
        return jnp.concatenate([jnp.where(half, x, 0.0), jnp.where(half, 0.0, x)], axis=0)

    def unit_chain(rr, k2, vv, kk, bb, logw):
        lcum = _dot_mask(cum_mask, logw)
        yield
        ltot = jnp.broadcast_to(lcum.reshape(sb, tc, LANES)[:, tc - 1:tc, :],
                                (sb, tc, LANES)).reshape(ru, LANES)
        p_in = jnp.exp(lcum)
        p_inv = jnp.exp(-lcum)
        alpha = kk * jnp.exp(lcum - logw)
        beta = bb * p_inv
        kt = k2 * p_inv
        rt = rr * p_in
        e_rest = jnp.exp(ltot - lcum)
        beta2, kt2, v2 = stack(beta), stack(kt), stack(vv)
        l_raw, rb_raw = _dot_rows([alpha, rt], beta2, dot=_dot_nt)
        lk_raw, rk_raw = _dot_rows([alpha, rt], kt2, dot=_dot_nt)
        yield
        lmat = jnp.where(strict_p, l_raw, 0.0)
        rb = jnp.where(incl_p, rb_raw, 0.0)
        lk = jnp.where(strict_p, lk_raw, 0.0)
        rk = jnp.where(incl_p, rk_raw, 0.0)
        lkv, rkv = _dot_rows([lk, rk], v2)
        t_inv = yield from _unit_lower_inverse(lmat, rip, cc, tc, dot=pdot)
        sol = _dot(t_inv, jnp.concatenate([stack(alpha), stack(lkv)], axis=1))
        yield
        return dict(wa=sol[:, :LANES], uv=sol[:, LANES:], rt=rt, rb=rb, rkv=rkv, p_in=p_in,
                    k_end=k2 * e_rest, b_end=bb * e_rest)

    def pair_chain(p):
        c = slice(p * LANES, (p + 1) * LANES)
        rr = r_ref[:, c]
        xk = k_ref[:, c]
        vv = v_ref[:, c]
        lw = jnp.dot(lora_w, w2_ref[:, c], preferred_element_type=F32)
        la = jnp.dot(lora_a, a2_ref[:, c], preferred_element_type=F32)
        gate = jnp.dot(lora_g, g2_ref[:, c], preferred_element_type=F32)
        yield
        logw = -math.exp(-0.5) * _sigmoid(w0_ref[:, c] + lw)
        aa = _sigmoid(a0_ref[:, c] + la)
        kk = xk * kkw_ref[:, c]
        k2 = xk * (1.0 + (aa - 1.0) * kaw_ref[:, c])
        kk = kk * lax.rsqrt(seg_sum(kk * kk) + 1e-6)
        bb = kk * aa
        wy = yield from _lockstep(
            unit_chain(rr[rows], k2[rows], vv[rows], kk[rows], bb[rows], logw[rows]) for rows in units)
        for rows, w in zip(units, wy):
            st = [_dot_nt(jnp.concatenate([w["wa"][sq], w["rt"][sq]], axis=0), s_scr[s, p])
                  for s, sq in enumerate(seqs)]
            yield
            u = w["uv"] + _rows([x[:tc] for x in st])
            r_state = _rows([x[tc:] for x in st])
            rbu = _dot(w["rb"], stack(u))
            v_u = vv[rows]
            upd = [_dot_tn(jnp.concatenate([v_u[sq], -u[sq]], axis=0),
                           jnp.concatenate([w["k_end"][sq], w["b_end"][sq]], axis=0)) for sq in seqs]
            yield
            for s in range(sb):
                p_end = w["p_in"][(s + 1) * tc - 1:(s + 1) * tc, :]
                s_scr[s, p] = s_scr[s, p] * p_end + jnp.where(bd_mask, upd[s], 0.0)
            o = r_state + (w["rkv"] - rbu)
            mean = seg_sum(o) * (1.0 / RW_HD)
            d = o - mean
            var = seg_sum(d * d) * (1.0 / RW_HD)
            bonus = seg_sum(rr[rows] * k2[rows] * rkw_ref[:, c])
            on = d * lax.rsqrt(var + RW_GN_EPS) * gnw_ref[:, c] + gnb_ref[:, c]
            o_rw = (on + bonus * v_u) * gate[rows]
            mix = _sigmoid(ga_ref[rows, c]) * odn_ref[rows, c] + _sigmoid(gb_ref[rows, c]) * o_rw
            o_ref[rows, c] = mix.astype(BF16)

    _run(_lockstep(pair_chain(p) for p in range(RW_PAIRS)))

    @pl.when(t == nt - 1)
    def _():
        for s in range(sb):
            for p in range(RW_PAIRS):
                sp = s_scr[s, p]
                sout_ref[s, 2 * p] = sp[0:RW_HD, 0:RW_HD]
                sout_ref[s, 2 * p + 1] = sp[RW_HD:, RW_HD:]


def _rw(p, o_dn, weights, s0, *, nb, seq, sb, tc, nc):
    nt = seq // (nc * tc)
    r = sb * nc * tc
    assert seq % (nc * tc) == 0 and nb % sb == 0 and (nt == 1 or sb == 1) and (nc == 1 or sb == 1)
    bcast = s0.shape[0] == 1
    assert not bcast or sb == 1
    (w0, a0, kkw, kaw, rkw, gnw, gnb, w2p, a2p, g2p) = weights

    def st_idx(i):
        return 0 if bcast else i

    def vec(width):
        return pl.BlockSpec((1, width), lambda i, t: (0, 0))

    def lora_w():
        return pl.BlockSpec((RW_LORA, RW_W), lambda i, t: (0, 0))

    def rows(col):
        return pl.BlockSpec((r, RW_W), lambda i, t: (i * nt + t, col))

    def named(name):
        return rows(COL[name] // RW_W)

    return pl.pallas_call(
        functools.partial(_rw_kernel, sb=sb, tc=tc, nc=nc, nt=nt),
        out_shape=(jax.ShapeDtypeStruct((nb * seq, RW_W), BF16),
                   jax.ShapeDtypeStruct((nb, RW_HEADS, RW_HD, RW_HD), F32)),
        grid=(nb // sb, nt),
        in_specs=[
            named("r"), named("kx"), named("vx"),
            pl.BlockSpec((r, RW_LORA), lambda i, t: (i * nt + t, COL["lora"] // RW_LORA)),
            vec(RW_W), vec(RW_W), vec(RW_W), vec(RW_W), vec(RW_W), vec(RW_W), vec(RW_W),
            lora_w(), lora_w(), lora_w(),
            pl.BlockSpec((sb, RW_HEADS, RW_HD, RW_HD), lambda i, t: (st_idx(i), 0, 0, 0)),
            rows(0), named("ga"), named("gb"),
        ],
        out_specs=(
            rows(0),
            pl.BlockSpec((sb, RW_HEADS, RW_HD, RW_HD), lambda i, t: (i, 0, 0, 0)),
        ),
        scratch_shapes=[pltpu.VMEM((sb, RW_PAIRS, LANES, LANES), F32)],
        compiler_params=pltpu.CompilerParams(
            dimension_semantics=("arbitrary", "arbitrary"), vmem_limit_bytes=VMEM_LIMIT),
        name="rw",
    )(p, p, p, p, w0, a0, kkw, kaw, rkw, gnw, gnb, w2p, a2p, g2p, s0, o_dn, p, p)


def _merge_kernel(x_ref, mix_ref, wout_ref, gffn_ref, w1_ref, w2_ref, gfin_ref,
                  y_ref, x1_scr, h2_scr, acc_scr, *, nf):
    f = pl.program_id(1)

    @pl.when(f == 0)
    def _():
        x1 = x_ref[...] + jnp.dot(mix_ref[...], wout_ref[...], preferred_element_type=F32)
        x1_scr[...] = x1
        h2 = x1 * lax.rsqrt(jnp.mean(x1 * x1, axis=-1, keepdims=True) + NORM_EPS) * gffn_ref[...]
        h2_scr[...] = h2.astype(BF16)
        acc_scr[...] = jnp.zeros_like(acc_scr)

    hid = jnp.dot(h2_scr[...], w1_ref[...], preferred_element_type=F32)
    hid = jnp.square(jnp.maximum(hid, 0.0))
    acc_scr[...] += jnp.dot(hid.astype(BF16), w2_ref[...], preferred_element_type=F32)

    @pl.when(f == nf - 1)
    def _():
        y = x1_scr[...] + acc_scr[...]
        y_ref[...] = y * lax.rsqrt(jnp.mean(y * y, axis=-1, keepdims=True) + NORM_EPS) * gfin_ref[...]


def _merge(x2d, mix, w_out, g_ffn, w1, w2, g_fin, *, tm, tf):
    n = x2d.shape[0]
    nf = D_FF // tf
    assert n % tm == 0 and D_FF % tf == 0
    row = lambda i, f: (i, 0)
    return pl.pallas_call(
        functools.partial(_merge_kernel, nf=nf),
        out_shape=jax.ShapeDtypeStruct((n, D_MODEL), F32),
        grid=(n // tm, nf),
        in_specs=[
            pl.BlockSpec((tm, D_MODEL), row),
            pl.BlockSpec((tm, D_MODEL), row),
            pl.BlockSpec((D_MODEL, D_MODEL), lambda i, f: (0, 0)),
            pl.BlockSpec((1, D_MODEL), lambda i, f: (0, 0)),
            pl.BlockSpec((D_MODEL, tf), lambda i, f: (0, f)),
            pl.BlockSpec((tf, D_MODEL), lambda i, f: (f, 0)),
            pl.BlockSpec((1, D_MODEL), lambda i, f: (0, 0)),
        ],
        out_specs=pl.BlockSpec((tm, D_MODEL), row),
        scratch_shapes=[
            pltpu.VMEM((tm, D_MODEL), F32),
            pltpu.VMEM((tm, D_MODEL), BF16),
            pltpu.VMEM((tm, D_MODEL), F32),
        ],
        compiler_params=pltpu.CompilerParams(
            dimension_semantics=("arbitrary", "arbitrary"), vmem_limit_bytes=VMEM_LIMIT),
        name="merge",
    )(x2d, mix, w_out, g_ffn, w1, w2, g_fin)


def _pad_rows_front(x, rows):
    b, n, w = x.shape
    return jnp.concatenate([jnp.zeros((b, rows - n, w), x.dtype), x], axis=1)


def _layer(x, conv_buf, dn_s, rw_prev, rw_s, wts, *, sb, tc, nc, tm, tm_mlp):
    nb, seq, _ = x.shape
    n = nb * seq
    x2d = x.reshape(n, D_MODEL)
    p, gb, conv_new, shift_new = _proj(
        x2d, wts["g_mix"], wts["w_main_t"], wts["w_ab_t"], wts["a_log"], wts["dt_bias"], wts["conv_w"],
        wts["rw_mu"], _pad_rows_front(conv_buf, SUBLANES), _pad_rows_front(rw_prev[:, None, :], SUBLANES),
        nb=nb, seq=seq, tm=tm, rc=sb * nc * tc)
    o_dn, dn_new = _dn(p, gb, wts["dn_norm_w"], dn_s, nb=nb, seq=seq, sb=sb, tc=tc, nc=nc)
    mix, rw_new = _rw(p, o_dn, wts["rw"], rw_s, nb=nb, seq=seq, sb=sb, tc=tc, nc=nc)
    y = _merge(x2d, mix, wts["w_out"], wts["g_ffn"], wts["w_ff1"], wts["w_ff2"], wts["g_final"],
               tm=tm_mlp, tf=FF_TILE)
    return y.reshape(nb, seq, D_MODEL), conv_new, dn_new, shift_new[:, 0], rw_new


def kernel(x_prompt, x_sample, state_dn_conv, state_dn, state_rw_shift, state_rw, meta_tokens, g_mix_norm, w_in, dn_conv_w, dn_a_log, dn_dt_bias, dn_norm_w, rw_mu, rw_w0, rw_w2, rw_a0, rw_a2, rw_g2, rw_k_k, rw_k_a, rw_r_k, rw_gn_w, rw_gn_b, w_out, g_ffn_norm, w_ff1, w_ff2, g_final):
    assert g_mix_norm.shape[0] == 1, "single layer"
    w_t = w_in[0].T
    o_a = DN_QKV + D_MODEL
    o_rwp = o_a + 2 * DN_HEADS
    o_gate = o_rwp + RW_SHIFT_W
    src = {"q": 0, "k": D_MODEL, "v": 2 * D_MODEL, "z": DN_QKV, "r": o_rwp, "kx": o_rwp + RW_W,
           "vx": o_rwp + 2 * RW_W, "lora": o_rwp + 3 * RW_W, "ga": o_gate, "gb": o_gate + D_MODEL}
    w_main_t = jnp.concatenate([w_t[src[name]:src[name] + width] for name, width, _, _ in SEGMENTS],
                               axis=0).astype(BF16)
    w_ab_t = w_t[o_a:o_rwp].astype(BF16)

    def lora_pad(wl, lo):
        return jnp.zeros((RW_LORA, RW_W), F32).at[lo:lo + wl.shape[0]].set(wl).astype(BF16)

    row = lambda v: v.reshape(1, -1).astype(F32)
    wts = {
        "g_mix": row(g_mix_norm[0]), "w_main_t": w_main_t, "w_ab_t": w_ab_t,
        "a_log": dn_a_log[0].reshape(DN_HEADS, 1), "dt_bias": dn_dt_bias[0].reshape(DN_HEADS, 1),
        "conv_w": dn_conv_w[0], "dn_norm_w": row(dn_norm_w[0]), "rw_mu": row(rw_mu[0]),
        "rw": (row(rw_w0[0]), row(rw_a0[0]), row(rw_k_k[0]), row(rw_k_a[0]), row(rw_r_k[0]),
               row(rw_gn_w[0]), row(rw_gn_b[0]),
               lora_pad(rw_w2[0], 0), lora_pad(rw_a2[0], 64), lora_pad(rw_g2[0], 128)),
        "w_out": w_out[0].astype(BF16), "g_ffn": row(g_ffn_norm[0]),
        "w_ff1": w_ff1[0].astype(BF16), "w_ff2": w_ff2[0].astype(BF16), "g_final": row(g_final),
    }

    nbm = UNIT_ROWS // N_META
    xm = jnp.broadcast_to(meta_tokens.astype(F32)[None], (nbm, N_META, D_MODEL))
    _, conv_m, dn_m, shift_m, rw_m = _layer(
        xm, jnp.zeros((nbm, 3, DN_QKV), F32), jnp.zeros((nbm, DN_HEADS, DN_D, DN_D), F32),
        jnp.zeros((nbm, RW_SHIFT_W), F32), jnp.zeros((nbm, RW_HEADS, RW_HD, RW_HD), F32), wts,
        sb=nbm, tc=N_META, nc=1, tm=nbm * N_META, tm_mlp=nbm * N_META)

    y_p, conv_p, dn_p, shift_p, rw_p = _layer(
        x_prompt, conv_m[:1], dn_m[:1], shift_m[:1], rw_m[:1], wts,
        sb=1, tc=CHUNK, nc=CHUNKS_PER_STEP, tm=PROJ_TILE, tm_mlp=MLP_TILE)

    dec_len = x_sample.shape[1]
    y_s, conv_s, dn_s, shift_s, rw_s = _layer(
        x_sample, state_dn_conv[0], state_dn[0], state_rw_shift[0], state_rw[0], wts,
        sb=UNIT_ROWS // dec_len, tc=dec_len, nc=1, tm=PROJ_TILE_SHORT, tm_mlp=MLP_TILE)

    return (y_p, y_s, conv_p[None], dn_p[None], shift_p[None], rw_p[None],
            conv_s[None], dn_s[None], shift_s[None], rw_s[None])
```

```python
import functools
import math

import jax
import jax.numpy as jnp
from jax import lax
from jax.experimental import pallas as pl
from jax.experimental.pallas import tpu as pltpu

F32 = jnp.float32
BF16 = jnp.bfloat16

D_MODEL = 1024
N_META = 16
DN_HEADS = 8
DN_D = 128
DN_QKV = 3 * DN_HEADS * DN_D
RW_HEADS = 16
RW_HD = 64
RW_PAIRS = RW_HEADS // 2
RW_W = RW_HEADS * RW_HD
RW_LORA = 256
RW_SHIFT_W = 3 * RW_W + RW_LORA
D_FF = 4 * D_MODEL
NORM_EPS = 1e-6
RW_GN_EPS = 64e-5
LANES = 128
SUBLANES = 8
INV_BLOCK = 16
CHUNK = 64
UNIT_ROWS = 64
CHUNKS_PER_STEP = 8
PROJ_TILE = 512
PROJ_TILE_SHORT = 256
MLP_TILE = 1024
FF_TILE = 1024

SEGMENTS = (
    ("q", D_MODEL, "conv", 0), ("k", D_MODEL, "conv", D_MODEL),
    ("r", RW_W, "shift", 0), ("kx", RW_W, "shift", RW_W),
    ("ga", D_MODEL, "raw", 0), ("gb", D_MODEL, "raw", 0),
    ("v", D_MODEL, "conv", 2 * D_MODEL), ("z", D_MODEL, "raw", 0),
    ("vx", RW_W, "shift", 2 * RW_W), ("lora", RW_LORA, "shift", 3 * RW_W),
)
COL = {}
P_W = 0
for _name, _width, _, _ in SEGMENTS:
    COL[_name] = P_W
    P_W += _width
P_HALF = P_W // 2
PASS0_CONV = 2 * D_MODEL
PASS0_SHIFT = 2 * RW_W
PIECE_W = 512


def _chunk_plan(half):
    lo, hi = half * P_HALF, (half + 1) * P_HALF
    plan = []
    for name, width, kind, off in SEGMENTS:
        start = COL[name]
        c = max(start, lo)
        while c < min(start + width, hi):
            w = min(PIECE_W, min(start + width, hi) - c)
            plan.append((c - lo, w, kind, off + (c - start), name))
            c += w
    return plan

VMEM_LIMIT = 56 * 1024 * 1024


def _sigmoid(x):
    return 1.0 / (1.0 + jnp.exp(-x))


def _silu(x):
    return x * _sigmoid(x)


def _softplus(x):
    return jnp.maximum(x, 0.0) + jnp.log(1.0 + jnp.exp(-jnp.abs(x)))


def _dot(a, b):
    return jnp.dot(a.astype(BF16), b.astype(BF16), preferred_element_type=F32)


def _dot_nt(a, b):
    return lax.dot_general(a.astype(BF16), b.astype(BF16), (((1,), (1,)), ((), ())),
                           preferred_element_type=F32)


def _dot_tn(a, b):
    return lax.dot_general(a.astype(BF16), b.astype(BF16), (((0,), (0,)), ((), ())),
                           preferred_element_type=F32)


def _rows(xs):
    return xs[0] if len(xs) == 1 else jnp.concatenate(xs, axis=0)


def _dot_rows(xs, b, dot=_dot):
    out = dot(_rows(xs), b)
    off, parts = 0, []
    for x in xs:
        parts.append(out[off:off + x.shape[0]])
        off += x.shape[0]
    return parts


def _dot_mask(mask_bf16, x):
    h1 = x.astype(BF16)
    r1 = x - h1.astype(F32)
    h2 = r1.astype(BF16)
    h3 = (r1 - h2.astype(F32)).astype(BF16)
    return (jnp.dot(mask_bf16, h1, preferred_element_type=F32)
            + (jnp.dot(mask_bf16, h2, preferred_element_type=F32)
               + jnp.dot(mask_bf16, h3, preferred_element_type=F32)))


def _iota2(n, m):
    return (lax.broadcasted_iota(jnp.int32, (n, m), 0),
            lax.broadcasted_iota(jnp.int32, (n, m), 1))


def _group(idx, size):
    return lax.shift_right_logical(idx, int(math.log2(size)))


def _column_packing(ru, tc):
    ri, ci = _iota2(ru, 2 * ru)
    cc = ci & (ru - 1)
    same = _group(ri, tc) == _group(cc, tc)
    first = lax.broadcasted_iota(jnp.int32, (1, 2 * ru), 1) < ru

    def pdot(x, y):
        y_bd = jnp.concatenate([jnp.where(first, y, 0.0), jnp.where(first, 0.0, y)], axis=0)
        return _dot(x, y_bd)

    return ri, cc, same, first, pdot


def _lockstep(chains):
    chains = list(chains)
    results = [None] * len(chains)
    active = list(range(len(chains)))
    while active:
        for i in list(active):
            try:
                next(chains[i])
            except StopIteration as done:
                results[i] = done.value
                active.remove(i)
        if active:
            yield
    return results


def _run(chain):
    for _ in chain:
        pass


def _neumann_inverse(a, eye, nil, dot):
    inv = eye - a
    if nil <= 2:
        return inv
    power = dot(a, a)
    yield
    k = 4
    while k < nil:
        step, power = _dot_rows([inv, power], power, dot=dot)
        yield
        inv = inv + step
        k *= 2
    step = dot(inv, power)
    yield
    return inv + step


def _unit_lower_inverse(a, ri, ci, tc, dot=_dot):
    eye = jnp.where(ri == ci, 1.0, 0.0).astype(F32)
    if tc <= INV_BLOCK:
        return (yield from _neumann_inverse(a, eye, tc, dot))
    assert tc // INV_BLOCK <= 4
    diag = _group(ri, INV_BLOCK) == _group(ci, INV_BLOCK)
    d = jnp.where(diag, a, 0.0)
    low = a - d
    dinv = yield from _neumann_inverse(d, eye, INV_BLOCK, dot)
    b = dot(low, dinv)
    yield
    db, bb = _dot_rows([dinv, b], b, dot=dot)
    yield
    p1 = dinv - db
    tail = dot(p1, bb)
    yield
    return p1 + tail


def _proj_kernel(x_ref, g_ref, wt_ref, wab_ref, alog_ref, dtb_ref, cw_ref, mu_ref, cb_ref, prev_ref,
                 p_ref, gb_ref, cout0_ref, shift0_ref, cout1_ref, shift1_ref, *maybe_hist,
                 tm, rc, sbp, ttp, ntp):
    carry = ntp > 1
    hist = maybe_hist[0] if carry else None
    j = pl.program_id(0)
    i = pl.program_id(1)
    tpos = lax.rem(i, ntp)
    x = x_ref[...]
    h = x * lax.rsqrt(jnp.mean(x * x, axis=-1, keepdims=True) + NORM_EPS) * g_ref[...]
    hb = h.astype(BF16)
    nt_dims = (((1,), (1,)), ((), ()))

    def mm(c0, cw):
        return lax.dot_general(hb, wt_ref[c0:c0 + cw, :], nt_dims, preferred_element_type=F32)

    seq_pos = lax.broadcasted_iota(jnp.int32, (tm, 1), 0) & (ttp - 1)

    def shift_rows(a, first):
        rolled = pltpu.roll(a, len(first), axis=0)
        if sbp == 1:
            head = rolled[0:SUBLANES]
            for i, f in enumerate(first):
                head = jnp.where(seq_pos[0:SUBLANES] == i, f[0], head)
            return jnp.concatenate([head, rolled[SUBLANES:]], axis=0)
        for i, f in enumerate(first):
            fill = jnp.broadcast_to(f, (sbp, ttp, a.shape[1])).reshape(tm, a.shape[1])
            rolled = jnp.where(seq_pos == i, fill, rolled)
        return rolled

    def run_pass(plan):
        if carry:
            @pl.when(tpos == 0)
            def _():
                hist[:, 0:SUBLANES, 0:DN_QKV] = cb_ref[...]
                hist[:, 0:SUBLANES, DN_QKV:] = prev_ref[...]

            @pl.when(tpos > 0)
            def _():
                hist[:, 0:SUBLANES, :] = hist[:, SUBLANES:, :]

        for c0, cw, kind, off, name in plan:
            cur = mm(c0, cw)
            if kind == "raw":
                p_ref[:, c0:c0 + cw] = cur
                continue
            cur3 = cur.reshape(sbp, ttp, cw)
            if carry:
                hc = slice(off, off + cw) if kind == "conv" else slice(DN_QKV + off, DN_QKV + off + cw)
                old = hist[:, 0:SUBLANES, hc]
                hist[:, SUBLANES:, hc] = cur3[:, ttp - SUBLANES:, :]
            elif kind == "conv":
                old = cb_ref[:, :, off:off + cw]
                out_ref, base = (cout0_ref, 0) if off < PASS0_CONV else (cout1_ref, PASS0_CONV)
                out_ref[:, :, off - base:off - base + cw] = cur3[:, ttp - 3:, :]
            else:
                old = prev_ref[:, :, off:off + cw]
                out_ref, base = (shift0_ref, 0) if off < PASS0_SHIFT else (shift1_ref, PASS0_SHIFT)
                out_ref[:, :, off - base:off - base + cw] = cur3[:, ttp - 1:, :]
            x1 = old[:, 7:8, :]
            if kind == "shift":
                prev = shift_rows(cur, [x1])
                p_ref[:, c0:c0 + cw] = cur + (prev - cur) * mu_ref[:, off:off + cw]
                continue
            w = [cw_ref[tap:tap + 1, off:off + cw] for tap in range(4)]
            x2, x3 = old[:, 6:7, :], old[:, 5:6, :]
            y1 = shift_rows(cur, [x1])
            far = cur * w[1] + y1 * w[0]
            acc = cur * w[3] + y1 * w[2] + shift_rows(far, [x2 * w[1] + x3 * w[0], x1 * w[1] + x2 * w[0]])
            y = _silu(acc)
            if name == "v":
                p_ref[:, c0:c0 + cw] = y
            else:
                scale = DN_D ** -0.5 if name == "q" else 1.0
                for c1 in range(0, cw, DN_D):
                    yh = y[:, c1:c1 + DN_D]
                    yh = yh * (lax.rsqrt(jnp.sum(yh * yh, axis=-1, keepdims=True) + 1e-6) * scale)
                    p_ref[:, c0 + c1:c0 + c1 + DN_D] = yh

    @pl.when(j == 0)
    def _():
        run_pass(_chunk_plan(0))
        for c in range(tm // rc):
            ab = lax.dot_general(wab_ref[...], hb[c * rc:(c + 1) * rc], nt_dims,
                                 preferred_element_type=F32)
            g = -jnp.exp(alog_ref[...]) * _softplus(ab[0:DN_HEADS] + dtb_ref[...])
            gb_ref[c, 0:DN_HEADS, :] = g
            gb_ref[c, DN_HEADS:2 * DN_HEADS, :] = _sigmoid(ab[DN_HEADS:2 * DN_HEADS])

        if carry:
            @pl.when(tpos == ntp - 1)
            def _():
                cout0_ref[...] = hist[:, 2 * SUBLANES - 3:, 0:PASS0_CONV]
                shift0_ref[...] = hist[:, 2 * SUBLANES - 1:, DN_QKV:DN_QKV + PASS0_SHIFT]

    @pl.when(j == 1)
    def _():
        run_pass(_chunk_plan(1))

        if carry:
            @pl.when(tpos == ntp - 1)
            def _():
                cout1_ref[...] = hist[:, 2 * SUBLANES - 3:, PASS0_CONV:DN_QKV]
                shift1_ref[...] = hist[:, 2 * SUBLANES - 1:, DN_QKV + PASS0_SHIFT:]


def _proj(x2d, g_norm, w_main_t, w_ab_t, a_log, dt_bias, conv_w, mu, conv_buf8, prev8, *, nb, seq, tm, rc):
    n = nb * seq
    ttp = min(seq, tm)
    sbp = tm // ttp
    ntp = seq // ttp
    assert n % tm == 0 and tm % rc == 0 and tm % ttp == 0 and seq % ttp == 0
    bcast = conv_buf8.shape[0] == 1
    assert bcast == (prev8.shape[0] == 1) and (not bcast or sbp == 1)
    last = (n // tm - 1) // ntp

    def st_idx(i):
        return 0 if bcast else i // ntp

    assert ttp % SUBLANES == 0 and ttp & (ttp - 1) == 0
    first_pass = lambda j, i: ((i // ntp) * (1 - j) + last * j, 0, 0)
    second_pass = lambda j, i: ((i // ntp) * j, 0, 0)
    p, gb, cout0, shift0, cout1, shift1 = pl.pallas_call(
        functools.partial(_proj_kernel, tm=tm, rc=rc, sbp=sbp, ttp=ttp, ntp=ntp),
        out_shape=(jax.ShapeDtypeStruct((n, P_W), F32),
                   jax.ShapeDtypeStruct((n // rc, 2 * DN_HEADS, rc), F32),
                   jax.ShapeDtypeStruct((nb, 3, PASS0_CONV), F32),
                   jax.ShapeDtypeStruct((nb, 1, PASS0_SHIFT), F32),
                   jax.ShapeDtypeStruct((nb, 3, DN_QKV - PASS0_CONV), F32),
                   jax.ShapeDtypeStruct((nb, 1, RW_SHIFT_W - PASS0_SHIFT), F32)),
        grid=(2, n // tm),
        in_specs=[
            pl.BlockSpec((tm, D_MODEL), lambda j, i: (i, 0)),
            pl.BlockSpec((1, D_MODEL), lambda j, i: (0, 0)),
            pl.BlockSpec((P_HALF, D_MODEL), lambda j, i: (j, 0)),
            pl.BlockSpec((2 * DN_HEADS, D_MODEL), lambda j, i: (0, 0)),
            pl.BlockSpec((DN_HEADS, 1), lambda j, i: (0, 0)),
            pl.BlockSpec((DN_HEADS, 1), lambda j, i: (0, 0)),
            pl.BlockSpec((4, DN_QKV), lambda j, i: (0, 0)),
            pl.BlockSpec((1, RW_SHIFT_W), lambda j, i: (0, 0)),
            pl.BlockSpec((sbp, SUBLANES, DN_QKV), lambda j, i: (st_idx(i), 0, 0)),
            pl.BlockSpec((sbp, SUBLANES, RW_SHIFT_W), lambda j, i: (st_idx(i), 0, 0)),
        ],
        out_specs=(
            pl.BlockSpec((tm, P_HALF), lambda j, i: (i, j)),
            pl.BlockSpec((tm // rc, 2 * DN_HEADS, rc),
                         lambda j, i: (i * (1 - j) + (n // tm - 1) * j, 0, 0)),
            pl.BlockSpec((sbp, 3, PASS0_CONV), first_pass),
            pl.BlockSpec((sbp, 1, PASS0_SHIFT), first_pass),
            pl.BlockSpec((sbp, 3, DN_QKV - PASS0_CONV), second_pass),
            pl.BlockSpec((sbp, 1, RW_SHIFT_W - PASS0_SHIFT), second_pass),
        ),
        scratch_shapes=[pltpu.VMEM((sbp, 2 * SUBLANES, DN_QKV + RW_SHIFT_W), F32)] if ntp > 1 else [],
        compiler_params=pltpu.CompilerParams(
            dimension_semantics=("arbitrary", "arbitrary"), vmem_limit_bytes=VMEM_LIMIT),
        name="proj",
    )(x2d, g_norm, w_main_t, w_ab_t, a_log, dt_bias, conv_w, mu, conv_buf8, prev8)
    conv_new = jnp.concatenate([cout0, cout1], axis=-1)
    shift_new = jnp.concatenate([shift0, shift1], axis=-1)
    return p, gb, conv_new, shift_new


def _dn_kernel(q_ref, k_ref, v_ref, z_ref, gb_ref, nw_ref, s0_ref, o_ref, sout_ref, s_scr, *, sb, tc, nc, nt):
    t = pl.program_id(1)
    ru = sb * tc
    hw = DN_HEADS * DN_D

    @pl.when(t == 0)
    def _():
        s_scr[...] = s0_ref[...]

    ri, ci = _iota2(ru, ru)
    same = _group(ri, tc) == _group(ci, tc)
    incl = same & (ri >= ci)
    eye = ri == ci
    rip, cc, same_p, first_cols, pdot = _column_packing(ru, tc)
    incl_p = same_p & (rip >= cc)
    strict_p = same_p & (rip > cc)
    seqs = [slice(s * tc, (s + 1) * tc) for s in range(sb)]
    units = [slice(c * ru, (c + 1) * ru) for c in range(nc)]

    def side_by_side(x0, x1):
        return jnp.concatenate([x0, x1], axis=1)

    def block_diag(x0, x1):
        z = jnp.zeros_like(x0)
        return jnp.concatenate([side_by_side(x0, z), side_by_side(z, x1)], axis=0)

    def unit_chain(heads, rows, q, k, v):
        per = []
        for h, kh in zip(heads, k):
            g_row = gb_ref[0, h:h + 1, :][:, rows]
            b_row = gb_ref[0, DN_HEADS + h:DN_HEADS + h + 1, :][:, rows]
            g_col = jnp.sum(jnp.where(incl, g_row, 0.0), axis=1, keepdims=True)
            g_cum_row = jnp.sum(jnp.where(eye, g_col, 0.0), axis=0, keepdims=True)
            g_tot = jnp.sum(jnp.where(same, g_row, 0.0), axis=1, keepdims=True)
            b_col = jnp.sum(jnp.where(eye, b_row, 0.0), axis=1, keepdims=True)
            per.append(dict(g_col=g_col, g_cum_row=g_cum_row, g_tot=g_tot, b_col=b_col,
                            e_g=jnp.exp(g_col), e_rest=jnp.exp(g_tot - g_col), kb=kh * b_col))
        g_cols = jnp.where(first_cols, per[0]["g_col"], per[1]["g_col"])
        g_rows = side_by_side(per[0]["g_cum_row"], per[1]["g_cum_row"])
        decay = jnp.where(incl_p, jnp.exp(jnp.minimum(g_cols - g_rows, 0.0)), 0.0)
        a_raw, qk_raw = _dot_rows([side_by_side(per[0]["kb"], per[1]["kb"]), side_by_side(q[0], q[1])],
                                  block_diag(k[0], k[1]), dot=_dot_nt)
        yield
        a = jnp.where(strict_p, a_raw * decay, 0.0)
        aqk = qk_raw * decay
        t_inv = yield from _unit_lower_inverse(a, rip, cc, tc, dot=pdot)
        rhs = [side_by_side(vh * w["b_col"], w["kb"] * w["e_g"]) for vh, w in zip(v, per)]
        sol = _dot(t_inv, block_diag(rhs[0], rhs[1]))
        yield
        out = []
        for i, w in enumerate(per):
            base = 2 * DN_D * i
            out.append(dict(u_all=sol[:, base:base + DN_D], w_all=sol[:, base + DN_D:base + 2 * DN_D],
                            qg=q[i] * w["e_g"], kd=k[i] * w["e_rest"], g_tot=w["g_tot"]))
        return aqk, out

    def pair_chain(pair):
        heads = (2 * pair, 2 * pair + 1)
        cols = [slice(h * DN_D, (h + 1) * DN_D) for h in heads]
        q = [q_ref[:, c] for c in cols]
        k = [k_ref[:, c] for c in cols]
        v = [v_ref[:, c] for c in cols]
        wy = yield from _lockstep(
            unit_chain(heads, rows, [x[rows] for x in q], [x[rows] for x in k], [x[rows] for x in v])
            for rows in units)
        for rows, (aqk, per) in zip(units, wy):
            ws = [[_dot(jnp.concatenate([w["w_all"][sq], w["qg"][sq]], axis=0), s_scr[s, h])
                   for s, sq in enumerate(seqs)] for h, w in zip(heads, per)]
            yield
            u = [_rows([w["u_all"][sq] - x[:tc] for sq, x in zip(seqs, wsh)]) for w, wsh in zip(per, ws)]
            q_s = [_rows([x[tc:] for x in wsh]) for wsh in ws]
            intra = _dot(aqk, block_diag(u[0], u[1]))
            upd = [[_dot_tn(w["kd"][sq], uh[sq]) for sq in seqs] for w, uh in zip(per, u)]
            yield
            for i, (h, w) in enumerate(zip(heads, per)):
                for s in range(sb):
                    gl = jnp.exp(w["g_tot"][s * tc:s * tc + 1, :])
                    s_scr[s, h] = s_scr[s, h] * gl + upd[i][s]
                o = q_s[i] + intra[:, i * DN_D:(i + 1) * DN_D]
                o = o * lax.rsqrt(jnp.mean(o * o, axis=-1, keepdims=True) + NORM_EPS) * nw_ref[...]
                o_ref[rows, cols[i]] = o * _silu(z_ref[rows, cols[i]])

    _run(_lockstep(pair_chain(pair) for pair in range(DN_HEADS // 2)))

    @pl.when(t == nt - 1)
    def _():
        sout_ref[...] = s_scr[...]


def _dn(p, gb, norm_w, s0, *, nb, seq, sb, tc, nc):
    nt = seq // (nc * tc)
    r = sb * nc * tc
    assert seq % (nc * tc) == 0 and nb % sb == 0 and (nt == 1 or sb == 1) and (nc == 1 or sb == 1)
    bcast = s0.shape[0] == 1
    assert not bcast or sb == 1
    hw = DN_HEADS * DN_D

    def st_idx(i):
        return 0 if bcast else i

    def rows(name):
        return pl.BlockSpec((r, hw), lambda i, t: (i * nt + t, COL[name] // hw))

    return pl.pallas_call(
        functools.partial(_dn_kernel, sb=sb, tc=tc, nc=nc, nt=nt),
        out_shape=(jax.ShapeDtypeStruct((nb * seq, hw), F32),
                   jax.ShapeDtypeStruct((nb, DN_HEADS, DN_D, DN_D), F32)),
        grid=(nb // sb, nt),
        in_specs=[
            rows("q"), rows("k"), rows("v"), rows("z"),
            pl.BlockSpec((1, 2 * DN_HEADS, r), lambda i, t: (i * nt + t, 0, 0)),
            pl.BlockSpec((1, DN_D), lambda i, t: (0, 0)),
            pl.BlockSpec((sb, DN_HEADS, DN_D, DN_D), lambda i, t: (st_idx(i), 0, 0, 0)),
        ],
        out_specs=(
            pl.BlockSpec((r, hw), lambda i, t: (i * nt + t, 0)),
            pl.BlockSpec((sb, DN_HEADS, DN_D, DN_D), lambda i, t: (i, 0, 0, 0)),
        ),
        scratch_shapes=[pltpu.VMEM((sb, DN_HEADS, DN_D, DN_D), F32)],
        compiler_params=pltpu.CompilerParams(
            dimension_semantics=("arbitrary", "arbitrary"), vmem_limit_bytes=VMEM_LIMIT),
        name="dn",
    )(p, p, p, p, gb, norm_w, s0)


def _rw_kernel(r_ref, k_ref, v_ref, l_ref, w0_ref, a0_ref, kkw_ref, kaw_ref, rkw_ref,
               gnw_ref, gnb_ref, w2_ref, a2_ref, g2_ref, s0_ref, odn_ref, ga_ref, gb_ref,
               o_ref, sout_ref, s_scr, *, sb, tc, nc, nt):
    t = pl.program_id(1)
    ru = sb * tc
    half = lax.broadcasted_iota(jnp.int32, (1, LANES), 1) < RW_HD
    ri128, ci128 = _iota2(LANES, LANES)
    bd_mask = _group(ri128, RW_HD) == _group(ci128, RW_HD)

    def seg_sum(x):
        first = jnp.sum(jnp.where(half, x, 0.0), axis=-1, keepdims=True)
        second = jnp.sum(jnp.where(half, 0.0, x), axis=-1, keepdims=True)
        return jnp.where(half, first, second)

    @pl.when(t == 0)
    def _():
        zero = jnp.zeros((RW_HD, RW_HD), F32)
        for s in range(sb):
            for p in range(RW_PAIRS):
                top = jnp.concatenate([s0_ref[s, 2 * p], zero], axis=1)
                bot = jnp.concatenate([zero, s0_ref[s, 2 * p + 1]], axis=1)
                s_scr[s, p] = jnp.concatenate([top, bot], axis=0)

    xm_l = l_ref[...]
    lora_w = jnp.tanh(xm_l).astype(BF16)
    lora_a = xm_l.astype(BF16)
    lora_g = _sigmoid(xm_l).astype(BF16)

    ri, ci = _iota2(ru, ru)
    cum_mask = jnp.where((_group(ri, tc) == _group(ci, tc)) & (ri >= ci), 1.0, 0.0).astype(BF16)
    rip, cc, same_p, _, pdot = _column_packing(ru, tc)
    incl_p = same_p & (rip >= cc)
    strict_p = same_p & (rip > cc)
    seqs = [slice(s * tc, (s + 1) * tc) for s in range(sb)]
    units = [slice(c * ru, (c + 1) * ru) for c in range(nc)]

    def stack(x):
        return jnp.concatenate([jnp.where(half, x, 0.0), jnp.where(half, 0.0, x)], axis=0)

    def unit_chain(rr, k2, vv, kk, bb, logw):
        lcum = _dot_mask(cum_mask, logw)
        yield
        ltot = jnp.broadcast_to(lcum.reshape(sb, tc, LANES)[:, tc - 1:tc, :],
                                (sb, tc, LANES)).reshape(ru, LANES)
        p_in = jnp.exp(lcum)
        p_inv = jnp.exp(-lcum)
        alpha = kk * jnp.exp(lcum - logw)
        beta = bb * p_inv
        kt = k2 * p_inv
        rt = rr * p_in
        e_rest = jnp.exp(ltot - lcum)
        beta2, kt2, v2 = stack(beta), stack(kt), stack(vv)
        l_raw, rb_raw = _dot_rows([alpha, rt], beta2, dot=_dot_nt)
        lk_raw, rk_raw = _dot_rows([alpha, rt], kt2, dot=_dot_nt)
        yield
        lmat = jnp.where(strict_p, l_raw, 0.0)
        rb = jnp.where(incl_p, rb_raw, 0.0)
        lk = jnp.where(strict_p, lk_raw, 0.0)
        rk = jnp.where(incl_p, rk_raw, 0.0)
        lkv, rkv = _dot_rows([lk, rk], v2)
        t_inv = yield from _unit_lower_inverse(lmat, rip, cc, tc, dot=pdot)
        sol = _dot(t_inv, jnp.concatenate([stack(alpha), stack(lkv)], axis=1))
        yield
        return dict(wa=sol[:, :LANES], uv=sol[:, LANES:], rt=rt, rb=rb, rkv=rkv, p_in=p_in,
                    k_end=k2 * e_rest, b_end=bb * e_rest)

    def pair_chain(p):
        c = slice(p * LANES, (p + 1) * LANES)
        rr = r_ref[:, c]
        xk = k_ref[:, c]
        vv = v_ref[:, c]
        lw = jnp.dot(lora_w, w2_ref[:, c], preferred_element_type=F32)
        la = jnp.dot(lora_a, a2_ref[:, c], preferred_element_type=F32)
        gate = jnp.dot(lora_g, g2_ref[:, c], preferred_element_type=F32)
        yield
        logw = -math.exp(-0.5) * _sigmoid(w0_ref[:, c] + lw)
        aa = _sigmoid(a0_ref[:, c] + la)
        kk = xk * kkw_ref[:, c]
        k2 = xk * (1.0 + (aa - 1.0) * kaw_ref[:, c])
        kk = kk * lax.rsqrt(seg_sum(kk * kk) + 1e-6)
        bb = kk * aa
        wy = yield from _lockstep(
            unit_chain(rr[rows], k2[rows], vv[rows], kk[rows], bb[rows], logw[rows]) for rows in units)
        for rows, w in zip(units, wy):
            st = [_dot_nt(jnp.concatenate([w["wa"][sq], w["rt"][sq]], axis=0), s_scr[s, p])
                  for s, sq in enumerate(seqs)]
            yield
            u = w["uv"] + _rows([x[:tc] for x in st])
            r_state = _rows([x[tc:] for x in st])
            rbu = _dot(w["rb"], stack(u))
            v_u = vv[rows]
            upd = [_dot_tn(jnp.concatenate([v_u[sq], -u[sq]], axis=0),
                           jnp.concatenate([w["k_end"][sq], w["b_end"][sq]], axis=0)) for sq in seqs]
            yield
            for s in range(sb):
                p_end = w["p_in"][(s + 1) * tc - 1:(s + 1) * tc, :]
                s_scr[s, p] = s_scr[s, p] * p_end + jnp.where(bd_mask, upd[s], 0.0)
            o = r_state + (w["rkv"] - rbu)
            mean = seg_sum(o) * (1.0 / RW_HD)
            d = o - mean
            var = seg_sum(d * d) * (1.0 / RW_HD)
            bonus = seg_sum(rr[rows] * k2[rows] * rkw_ref[:, c])
            on = d * lax.rsqrt(var + RW_GN_EPS) * gnw_ref[:, c] + gnb_ref[:, c]
            o_rw = (on + bonus * v_u) * gate[rows]
            mix = _sigmoid(ga_ref[rows, c]) * odn_ref[rows, c] + _sigmoid(gb_ref[rows, c]) * o_rw
            o_ref[rows, c] = mix.astype(BF16)

    _run(_lockstep(pair_chain(p) for p in range(RW_PAIRS)))

    @pl.when(t == nt - 1)
    def _():
        for s in range(sb):
            for p in range(RW_PAIRS):
                sp = s_scr[s, p]
                sout_ref[s, 2 * p] = sp[0:RW_HD, 0:RW_HD]
                sout_ref[s, 2 * p + 1] = sp[RW_HD:, RW_HD:]


def _rw(p, o_dn, weights, s0, *, nb, seq, sb, tc, nc):
    nt = seq // (nc * tc)
    r = sb * nc * tc
    assert seq % (nc * tc) == 0 and nb % sb == 0 and (nt == 1 or sb == 1) and (nc == 1 or sb == 1)
    bcast = s0.shape[0] == 1
    assert not bcast or sb == 1
    (w0, a0, kkw, kaw, rkw, gnw, gnb, w2p, a2p, g2p) = weights

    def st_idx(i):
        return 0 if bcast else i

    def vec(width):
        return pl.BlockSpec((1, width), lambda i, t: (0, 0))

    def lora_w():
        return pl.BlockSpec((RW_LORA, RW_W), lambda i, t: (0, 0))

    def rows(col):
        return pl.BlockSpec((r, RW_W), lambda i, t: (i * nt + t, col))

    def named(name):
        return rows(COL[name] // RW_W)

    return pl.pallas_call(
        functools.partial(_rw_kernel, sb=sb, tc=tc, nc=nc, nt=nt),
        out_shape=(jax.ShapeDtypeStruct((nb * seq, RW_W), BF16),
                   jax.ShapeDtypeStruct((nb, RW_HEADS, RW_HD, RW_HD), F32)),
        grid=(nb // sb, nt),
        in_specs=[
            named("r"), named("kx"), named("vx"),
            pl.BlockSpec((r, RW_LORA), lambda i, t: (i * nt + t, COL["lora"] // RW_LORA)),
            vec(RW_W), vec(RW_W), vec(RW_W), vec(RW_W), vec(RW_W), vec(RW_W), vec(RW_W),
            lora_w(), lora_w(), lora_w(),
            pl.BlockSpec((sb, RW_HEADS, RW_HD, RW_HD), lambda i, t: (st_idx(i), 0, 0, 0)),
            rows(0), named("ga"), named("gb"),
        ],
        out_specs=(
            rows(0),
            pl.BlockSpec((sb, RW_HEADS, RW_HD, RW_HD), lambda i, t: (i, 0, 0, 0)),
        ),
        scratch_shapes=[pltpu.VMEM((sb, RW_PAIRS, LANES, LANES), F32)],
        compiler_params=pltpu.CompilerParams(
            dimension_semantics=("arbitrary", "arbitrary"), vmem_limit_bytes=VMEM_LIMIT),
        name="rw",
    )(p, p, p, p, w0, a0, kkw, kaw, rkw, gnw, gnb, w2p, a2p, g2p, s0, o_dn, p, p)


def _merge_kernel(x_ref, mix_ref, wout_ref, gffn_ref, w1_ref, w2_ref, gfin_ref,
                  y_ref, x1_scr, h2_scr, acc_scr, *, nf):
    f = pl.program_id(1)

    @pl.when(f == 0)
    def _():
        x1 = x_ref[...] + jnp.dot(mix_ref[...], wout_ref[...], preferred_element_type=F32)
        x1_scr[...] = x1
        h2 = x1 * lax.rsqrt(jnp.mean(x1 * x1, axis=-1, keepdims=True) + NORM_EPS) * gffn_ref[...]
        h2_scr[...] = h2.astype(BF16)
        acc_scr[...] = jnp.zeros_like(acc_scr)

    hid = jnp.dot(h2_scr[...], w1_ref[...], preferred_element_type=F32)
    hid = jnp.square(jnp.maximum(hid, 0.0))
    acc_scr[...] += jnp.dot(hid.astype(BF16), w2_ref[...], preferred_element_type=F32)

    @pl.when(f == nf - 1)
    def _():
        y = x1_scr[...] + acc_scr[...]
        y_ref[...] = y * lax.rsqrt(jnp.mean(y * y, axis=-1, keepdims=True) + NORM_EPS) * gfin_ref[...]


def _merge(x2d, mix, w_out, g_ffn, w1, w2, g_fin, *, tm, tf):
    n = x2d.shape[0]
    nf = D_FF // tf
    assert n % tm == 0 and D_FF % tf == 0
    row = lambda i, f: (i, 0)
    return pl.pallas_call(
        functools.partial(_merge_kernel, nf=nf),
        out_shape=jax.ShapeDtypeStruct((n, D_MODEL), F32),
        grid=(n // tm, nf),
        in_specs=[
            pl.BlockSpec((tm, D_MODEL), row),
            pl.BlockSpec((tm, D_MODEL), row),
            pl.BlockSpec((D_MODEL, D_MODEL), lambda i, f: (0, 0)),
            pl.BlockSpec((1, D_MODEL), lambda i, f: (0, 0)),
            pl.BlockSpec((D_MODEL, tf), lambda i, f: (0, f)),
            pl.BlockSpec((tf, D_MODEL), lambda i, f: (f, 0)),
            pl.BlockSpec((1, D_MODEL), lambda i, f: (0, 0)),
        ],
        out_specs=pl.BlockSpec((tm, D_MODEL), row),
        scratch_shapes=[
            pltpu.VMEM((tm, D_MODEL), F32),
            pltpu.VMEM((tm, D_MODEL), BF16),
            pltpu.VMEM((tm, D_MODEL), F32),
        ],
        compiler_params=pltpu.CompilerParams(
            dimension_semantics=("arbitrary", "arbitrary"), vmem_limit_bytes=VMEM_LIMIT),
        name="merge",
    )(x2d, mix, w_out, g_ffn, w1, w2, g_fin)


def _pad_rows_front(x, rows):
    b, n, w = x.shape
    return jnp.concatenate([jnp.zeros((b, rows - n, w), x.dtype), x], axis=1)


def _layer(x, conv_buf, dn_s, rw_prev, rw_s, wts, *, sb, tc, nc, tm, tm_mlp):
    nb, seq, _ = x.shape
    n = nb * seq
    x2d = x.reshape(n, D_MODEL)
    p, gb, conv_new, shift_new = _proj(
        x2d, wts["g_mix"], wts["w_main_t"], wts["w_ab_t"], wts["a_log"], wts["dt_bias"], wts["conv_w"],
        wts["rw_mu"], _pad_rows_front(conv_buf, SUBLANES), _pad_rows_front(rw_prev[:, None, :], SUBLANES),
        nb=nb, seq=seq, tm=tm, rc=sb * nc * tc)
    o_dn, dn_new = _dn(p, gb, wts["dn_norm_w"], dn_s, nb=nb, seq=seq, sb=sb, tc=tc, nc=nc)
    mix, rw_new = _rw(p, o_dn, wts["rw"], rw_s, nb=nb, seq=seq, sb=sb, tc=tc, nc=nc)
    y = _merge(x2d, mix, wts["w_out"], wts["g_ffn"], wts["w_ff1"], wts["w_ff2"], wts["g_final"],
               tm=tm_mlp, tf=FF_TILE)
    return y.reshape(nb, seq, D_MODEL), conv_new, dn_new, shift_new[:, 0], rw_new


def kernel(x_prompt, x_sample, state_dn_conv, state_dn, state_rw_shift, state_rw, meta_tokens, g_mix_norm, w_in, dn_conv_w, dn_a_log, dn_dt_bias, dn_norm_w, rw_mu, rw_w0, rw_w2, rw_a0, rw_a2, rw_g2, rw_k_k, rw_k_a, rw_r_k, rw_gn_w, rw_gn_b, w_out, g_ffn_norm, w_ff1, w_ff2, g_final):
    assert g_mix_norm.shape[0] == 1, "single layer"
    w_t = w_in[0].T
    o_a = DN_QKV + D_MODEL
    o_rwp = o_a + 2 * DN_HEADS
    o_gate = o_rwp + RW_SHIFT_W
    src = {"q": 0, "k": D_MODEL, "v": 2 * D_MODEL, "z": DN_QKV, "r": o_rwp, "kx": o_rwp + RW_W,
           "vx": o_rwp + 2 * RW_W, "lora": o_rwp + 3 * RW_W, "ga": o_gate, "gb": o_gate + D_MODEL}
    w_main_t = jnp.concatenate([w_t[src[name]:src[name] + width] for name, width, _, _ in SEGMENTS],
                               axis=0).astype(BF16)
    w_ab_t = w_t[o_a:o_rwp].astype(BF16)

    def lora_pad(wl, lo):
        return jnp.zeros((RW_LORA, RW_W), F32).at[lo:lo + wl.shape[0]].set(wl).astype(BF16)

    row = lambda v: v.reshape(1, -1).astype(F32)
    wts = {
        "g_mix": row(g_mix_norm[0]), "w_main_t": w_main_t, "w_ab_t": w_ab_t,
        "a_log": dn_a_log[0].reshape(DN_HEADS, 1), "dt_bias": dn_dt_bias[0].reshape(DN_HEADS, 1),
        "conv_w": dn_conv_w[0], "dn_norm_w": row(dn_norm_w[0]), "rw_mu": row(rw_mu[0]),
        "rw": (row(rw_w0[0]), row(rw_a0[0]), row(rw_k_k[0]), row(rw_k_a[0]), row(rw_r_k[0]),
               row(rw_gn_w[0]), row(rw_gn_b[0]),
               lora_pad(rw_w2[0], 0), lora_pad(rw_a2[0], 64), lora_pad(rw_g2[0], 128)),
        "w_out": w_out[0].astype(BF16), "g_ffn": row(g_ffn_norm[0]),
        "w_ff1": w_ff1[0].astype(BF16), "w_ff2": w_ff2[0].astype(BF16), "g_final": row(g_final),
    }

    nbm = UNIT_ROWS // N_META
    xm = jnp.broadcast_to(meta_tokens.astype(F32)[None], (nbm, N_META, D_MODEL))
    _, conv_m, dn_m, shift_m, rw_m = _layer(
        xm, jnp.zeros((nbm, 3, DN_QKV), F32), jnp.zeros((nbm, DN_HEADS, DN_D, DN_D), F32),
        jnp.zeros((nbm, RW_SHIFT_W), F32), jnp.zeros((nbm, RW_HEADS, RW_HD, RW_HD), F32), wts,
        sb=nbm, tc=N_META, nc=1, tm=nbm * N_META, tm_mlp=nbm * N_META)

    y_p, conv_p, dn_p, shift_p, rw_p = _layer(
        x_prompt, conv_m[:1], dn_m[:1], shift_m[:1], rw_m[:1], wts,
        sb=1, tc=CHUNK, nc=CHUNKS_PER_STEP, tm=PROJ_TILE, tm_mlp=MLP_TILE)

    dec_len = x_sample.shape[1]
    y_s, conv_s, dn_s, shift_s, rw_s = _layer(
        x_sample, state_dn_conv[0], state_dn[0], state_rw_shift[0], state_rw[0], wts,
        sb=UNIT_ROWS // dec_len, tc=dec_len, nc=1, tm=PROJ_TILE_SHORT, tm_mlp=MLP_TILE)

    return (y_p, y_s, conv_p[None], dn_p[None], shift_p[None], rw_p[None],
            conv_s[None], dn_s[None], shift_s[None], rw_s[None])
```

```python
import functools
import math

import jax
import jax.numpy as jnp
from jax import lax
from jax.experimental import pallas as pl
from jax.experimental.pallas import tpu as pltpu

F32 = jnp.float32
BF16 = jnp.bfloat16

D_MODEL = 1024
N_META = 16
DN_HEADS = 8
DN_D = 128
DN_QKV = 3 * DN_HEADS * DN_D
RW_HEADS = 16
RW_HD = 64
RW_PAIRS = RW_HEADS // 2
RW_W = RW_HEADS * RW_HD
RW_LORA = 256
RW_SHIFT_W = 3 * RW_W + RW_LORA
D_FF = 4 * D_MODEL
NORM_EPS = 1e-6
RW_GN_EPS = 64e-5
LANES = 128
SUBLANES = 8
INV_BLOCK = 16
CHUNK = 64
UNIT_ROWS = 64
CHUNKS_PER_STEP = 8
PROJ_TILE = 512
PROJ_TILE_SHORT = 256
MLP_TILE = 1024
FF_TILE = 2048

SEGMENTS = (
    ("q", D_MODEL, "conv", 0), ("k", D_MODEL, "conv", D_MODEL),
    ("r", RW_W, "shift", 0), ("kx", RW_W, "shift", RW_W),
    ("ga", D_MODEL, "raw", 0), ("gb", D_MODEL, "raw", 0),
    ("v", D_MODEL, "conv", 2 * D_MODEL), ("z", D_MODEL, "raw", 0),
    ("vx", RW_W, "shift", 2 * RW_W), ("lora", RW_LORA, "shift", 3 * RW_W),
)
COL = {}
P_W = 0
for _name, _width, _, _ in SEGMENTS:
    COL[_name] = P_W
    P_W += _width
P_HALF = P_W // 2
PASS0_CONV = 2 * D_MODEL
PASS0_SHIFT = 2 * RW_W
PIECE_W = 512


def _chunk_plan(half):
    lo, hi = half * P_HALF, (half + 1) * P_HALF
    plan = []
    for name, width, kind, off in SEGMENTS:
        start = COL[name]
        c = max(start, lo)
        while c < min(start + width, hi):
            w = min(PIECE_W, min(start + width, hi) - c)
            plan.append((c - lo, w, kind, off + (c - start), name))
            c += w
    return plan

VMEM_LIMIT = 56 * 1024 * 1024


def _sigmoid(x):
    return 1.0 / (1.0 + jnp.exp(-x))


def _silu(x):
    return x * _sigmoid(x)


def _softplus(x):
    return jnp.maximum(x, 0.0) + jnp.log(1.0 + jnp.exp(-jnp.abs(x)))


def _dot(a, b):
    return jnp.dot(a.astype(BF16), b.astype(BF16), preferred_element_type=F32)


def _dot_nt(a, b):
    return lax.dot_general(a.astype(BF16), b.astype(BF16), (((1,), (1,)), ((), ())),
                           preferred_element_type=F32)


def _dot_tn(a, b):
    return lax.dot_general(a.astype(BF16), b.astype(BF16), (((0,), (0,)), ((), ())),
                           preferred_element_type=F32)


def _rows(xs):
    return xs[0] if len(xs) == 1 else jnp.concatenate(xs, axis=0)


def _dot_rows(xs, b, dot=_dot):
    out = dot(_rows(xs), b)
    off, parts = 0, []
    for x in xs:
        parts.append(out[off:off + x.shape[0]])
        off += x.shape[0]
    return parts


def _dot_mask(mask_bf16, x):
    h1 = x.astype(BF16)
    r1 = x - h1.astype(F32)
    h2 = r1.astype(BF16)
    h3 = (r1 - h2.astype(F32)).astype(BF16)
    return (jnp.dot(mask_bf16, h1, preferred_element_type=F32)
            + (jnp.dot(mask_bf16, h2, preferred_element_type=F32)
               + jnp.dot(mask_bf16, h3, preferred_element_type=F32)))


def _iota2(n, m):
    return (lax.broadcasted_iota(jnp.int32, (n, m), 0),
            lax.broadcasted_iota(jnp.int32, (n, m), 1))


def _group(idx, size):
    return lax.shift_right_logical(idx, int(math.log2(size)))


def _column_packing(ru, tc):
    ri, ci = _iota2(ru, 2 * ru)
    cc = ci & (ru - 1)
    same = _group(ri, tc) == _group(cc, tc)
    first = lax.broadcasted_iota(jnp.int32, (1, 2 * ru), 1) < ru

    def pdot(x, y):
        y_bd = jnp.concatenate([jnp.where(first, y, 0.0), jnp.where(first, 0.0, y)], axis=0)
        return _dot(x, y_bd)

    return ri, cc, same, first, pdot


def _lockstep(chains):
    chains = list(chains)
    results = [None] * len(chains)
    active = list(range(len(chains)))
    while active:
        for i in list(active):
            try:
                next(chains[i])
            except StopIteration as done:
                results[i] = done.value
                active.remove(i)
        if active:
            yield
    return results


def _run(chain):
    for _ in chain:
        pass


def _neumann_inverse(a, eye, nil, dot):
    inv = eye - a
    if nil <= 2:
        return inv
    power = dot(a, a)
    yield
    k = 4
    while k < nil:
        step, power = _dot_rows([inv, power], power, dot=dot)
        yield
        inv = inv + step
        k *= 2
    step = dot(inv, power)
    yield
    return inv + step


def _unit_lower_inverse(a, ri, ci, tc, dot=_dot):
    eye = jnp.where(ri == ci, 1.0, 0.0).astype(F32)
    if tc <= INV_BLOCK:
        return (yield from _neumann_inverse(a, eye, tc, dot))
    assert tc // INV_BLOCK <= 4
    diag = _group(ri, INV_BLOCK) == _group(ci, INV_BLOCK)
    d = jnp.where(diag, a, 0.0)
    low = a - d
    dinv = yield from _neumann_inverse(d, eye, INV_BLOCK, dot)
    b = dot(low, dinv)
    yield
    db, bb = _dot_rows([dinv, b], b, dot=dot)
    yield
    p1 = dinv - db
    tail = dot(p1, bb)
    yield
    return p1 + tail


def _proj_kernel(x_ref, g_ref, wt_ref, wab_ref, alog_ref, dtb_ref, cw_ref, mu_ref, cb_ref, prev_ref,
                 p_ref, gb_ref, cout0_ref, shift0_ref, cout1_ref, shift1_ref, *maybe_hist,
                 tm, rc, sbp, ttp, ntp):
    carry = ntp > 1
    hist = maybe_hist[0] if carry else None
    j = pl.program_id(0)
    i = pl.program_id(1)
    tpos = lax.rem(i, ntp)
    x = x_ref[...]
    h = x * lax.rsqrt(jnp.mean(x * x, axis=-1, keepdims=True) + NORM_EPS) * g_ref[...]
    hb = h.astype(BF16)
    nt_dims = (((1,), (1,)), ((), ()))

    def mm(c0, cw):
        return lax.dot_general(hb, wt_ref[c0:c0 + cw, :], nt_dims, preferred_element_type=F32)

    seq_pos = lax.broadcasted_iota(jnp.int32, (tm, 1), 0) & (ttp - 1)

    def shift_rows(a, first):
        rolled = pltpu.roll(a, len(first), axis=0)
        if sbp == 1:
            head = rolled[0:SUBLANES]
            for i, f in enumerate(first):
                head = jnp.where(seq_pos[0:SUBLANES] == i, f[0], head)
            return jnp.concatenate([head, rolled[SUBLANES:]], axis=0)
        for i, f in enumerate(first):
            fill = jnp.broadcast_to(f, (sbp, ttp, a.shape[1])).reshape(tm, a.shape[1])
            rolled = jnp.where(seq_pos == i, fill, rolled)
        return rolled

    def run_pass(plan):
        if carry:
            @pl.when(tpos == 0)
            def _():
                hist[:, 0:SUBLANES, 0:DN_QKV] = cb_ref[...]
                hist[:, 0:SUBLANES, DN_QKV:] = prev_ref[...]

            @pl.when(tpos > 0)
            def _():
                hist[:, 0:SUBLANES, :] = hist[:, SUBLANES:, :]

        for c0, cw, kind, off, name in plan:
            cur = mm(c0, cw)
            if kind == "raw":
                p_ref[:, c0:c0 + cw] = cur
                continue
            cur3 = cur.reshape(sbp, ttp, cw)
            if carry:
                hc = slice(off, off + cw) if kind == "conv" else slice(DN_QKV + off, DN_QKV + off + cw)
                old = hist[:, 0:SUBLANES, hc]
                hist[:, SUBLANES:, hc] = cur3[:, ttp - SUBLANES:, :]
            elif kind == "conv":
                old = cb_ref[:, :, off:off + cw]
                out_ref, base = (cout0_ref, 0) if off < PASS0_CONV else (cout1_ref, PASS0_CONV)
                out_ref[:, :, off - base:off - base + cw] = cur3[:, ttp - 3:, :]
            else:
                old = prev_ref[:, :, off:off + cw]
                out_ref, base = (shift0_ref, 0) if off < PASS0_SHIFT else (shift1_ref, PASS0_SHIFT)
                out_ref[:, :, off - base:off - base + cw] = cur3[:, ttp - 1:, :]
            x1 = old[:, 7:8, :]
            if kind == "shift":
                prev = shift_rows(cur, [x1])
                p_ref[:, c0:c0 + cw] = cur + (prev - cur) * mu_ref[:, off:off + cw]
                continue
            w = [cw_ref[tap:tap + 1, off:off + cw] for tap in range(4)]
            x2, x3 = old[:, 6:7, :], old[:, 5:6, :]
            y1 = shift_rows(cur, [x1])
            far = cur * w[1] + y1 * w[0]
            acc = cur * w[3] + y1 * w[2] + shift_rows(far, [x2 * w[1] + x3 * w[0], x1 * w[1] + x2 * w[0]])
            y = _silu(acc)
            if name == "v":
                p_ref[:, c0:c0 + cw] = y
            else:
                scale = DN_D ** -0.5 if name == "q" else 1.0
                for c1 in range(0, cw, DN_D):
                    yh = y[:, c1:c1 + DN_D]
                    yh = yh * (lax.rsqrt(jnp.sum(yh * yh, axis=-1, keepdims=True) + 1e-6) * scale)
                    p_ref[:, c0 + c1:c0 + c1 + DN_D] = yh

    @pl.when(j == 0)
    def _():
        run_pass(_chunk_plan(0))
        for c in range(tm // rc):
            ab = lax.dot_general(wab_ref[...], hb[c * rc:(c + 1) * rc], nt_dims,
                                 preferred_element_type=F32)
            g = -jnp.exp(alog_ref[...]) * _softplus(ab[0:DN_HEADS] + dtb_ref[...])
            gb_ref[c, 0:DN_HEADS, :] = g
            gb_ref[c, DN_HEADS:2 * DN_HEADS, :] = _sigmoid(ab[DN_HEADS:2 * DN_HEADS])

        if carry:
            @pl.when(tpos == ntp - 1)
            def _():
                cout0_ref[...] = hist[:, 2 * SUBLANES - 3:, 0:PASS0_CONV]
                shift0_ref[...] = hist[:, 2 * SUBLANES - 1:, DN_QKV:DN_QKV + PASS0_SHIFT]

    @pl.when(j == 1)
    def _():
        run_pass(_chunk_plan(1))

        if carry:
            @pl.when(tpos == ntp - 1)
            def _():
                cout1_ref[...] = hist[:, 2 * SUBLANES - 3:, PASS0_CONV:DN_QKV]
                shift1_ref[...] = hist[:, 2 * SUBLANES - 1:, DN_QKV + PASS0_SHIFT:]


def _proj(x2d, g_norm, w_main_t, w_ab_t, a_log, dt_bias, conv_w, mu, conv_buf8, prev8, *, nb, seq, tm, rc):
    n = nb * seq
    ttp = min(seq, tm)
    sbp = tm // ttp
    ntp = seq // ttp
    assert n % tm == 0 and tm % rc == 0 and tm % ttp == 0 and seq % ttp == 0
    bcast = conv_buf8.shape[0] == 1
    assert bcast == (prev8.shape[0] == 1) and (not bcast or sbp == 1)
    last = (n // tm - 1) // ntp

    def st_idx(i):
        return 0 if bcast else i // ntp

    assert ttp % SUBLANES == 0 and ttp & (ttp - 1) == 0
    first_pass = lambda j, i: ((i // ntp) * (1 - j) + last * j, 0, 0)
    second_pass = lambda j, i: ((i // ntp) * j, 0, 0)
    p, gb, cout0, shift0, cout1, shift1 = pl.pallas_call(
        functools.partial(_proj_kernel, tm=tm, rc=rc, sbp=sbp, ttp=ttp, ntp=ntp),
        out_shape=(jax.ShapeDtypeStruct((n, P_W), F32),
                   jax.ShapeDtypeStruct((n // rc, 2 * DN_HEADS, rc), F32),
                   jax.ShapeDtypeStruct((nb, 3, PASS0_CONV), F32),
                   jax.ShapeDtypeStruct((nb, 1, PASS0_SHIFT), F32),
                   jax.ShapeDtypeStruct((nb, 3, DN_QKV - PASS0_CONV), F32),
                   jax.ShapeDtypeStruct((nb, 1, RW_SHIFT_W - PASS0_SHIFT), F32)),
        grid=(2, n // tm),
        in_specs=[
            pl.BlockSpec((tm, D_MODEL), lambda j, i: (i, 0)),
            pl.BlockSpec((1, D_MODEL), lambda j, i: (0, 0)),
            pl.BlockSpec((P_HALF, D_MODEL), lambda j, i: (j, 0)),
            pl.BlockSpec((2 * DN_HEADS, D_MODEL), lambda j, i: (0, 0)),
            pl.BlockSpec((DN_HEADS, 1), lambda j, i: (0, 0)),
            pl.BlockSpec((DN_HEADS, 1), lambda j, i: (0, 0)),
            pl.BlockSpec((4, DN_QKV), lambda j, i: (0, 0)),
            pl.BlockSpec((1, RW_SHIFT_W), lambda j, i: (0, 0)),
            pl.BlockSpec((sbp, SUBLANES, DN_QKV), lambda j, i: (st_idx(i), 0, 0)),
            pl.BlockSpec((sbp, SUBLANES, RW_SHIFT_W), lambda j, i: (st_idx(i), 0, 0)),
        ],
        out_specs=(
            pl.BlockSpec((tm, P_HALF), lambda j, i: (i, j)),
            pl.BlockSpec((tm // rc, 2 * DN_HEADS, rc),
                         lambda j, i: (i * (1 - j) + (n // tm - 1) * j, 0, 0)),
            pl.BlockSpec((sbp, 3, PASS0_CONV), first_pass),
            pl.BlockSpec((sbp, 1, PASS0_SHIFT), first_pass),
            pl.BlockSpec((sbp, 3, DN_QKV - PASS0_CONV), second_pass),
            pl.BlockSpec((sbp, 1, RW_SHIFT_W - PASS0_SHIFT), second_pass),
        ),
        scratch_shapes=[pltpu.VMEM((sbp, 2 * SUBLANES, DN_QKV + RW_SHIFT_W), F32)] if ntp > 1 else [],
        compiler_params=pltpu.CompilerParams(
            dimension_semantics=("arbitrary", "arbitrary"), vmem_limit_bytes=VMEM_LIMIT),
        name="proj",
    )(x2d, g_norm, w_main_t, w_ab_t, a_log, dt_bias, conv_w, mu, conv_buf8, prev8)
    conv_new = jnp.concatenate([cout0, cout1], axis=-1)
    shift_new = jnp.concatenate([shift0, shift1], axis=-1)
    return p, gb, conv_new, shift_new


def _dn_kernel(q_ref, k_ref, v_ref, z_ref, gb_ref, nw_ref, s0_ref, o_ref, sout_ref, s_scr, *, sb, tc, nc, nt):
    t = pl.program_id(1)
    ru = sb * tc
    hw = DN_HEADS * DN_D

    @pl.when(t == 0)
    def _():
        s_scr[...] = s0_ref[...]

    ri, ci = _iota2(ru, ru)
    same = _group(ri, tc) == _group(ci, tc)
    incl = same & (ri >= ci)
    eye = ri == ci
    rip, cc, same_p, first_cols, pdot = _column_packing(ru, tc)
    incl_p = same_p & (rip >= cc)
    strict_p = same_p & (rip > cc)
    seqs = [slice(s * tc, (s + 1) * tc) for s in range(sb)]
    units = [slice(c * ru, (c + 1) * ru) for c in range(nc)]

    def side_by_side(x0, x1):
        return jnp.concatenate([x0, x1], axis=1)

    def block_diag(x0, x1):
        z = jnp.zeros_like(x0)
        return jnp.concatenate([side_by_side(x0, z), side_by_side(z, x1)], axis=0)

    def unit_chain(heads, rows, q, k, v):
        per = []
        for h, kh in zip(heads, k):
            g_row = gb_ref[0, h:h + 1, :][:, rows]
            b_row = gb_ref[0, DN_HEADS + h:DN_HEADS + h + 1, :][:, rows]
            g_col = jnp.sum(jnp.where(incl, g_row, 0.0), axis=1, keepdims=True)
            g_cum_row = jnp.sum(jnp.where(eye, g_col, 0.0), axis=0, keepdims=True)
            g_tot = jnp.sum(jnp.where(same, g_row, 0.0), axis=1, keepdims=True)
            b_col = jnp.sum(jnp.where(eye, b_row, 0.0), axis=1, keepdims=True)
            per.append(dict(g_col=g_col, g_cum_row=g_cum_row, g_tot=g_tot, b_col=b_col,
                            e_g=jnp.exp(g_col), e_rest=jnp.exp(g_tot - g_col), kb=kh * b_col))
        g_cols = jnp.where(first_cols, per[0]["g_col"], per[1]["g_col"])
        g_rows = side_by_side(per[0]["g_cum_row"], per[1]["g_cum_row"])
        decay = jnp.where(incl_p, jnp.exp(jnp.minimum(g_cols - g_rows, 0.0)), 0.0)
        a_raw, qk_raw = _dot_rows([side_by_side(per[0]["kb"], per[1]["kb"]), side_by_side(q[0], q[1])],
                                  block_diag(k[0], k[1]), dot=_dot_nt)
        yield
        a = jnp.where(strict_p, a_raw * decay, 0.0)
        aqk = qk_raw * decay
        t_inv = yield from _unit_lower_inverse(a, rip, cc, tc, dot=pdot)
        rhs = [side_by_side(vh * w["b_col"], w["kb"] * w["e_g"]) for vh, w in zip(v, per)]
        sol = _dot(t_inv, block_diag(rhs[0], rhs[1]))
        yield
        out = []
        for i, w in enumerate(per):
            base = 2 * DN_D * i
            out.append(dict(u_all=sol[:, base:base + DN_D], w_all=sol[:, base + DN_D:base + 2 * DN_D],
                            qg=q[i] * w["e_g"], kd=k[i] * w["e_rest"], g_tot=w["g_tot"]))
        return aqk, out

    def pair_chain(pair):
        heads = (2 * pair, 2 * pair + 1)
        cols = [slice(h * DN_D, (h + 1) * DN_D) for h in heads]
        q = [q_ref[:, c] for c in cols]
        k = [k_ref[:, c] for c in cols]
        v = [v_ref[:, c] for c in cols]
        wy = yield from _lockstep(
            unit_chain(heads, rows, [x[rows] for x in q], [x[rows] for x in k], [x[rows] for x in v])
            for rows in units)
        for rows, (aqk, per) in zip(units, wy):
            ws = [[_dot(jnp.concatenate([w["w_all"][sq], w["qg"][sq]], axis=0), s_scr[s, h])
                   for s, sq in enumerate(seqs)] for h, w in zip(heads, per)]
            yield
            u = [_rows([w["u_all"][sq] - x[:tc] for sq, x in zip(seqs, wsh)]) for w, wsh in zip(per, ws)]
            q_s = [_rows([x[tc:] for x in wsh]) for wsh in ws]
            intra = _dot(aqk, block_diag(u[0], u[1]))
            upd = [[_dot_tn(w["kd"][sq], uh[sq]) for sq in seqs] for w, uh in zip(per, u)]
            yield
            for i, (h, w) in enumerate(zip(heads, per)):
                for s in range(sb):
                    gl = jnp.exp(w["g_tot"][s * tc:s * tc + 1, :])
                    s_scr[s, h] = s_scr[s, h] * gl + upd[i][s]
                o = q_s[i] + intra[:, i * DN_D:(i + 1) * DN_D]
                o = o * lax.rsqrt(jnp.mean(o * o, axis=-1, keepdims=True) + NORM_EPS) * nw_ref[...]
                o_ref[rows, cols[i]] = o * _silu(z_ref[rows, cols[i]])

    _run(_lockstep(pair_chain(pair) for pair in range(DN_HEADS // 2)))

    @pl.when(t == nt - 1)
    def _():
        sout_ref[...] = s_scr[...]


def _dn(p, gb, norm_w, s0, *, nb, seq, sb, tc, nc):
    nt = seq // (nc * tc)
    r = sb * nc * tc
    assert seq % (nc * tc) == 0 and nb % sb == 0 and (nt == 1 or sb == 1) and (nc == 1 or sb == 1)
    bcast = s0.shape[0] == 1
    assert not bcast or sb == 1
    hw = DN_HEADS * DN_D

    def st_idx(i):
        return 0 if bcast else i

    def rows(name):
        return pl.BlockSpec((r, hw), lambda i, t: (i * nt + t, COL[name] // hw))

    return pl.pallas_call(
        functools.partial(_dn_kernel, sb=sb, tc=tc, nc=nc, nt=nt),
        out_shape=(jax.ShapeDtypeStruct((nb * seq, hw), F32),
                   jax.ShapeDtypeStruct((nb, DN_HEADS, DN_D, DN_D), F32)),
        grid=(nb // sb, nt),
        in_specs=[
            rows("q"), rows("k"), rows("v"), rows("z"),
            pl.BlockSpec((1, 2 * DN_HEADS, r), lambda i, t: (i * nt + t, 0, 0)),
            pl.BlockSpec((1, DN_D), lambda i, t: (0, 0)),
            pl.BlockSpec((sb, DN_HEADS, DN_D, DN_D), lambda i, t: (st_idx(i), 0, 0, 0)),
        ],
        out_specs=(
            pl.BlockSpec((r, hw), lambda i, t: (i * nt + t, 0)),
            pl.BlockSpec((sb, DN_HEADS, DN_D, DN_D), lambda i, t: (i, 0, 0, 0)),
        ),
        scratch_shapes=[pltpu.VMEM((sb, DN_HEADS, DN_D, DN_D), F32)],
        compiler_params=pltpu.CompilerParams(
            dimension_semantics=("arbitrary", "arbitrary"), vmem_limit_bytes=VMEM_LIMIT),
        name="dn",
    )(p, p, p, p, gb, norm_w, s0)


def _rw_kernel(r_ref, k_ref, v_ref, l_ref, w0_ref, a0_ref, kkw_ref, kaw_ref, rkw_ref,
               gnw_ref, gnb_ref, w2_ref, a2_ref, g2_ref, s0_ref, odn_ref, ga_ref, gb_ref,
               o_ref, sout_ref, s_scr, *, sb, tc, nc, nt):
    t = pl.program_id(1)
    ru = sb * tc
    half = lax.broadcasted_iota(jnp.int32, (1, LANES), 1) < RW_HD
    ri128, ci128 = _iota2(LANES, LANES)
    bd_mask = _group(ri128, RW_HD) == _group(ci128, RW_HD)

    def seg_sum(x):
        first = jnp.sum(jnp.where(half, x, 0.0), axis=-1, keepdims=True)
        second = jnp.sum(jnp.where(half, 0.0, x), axis=-1, keepdims=True)
        return jnp.where(half, first, second)

    @pl.when(t == 0)
    def _():
        zero = jnp.zeros((RW_HD, RW_HD), F32)
        for s in range(sb):
            for p in range(RW_PAIRS):
                top = jnp.concatenate([s0_ref[s, 2 * p], zero], axis=1)
                bot = jnp.concatenate([zero, s0_ref[s, 2 * p + 1]], axis=1)
                s_scr[s, p] = jnp.concatenate([top, bot], axis=0)

    xm_l = l_ref[...]
    lora_w = jnp.tanh(xm_l).astype(BF16)
    lora_a = xm_l.astype(BF16)
    lora_g = _sigmoid(xm_l).astype(BF16)

    ri, ci = _iota2(ru, ru)
    cum_mask = jnp.where((_group(ri, tc) == _group(ci, tc)) & (ri >= ci), 1.0, 0.0).astype(BF16)
    rip, cc, same_p, _, pdot = _column_packing(ru, tc)
    incl_p = same_p & (rip >= cc)
    strict_p = same_p & (rip > cc)
    seqs = [slice(s * tc, (s + 1) * tc) for s in range(sb)]
    units = [slice(c * ru, (c + 1) * ru) for c in range(nc)]

    def stack(x):
        return jnp.concatenate([jnp.where(half, x, 0.0), jnp.where(half, 0.0, x)], axis=0)

    def unit_chain(rr, k2, vv, kk, bb, logw):
        lcum = _dot_mask(cum_mask, logw)
        yield
        ltot = jnp.broadcast_to(lcum.reshape(sb, tc, LANES)[:, tc - 1:tc, :],
                                (sb, tc, LANES)).reshape(ru, LANES)
        p_in = jnp.exp(lcum)
        p_inv = jnp.exp(-lcum)
        alpha = kk * jnp.exp(lcum - logw)
        beta = bb * p_inv
        kt = k2 * p_inv
        rt = rr * p_in
        e_rest = jnp.exp(ltot - lcum)
        beta2, kt2, v2 = stack(beta), stack(kt), stack(vv)
        l_raw, rb_raw = _dot_rows([alpha, rt], beta2, dot=_dot_nt)
        lk_raw, rk_raw = _dot_rows([alpha, rt], kt2, dot=_dot_nt)
        yield
        lmat = jnp.where(strict_p, l_raw, 0.0)
        rb = jnp.where(incl_p, rb_raw, 0.0)
        lk = jnp.where(strict_p, lk_raw, 0.0)
        rk = jnp.where(incl_p, rk_raw, 0.0)
        lkv, rkv = _dot_rows([lk, rk], v2)
        t_inv = yield from _unit_lower_inverse(lmat, rip, cc, tc, dot=pdot)
        sol = _dot(t_inv, jnp.concatenate([stack(alpha), stack(lkv)], axis=1))
        yield
        return dict(wa=sol[:, :LANES], uv=sol[:, LANES:], rt=rt, rb=rb, rkv=rkv, p_in=p_in,
                    k_end=k2 * e_rest, b_end=bb * e_rest)

    def pair_chain(p):
        c = slice(p * LANES, (p + 1) * LANES)
        rr = r_ref[:, c]
        xk = k_ref[:, c]
        vv = v_ref[:, c]
        lw = jnp.dot(lora_w, w2_ref[:, c], preferred_element_type=F32)
        la = jnp.dot(lora_a, a2_ref[:, c], preferred_element_type=F32)
        gate = jnp.dot(lora_g, g2_ref[:, c], preferred_element_type=F32)
        yield
        logw = -math.exp(-0.5) * _sigmoid(w0_ref[:, c] + lw)
        aa = _sigmoid(a0_ref[:, c] + la)
        kk = xk * kkw_ref[:, c]
        k2 = xk * (1.0 + (aa - 1.0) * kaw_ref[:, c])
        kk = kk * lax.rsqrt(seg_sum(kk * kk) + 1e-6)
        bb = kk * aa
        wy = yield from _lockstep(
            unit_chain(rr[rows], k2[rows], vv[rows], kk[rows], bb[rows], logw[rows]) for rows in units)
        for rows, w in zip(units, wy):
            st = [_dot_nt(jnp.concatenate([w["wa"][sq], w["rt"][sq]], axis=0), s_scr[s, p])
                  for s, sq in enumerate(seqs)]
            yield
            u = w["uv"] + _rows([x[:tc] for x in st])
            r_state = _rows([x[tc:] for x in st])
            rbu = _dot(w["rb"], stack(u))
            v_u = vv[rows]
            upd = [_dot_tn(jnp.concatenate([v_u[sq], -u[sq]], axis=0),
                           jnp.concatenate([w["k_end"][sq], w["b_end"][sq]], axis=0)) for sq in seqs]
            yield
            for s in range(sb):
                p_end = w["p_in"][(s + 1) * tc - 1:(s + 1) * tc, :]
                s_scr[s, p] = s_scr[s, p] * p_end + jnp.where(bd_mask, upd[s], 0.0)
            o = r_state + (w["rkv"] - rbu)
            mean = seg_sum(o) * (1.0 / RW_HD)
            d = o - mean
            var = seg_sum(d * d) * (1.0 / RW_HD)
            bonus = seg_sum(rr[rows] * k2[rows] * rkw_ref[:, c])
            on = d * lax.rsqrt(var + RW_GN_EPS) * gnw_ref[:, c] + gnb_ref[:, c]
            o_rw = (on + bonus * v_u) * gate[rows]
            mix = _sigmoid(ga_ref[rows, c]) * odn_ref[rows, c] + _sigmoid(gb_ref[rows, c]) * o_rw
            o_ref[rows, c] = mix.astype(BF16)

    _run(_lockstep(pair_chain(p) for p in range(RW_PAIRS)))

    @pl.when(t == nt - 1)
    def _():
        for s in range(sb):
            for p in range(RW_PAIRS):
                sp = s_scr[s, p]
                sout_ref[s, 2 * p] = sp[0:RW_HD, 0:RW_HD]
                sout_ref[s, 2 * p + 1] = sp[RW_HD:, RW_HD:]


def _rw(p, o_dn, weights, s0, *, nb, seq, sb, tc, nc):
    nt = seq // (nc * tc)
    r = sb * nc * tc
    assert seq % (nc * tc) == 0 and nb % sb == 0 and (nt == 1 or sb == 1) and (nc == 1 or sb == 1)
    bcast = s0.shape[0] == 1
    assert not bcast or sb == 1
    (w0, a0, kkw, kaw, rkw, gnw, gnb, w2p, a2p, g2p) = weights

    def st_idx(i):
        return 0 if bcast else i

    def vec(width):
        return pl.BlockSpec((1, width), lambda i, t: (0, 0))

    def lora_w():
        return pl.BlockSpec((RW_LORA, RW_W), lambda i, t: (0, 0))

    def rows(col):
        return pl.BlockSpec((r, RW_W), lambda i, t: (i * nt + t, col))

    def named(name):
        return rows(COL[name] // RW_W)

    return pl.pallas_call(
        functools.partial(_rw_kernel, sb=sb, tc=tc, nc=nc, nt=nt),
        out_shape=(jax.ShapeDtypeStruct((nb * seq, RW_W), BF16),
                   jax.ShapeDtypeStruct((nb, RW_HEADS, RW_HD, RW_HD), F32)),
        grid=(nb // sb, nt),
        in_specs=[
            named("r"), named("kx"), named("vx"),
            pl.BlockSpec((r, RW_LORA), lambda i, t: (i * nt + t, COL["lora"] // RW_LORA)),
            vec(RW_W), vec(RW_W), vec(RW_W), vec(RW_W), vec(RW_W), vec(RW_W), vec(RW_W),
            lora_w(), lora_w(), lora_w(),
            pl.BlockSpec((sb, RW_HEADS, RW_HD, RW_HD), lambda i, t: (st_idx(i), 0, 0, 0)),
            rows(0), named("ga"), named("gb"),
        ],
        out_specs=(
            rows(0),
            pl.BlockSpec((sb, RW_HEADS, RW_HD, RW_HD), lambda i, t: (i, 0, 0, 0)),
        ),
        scratch_shapes=[pltpu.VMEM((sb, RW_PAIRS, LANES, LANES), F32)],
        compiler_params=pltpu.CompilerParams(
            dimension_semantics=("arbitrary", "arbitrary"), vmem_limit_bytes=VMEM_LIMIT),
        name="rw",
    )(p, p, p, p, w0, a0, kkw, kaw, rkw, gnw, gnb, w2p, a2p, g2p, s0, o_dn, p, p)


def _merge_kernel(x_ref, mix_ref, wout_ref, gffn_ref, w1_ref, w2_ref, gfin_ref,
                  y_ref, x1_scr, h2_scr, acc_scr, *, nf):
    f = pl.program_id(1)

    @pl.when(f == 0)
    def _():
        x1 = x_ref[...] + jnp.dot(mix_ref[...], wout_ref[...], preferred_element_type=F32)
        x1_scr[...] = x1
        h2 = x1 * lax.rsqrt(jnp.mean(x1 * x1, axis=-1, keepdims=True) + NORM_EPS) * gffn_ref[...]
        h2_scr[...] = h2.astype(BF16)
        acc_scr[...] = jnp.zeros_like(acc_scr)

    hid = jnp.dot(h2_scr[...], w1_ref[...], preferred_element_type=F32)
    hid = jnp.square(jnp.maximum(hid, 0.0))
    acc_scr[...] += jnp.dot(hid.astype(BF16), w2_ref[...], preferred_element_type=F32)

    @pl.when(f == nf - 1)
    def _():
        y = x1_scr[...] + acc_scr[...]
        y_ref[...] = y * lax.rsqrt(jnp.mean(y * y, axis=-1, keepdims=True) + NORM_EPS) * gfin_ref[...]


def _merge(x2d, mix, w_out, g_ffn, w1, w2, g_fin, *, tm, tf):
    n = x2d.shape[0]
    nf = D_FF // tf
    assert n % tm == 0 and D_FF % tf == 0
    row = lambda i, f: (i, 0)
    return pl.pallas_call(
        functools.partial(_merge_kernel, nf=nf),
        out_shape=jax.ShapeDtypeStruct((n, D_MODEL), F32),
        grid=(n // tm, nf),
        in_specs=[
            pl.BlockSpec((tm, D_MODEL), row),
            pl.BlockSpec((tm, D_MODEL), row),
            pl.BlockSpec((D_MODEL, D_MODEL), lambda i, f: (0, 0)),
            pl.BlockSpec((1, D_MODEL), lambda i, f: (0, 0)),
            pl.BlockSpec((D_MODEL, tf), lambda i, f: (0, f)),
            pl.BlockSpec((tf, D_MODEL), lambda i, f: (f, 0)),
            pl.BlockSpec((1, D_MODEL), lambda i, f: (0, 0)),
        ],
        out_specs=pl.BlockSpec((tm, D_MODEL), row),
        scratch_shapes=[
            pltpu.VMEM((tm, D_MODEL), F32),
            pltpu.VMEM((tm, D_MODEL), BF16),
            pltpu.VMEM((tm, D_MODEL), F32),
        ],
        compiler_params=pltpu.CompilerParams(
            dimension_semantics=("arbitrary", "arbitrary"), vmem_limit_bytes=VMEM_LIMIT),
        name="merge",
    )(x2d, mix, w_out, g_ffn, w1, w2, g_fin)


def _pad_rows_front(x, rows):
    b, n, w = x.shape
    return jnp.concatenate([jnp.zeros((b, rows - n, w), x.dtype), x], axis=1)


def _layer(x, conv_buf, dn_s, rw_prev, rw_s, wts, *, sb, tc, nc, tm, tm_mlp):
    nb, seq, _ = x.shape
    n = nb * seq
    x2d = x.reshape(n, D_MODEL)
    p, gb, conv_new, shift_new = _proj(
        x2d, wts["g_mix"], wts["w_main_t"], wts["w_ab_t"], wts["a_log"], wts["dt_bias"], wts["conv_w"],
        wts["rw_mu"], _pad_rows_front(conv_buf, SUBLANES), _pad_rows_front(rw_prev[:, None, :], SUBLANES),
        nb=nb, seq=seq, tm=tm, rc=sb * nc * tc)
    o_dn, dn_new = _dn(p, gb, wts["dn_norm_w"], dn_s, nb=nb, seq=seq, sb=sb, tc=tc, nc=nc)
    mix, rw_new = _rw(p, o_dn, wts["rw"], rw_s, nb=nb, seq=seq, sb=sb, tc=tc, nc=nc)
    y = _merge(x2d, mix, wts["w_out"], wts["g_ffn"], wts["w_ff1"], wts["w_ff2"], wts["g_final"],
               tm=tm_mlp, tf=FF_TILE)
    return y.reshape(nb, seq, D_MODEL), conv_new, dn_new, shift_new[:, 0], rw_new


def kernel(x_prompt, x_sample, state_dn_conv, state_dn, state_rw_shift, state_rw, meta_tokens, g_mix_norm, w_in, dn_conv_w, dn_a_log, dn_dt_bias, dn_norm_w, rw_mu, rw_w0, rw_w2, rw_a0, rw_a2, rw_g2, rw_k_k, rw_k_a, rw_r_k, rw_gn_w, rw_gn_b, w_out, g_ffn_norm, w_ff1, w_ff2, g_final):
    assert g_mix_norm.shape[0] == 1, "single layer"
    w_t = w_in[0].T
    o_a = DN_QKV + D_MODEL
    o_rwp = o_a + 2 * DN_HEADS
    o_gate = o_rwp + RW_SHIFT_W
    src = {"q": 0, "k": D_MODEL, "v": 2 * D_MODEL, "z": DN_QKV, "r": o_rwp, "kx": o_rwp + RW_W,
           "vx": o_rwp + 2 * RW_W, "lora": o_rwp + 3 * RW_W, "ga": o_gate, "gb": o_gate + D_MODEL}
    w_main_t = jnp.concatenate([w_t[src[name]:src[name] + width] for name, width, _, _ in SEGMENTS],
                               axis=0).astype(BF16)
    w_ab_t = w_t[o_a:o_rwp].astype(BF16)

    def lora_pad(wl, lo):
        return jnp.zeros((RW_LORA, RW_W), F32).at[lo:lo + wl.shape[0]].set(wl).astype(BF16)

    row = lambda v: v.reshape(1, -1).astype(F32)
    wts = {
        "g_mix": row(g_mix_norm[0]), "w_main_t": w_main_t, "w_ab_t": w_ab_t,
        "a_log": dn_a_log[0].reshape(DN_HEADS, 1), "dt_bias": dn_dt_bias[0].reshape(DN_HEADS, 1),
        "conv_w": dn_conv_w[0], "dn_norm_w": row(dn_norm_w[0]), "rw_mu": row(rw_mu[0]),
        "rw": (row(rw_w0[0]), row(rw_a0[0]), row(rw_k_k[0]), row(rw_k_a[0]), row(rw_r_k[0]),
               row(rw_gn_w[0]), row(rw_gn_b[0]),
               lora_pad(rw_w2[0], 0), lora_pad(rw_a2[0], 64), lora_pad(rw_g2[0], 128)),
        "w_out": w_out[0].astype(BF16), "g_ffn": row(g_ffn_norm[0]),
        "w_ff1": w_ff1[0].astype(BF16), "w_ff2": w_ff2[0].astype(BF16), "g_final": row(g_final),
    }

    nbm = UNIT_ROWS // N_META
    xm = jnp.broadcast_to(meta_tokens.astype(F32)[None], (nbm, N_META, D_MODEL))
    _, conv_m, dn_m, shift_m, rw_m = _layer(
        xm, jnp.zeros((nbm, 3, DN_QKV), F32), jnp.zeros((nbm, DN_HEADS, DN_D, DN_D), F32),
        jnp.zeros((nbm, RW_SHIFT_W), F32), jnp.zeros((nbm, RW_HEADS, RW_HD, RW_HD), F32), wts,
        sb=nbm, tc=N_META, nc=1, tm=nbm * N_META, tm_mlp=nbm * N_META)

    y_p, conv_p, dn_p, shift_p, rw_p = _layer(
        x_prompt, conv_m[:1], dn_m[:1], shift_m[:1], rw_m[:1], wts,
        sb=1, tc=CHUNK, nc=CHUNKS_PER_STEP, tm=PROJ_TILE, tm_mlp=MLP_TILE)

    dec_len = x_sample.shape[1]
    y_s, conv_s, dn_s, shift_s, rw_s = _layer(
        x_sample, state_dn_conv[0], state_dn[0], state_rw_shift[0], state_rw[0], wts,
        sb=UNIT_ROWS // dec_len, tc=dec_len, nc=1, tm=PROJ_TILE_SHORT, tm_mlp=MLP_TILE)

    return (y_p, y_s, conv_p[None], dn_p[None], shift_p[None], rw_p[None],
            conv_s[None], dn_s[None], shift_s[None], rw_s[None])
```

```python
import functools
import math

import jax
import jax.numpy as jnp
from jax import lax
from jax.experimental import pallas as pl
from jax.experimental.pallas import tpu as pltpu

F32 = jnp.float32
BF16 = jnp.bfloat16

D_MODEL = 1024
N_META = 16
DN_HEADS = 8
DN_D = 128
DN_QKV = 3 * DN_HEADS * DN_D
RW_HEADS = 16
RW_HD = 64
RW_PAIRS = RW_HEADS // 2
RW_W = RW_HEADS * RW_HD
RW_LORA = 256
RW_SHIFT_W = 3 * RW_W + RW_LORA
D_FF = 4 * D_MODEL
NORM_EPS = 1e-6
RW_GN_EPS = 64e-5
LANES = 128
SUBLANES = 8
INV_BLOCK = 16
CHUNK = 64
UNIT_ROWS = 64
CHUNKS_PER_STEP = 8
PROJ_TILE = 512
PROJ_TILE_SHORT = 256
MLP_TILE = 1024
FF_TILE = 2048

SEGMENTS = (
    ("q", D_MODEL, "conv", 0), ("k", D_MODEL, "conv", D_MODEL),
    ("r", RW_W, "shift", 0), ("kx", RW_W, "shift", RW_W),
    ("ga", D_MODEL, "raw", 0), ("gb", D_MODEL, "raw", 0),
    ("v", D_MODEL, "conv", 2 * D_MODEL), ("z", D_MODEL, "raw", 0),
    ("vx", RW_W, "shift", 2 * RW_W), ("lora", RW_LORA, "shift", 3 * RW_W),
)
COL = {}
P_W = 0
for _name, _width, _, _ in SEGMENTS:
    COL[_name] = P_W
    P_W += _width
P_HALF = P_W // 2
PASS0_CONV = 2 * D_MODEL
PASS0_SHIFT = 2 * RW_W
PIECE_W = 256


def _chunk_plan(half):
    lo, hi = half * P_HALF, (half + 1) * P_HALF
    plan = []
    for name, width, kind, off in SEGMENTS:
        start = COL[name]
        c = max(start, lo)
        while c < min(start + width, hi):
            w = min(PIECE_W, min(start + width, hi) - c)
            plan.append((c - lo, w, kind, off + (c - start), name))
            c += w
    return plan

VMEM_LIMIT = 56 * 1024 * 1024


def _sigmoid(x):
    return 1.0 / (1.0 + jnp.exp(-x))


def _silu(x):
    return x * _sigmoid(x)


def _softplus(x):
    return jnp.maximum(x, 0.0) + jnp.log(1.0 + jnp.exp(-jnp.abs(x)))


def _dot(a, b):
    return jnp.dot(a.astype(BF16), b.astype(BF16), preferred_element_type=F32)


def _dot_nt(a, b):
    return lax.dot_general(a.astype(BF16), b.astype(BF16), (((1,), (1,)), ((), ())),
                           preferred_element_type=F32)


def _dot_tn(a, b):
    return lax.dot_general(a.astype(BF16), b.astype(BF16), (((0,), (0,)), ((), ())),
                           preferred_element_type=F32)


def _rows(xs):
    return xs[0] if len(xs) == 1 else jnp.concatenate(xs, axis=0)


def _dot_rows(xs, b, dot=_dot):
    out = dot(_rows(xs), b)
    off, parts = 0, []
    for x in xs:
        parts.append(out[off:off + x.shape[0]])
        off += x.shape[0]
    return parts


def _dot_mask(mask_bf16, x):
    h1 = x.astype(BF16)
    r1 = x - h1.astype(F32)
    h2 = r1.astype(BF16)
    h3 = (r1 - h2.astype(F32)).astype(BF16)
    return (jnp.dot(mask_bf16, h1, preferred_element_type=F32)
            + (jnp.dot(mask_bf16, h2, preferred_element_type=F32)
               + jnp.dot(mask_bf16, h3, preferred_element_type=F32)))


def _iota2(n, m):
    return (lax.broadcasted_iota(jnp.int32, (n, m), 0),
            lax.broadcasted_iota(jnp.int32, (n, m), 1))


def _group(idx, size):
    return lax.shift_right_logical(idx, int(math.log2(size)))


def _column_packing(ru, tc):
    ri, ci = _iota2(ru, 2 * ru)
    cc = ci & (ru - 1)
    same = _group(ri, tc) == _group(cc, tc)
    first = lax.broadcasted_iota(jnp.int32, (1, 2 * ru), 1) < ru

    def pdot(x, y):
        y_bd = jnp.concatenate([jnp.where(first, y, 0.0), jnp.where(first, 0.0, y)], axis=0)
        return _dot(x, y_bd)

    return ri, cc, same, first, pdot


def _lockstep(chains):
    chains = list(chains)
    results = [None] * len(chains)
    active = list(range(len(chains)))
    while active:
        for i in list(active):
            try:
                next(chains[i])
            except StopIteration as done:
                results[i] = done.value
                active.remove(i)
        if active:
            yield
    return results


def _run(chain):
    for _ in chain:
        pass


def _neumann_inverse(a, eye, nil, dot):
    inv = eye - a
    if nil <= 2:
        return inv
    power = dot(a, a)
    yield
    k = 4
    while k < nil:
        step, power = _dot_rows([inv, power], power, dot=dot)
        yield
        inv = inv + step
        k *= 2
    step = dot(inv, power)
    yield
    return inv + step


def _unit_lower_inverse(a, ri, ci, tc, dot=_dot):
    eye = jnp.where(ri == ci, 1.0, 0.0).astype(F32)
    if tc <= INV_BLOCK:
        return (yield from _neumann_inverse(a, eye, tc, dot))
    assert tc // INV_BLOCK <= 4
    diag = _group(ri, INV_BLOCK) == _group(ci, INV_BLOCK)
    d = jnp.where(diag, a, 0.0)
    low = a - d
    dinv = yield from _neumann_inverse(d, eye, INV_BLOCK, dot)
    b = dot(low, dinv)
    yield
    db, bb = _dot_rows([dinv, b], b, dot=dot)
    yield
    p1 = dinv - db
    tail = dot(p1, bb)
    yield
    return p1 + tail


def _proj_kernel(x_ref, g_ref, wt_ref, wab_ref, alog_ref, dtb_ref, cw_ref, mu_ref, cb_ref, prev_ref,
                 p_ref, gb_ref, cout0_ref, shift0_ref, cout1_ref, shift1_ref, *maybe_hist,
                 tm, rc, sbp, ttp, ntp):
    carry = ntp > 1
    hist = maybe_hist[0] if carry else None
    j = pl.program_id(0)
    i = pl.program_id(1)
    tpos = lax.rem(i, ntp)
    x = x_ref[...]
    h = x * lax.rsqrt(jnp.mean(x * x, axis=-1, keepdims=True) + NORM_EPS) * g_ref[...]
    hb = h.astype(BF16)
    nt_dims = (((1,), (1,)), ((), ()))

    def mm(c0, cw):
        return lax.dot_general(hb, wt_ref[c0:c0 + cw, :], nt_dims, preferred_element_type=F32)

    seq_pos = lax.broadcasted_iota(jnp.int32, (tm, 1), 0) & (ttp - 1)

    def shift_rows(a, first):
        rolled = pltpu.roll(a, len(first), axis=0)
        if sbp == 1:
            head = rolled[0:SUBLANES]
            for i, f in enumerate(first):
                head = jnp.where(seq_pos[0:SUBLANES] == i, f[0], head)
            return jnp.concatenate([head, rolled[SUBLANES:]], axis=0)
        for i, f in enumerate(first):
            fill = jnp.broadcast_to(f, (sbp, ttp, a.shape[1])).reshape(tm, a.shape[1])
            rolled = jnp.where(seq_pos == i, fill, rolled)
        return rolled

    def run_pass(plan):
        if carry:
            @pl.when(tpos == 0)
            def _():
                hist[:, 0:SUBLANES, 0:DN_QKV] = cb_ref[...]
                hist[:, 0:SUBLANES, DN_QKV:] = prev_ref[...]

            @pl.when(tpos > 0)
            def _():
                hist[:, 0:SUBLANES, :] = hist[:, SUBLANES:, :]

        for c0, cw, kind, off, name in plan:
            cur = mm(c0, cw)
            if kind == "raw":
                p_ref[:, c0:c0 + cw] = cur
                continue
            cur3 = cur.reshape(sbp, ttp, cw)
            if carry:
                hc = slice(off, off + cw) if kind == "conv" else slice(DN_QKV + off, DN_QKV + off + cw)
                old = hist[:, 0:SUBLANES, hc]
                hist[:, SUBLANES:, hc] = cur3[:, ttp - SUBLANES:, :]
            elif kind == "conv":
                old = cb_ref[:, :, off:off + cw]
                out_ref, base = (cout0_ref, 0) if off < PASS0_CONV else (cout1_ref, PASS0_CONV)
                out_ref[:, :, off - base:off - base + cw] = cur3[:, ttp - 3:, :]
            else:
                old = prev_ref[:, :, off:off + cw]
                out_ref, base = (shift0_ref, 0) if off < PASS0_SHIFT else (shift1_ref, PASS0_SHIFT)
                out_ref[:, :, off - base:off - base + cw] = cur3[:, ttp - 1:, :]
            x1 = old[:, 7:8, :]
            if kind == "shift":
                prev = shift_rows(cur, [x1])
                p_ref[:, c0:c0 + cw] = cur + (prev - cur) * mu_ref[:, off:off + cw]
                continue
            w = [cw_ref[tap:tap + 1, off:off + cw] for tap in range(4)]
            x2, x3 = old[:, 6:7, :], old[:, 5:6, :]
            y1 = shift_rows(cur, [x1])
            far = cur * w[1] + y1 * w[0]
            acc = cur * w[3] + y1 * w[2] + shift_rows(far, [x2 * w[1] + x3 * w[0], x1 * w[1] + x2 * w[0]])
            y = _silu(acc)
            if name == "v":
                p_ref[:, c0:c0 + cw] = y
            else:
                scale = DN_D ** -0.5 if name == "q" else 1.0
                for c1 in range(0, cw, DN_D):
                    yh = y[:, c1:c1 + DN_D]
                    yh = yh * (lax.rsqrt(jnp.sum(yh * yh, axis=-1, keepdims=True) + 1e-6) * scale)
                    p_ref[:, c0 + c1:c0 + c1 + DN_D] = yh

    @pl.when(j == 0)
    def _():
        run_pass(_chunk_plan(0))
        for c in range(tm // rc):
            ab = lax.dot_general(wab_ref[...], hb[c * rc:(c + 1) * rc], nt_dims,
                                 preferred_element_type=F32)
            g = -jnp.exp(alog_ref[...]) * _softplus(ab[0:DN_HEADS] + dtb_ref[...])
            gb_ref[c, 0:DN_HEADS, :] = g
            gb_ref[c, DN_HEADS:2 * DN_HEADS, :] = _sigmoid(ab[DN_HEADS:2 * DN_HEADS])

        if carry:
            @pl.when(tpos == ntp - 1)
            def _():
                cout0_ref[...] = hist[:, 2 * SUBLANES - 3:, 0:PASS0_CONV]
                shift0_ref[...] = hist[:, 2 * SUBLANES - 1:, DN_QKV:DN_QKV + PASS0_SHIFT]

    @pl.when(j == 1)
    def _():
        run_pass(_chunk_plan(1))

        if carry:
            @pl.when(tpos == ntp - 1)
            def _():
                cout1_ref[...] = hist[:, 2 * SUBLANES - 3:, PASS0_CONV:DN_QKV]
                shift1_ref[...] = hist[:, 2 * SUBLANES - 1:, DN_QKV + PASS0_SHIFT:]


def _proj(x2d, g_norm, w_main_t, w_ab_t, a_log, dt_bias, conv_w, mu, conv_buf8, prev8, *, nb, seq, tm, rc):
    n = nb * seq
    ttp = min(seq, tm)
    sbp = tm // ttp
    ntp = seq // ttp
    assert n % tm == 0 and tm % rc == 0 and tm % ttp == 0 and seq % ttp == 0
    bcast = conv_buf8.shape[0] == 1
    assert bcast == (prev8.shape[0] == 1) and (not bcast or sbp == 1)
    last = (n // tm - 1) // ntp

    def st_idx(i):
        return 0 if bcast else i // ntp

    assert ttp % SUBLANES == 0 and ttp & (ttp - 1) == 0
    first_pass = lambda j, i: ((i // ntp) * (1 - j) + last * j, 0, 0)
    second_pass = lambda j, i: ((i // ntp) * j, 0, 0)
    p, gb, cout0, shift0, cout1, shift1 = pl.pallas_call(
        functools.partial(_proj_kernel, tm=tm, rc=rc, sbp=sbp, ttp=ttp, ntp=ntp),
        out_shape=(jax.ShapeDtypeStruct((n, P_W), F32),
                   jax.ShapeDtypeStruct((n // rc, 2 * DN_HEADS, rc), F32),
                   jax.ShapeDtypeStruct((nb, 3, PASS0_CONV), F32),
                   jax.ShapeDtypeStruct((nb, 1, PASS0_SHIFT), F32),
                   jax.ShapeDtypeStruct((nb, 3, DN_QKV - PASS0_CONV), F32),
                   jax.ShapeDtypeStruct((nb, 1, RW_SHIFT_W - PASS0_SHIFT), F32)),
        grid=(2, n // tm),
        in_specs=[
            pl.BlockSpec((tm, D_MODEL), lambda j, i: (i, 0)),
            pl.BlockSpec((1, D_MODEL), lambda j, i: (0, 0)),
            pl.BlockSpec((P_HALF, D_MODEL), lambda j, i: (j, 0)),
            pl.BlockSpec((2 * DN_HEADS, D_MODEL), lambda j, i: (0, 0)),
            pl.BlockSpec((DN_HEADS, 1), lambda j, i: (0, 0)),
            pl.BlockSpec((DN_HEADS, 1), lambda j, i: (0, 0)),
            pl.BlockSpec((4, DN_QKV), lambda j, i: (0, 0)),
            pl.BlockSpec((1, RW_SHIFT_W), lambda j, i: (0, 0)),
            pl.BlockSpec((sbp, SUBLANES, DN_QKV), lambda j, i: (st_idx(i), 0, 0)),
            pl.BlockSpec((sbp, SUBLANES, RW_SHIFT_W), lambda j, i: (st_idx(i), 0, 0)),
        ],
        out_specs=(
            pl.BlockSpec((tm, P_HALF), lambda j, i: (i, j)),
            pl.BlockSpec((tm // rc, 2 * DN_HEADS, rc),
                         lambda j, i: (i * (1 - j) + (n // tm - 1) * j, 0, 0)),
            pl.BlockSpec((sbp, 3, PASS0_CONV), first_pass),
            pl.BlockSpec((sbp, 1, PASS0_SHIFT), first_pass),
            pl.BlockSpec((sbp, 3, DN_QKV - PASS0_CONV), second_pass),
            pl.BlockSpec((sbp, 1, RW_SHIFT_W - PASS0_SHIFT), second_pass),
        ),
        scratch_shapes=[pltpu.VMEM((sbp, 2 * SUBLANES, DN_QKV + RW_SHIFT_W), F32)] if ntp > 1 else [],
        compiler_params=pltpu.CompilerParams(
            dimension_semantics=("arbitrary", "arbitrary"), vmem_limit_bytes=VMEM_LIMIT),
        name="proj",
    )(x2d, g_norm, w_main_t, w_ab_t, a_log, dt_bias, conv_w, mu, conv_buf8, prev8)
    conv_new = jnp.concatenate([cout0, cout1], axis=-1)
    shift_new = jnp.concatenate([shift0, shift1], axis=-1)
    return p, gb, conv_new, shift_new


def _dn_kernel(q_ref, k_ref, v_ref, z_ref, gb_ref, nw_ref, s0_ref, o_ref, sout_ref, s_scr, *, sb, tc, nc, nt):
    t = pl.program_id(1)
    ru = sb * tc
    hw = DN_HEADS * DN_D

    @pl.when(t == 0)
    def _():
        s_scr[...] = s0_ref[...]

    ri, ci = _iota2(ru, ru)
    same = _group(ri, tc) == _group(ci, tc)
    incl = same & (ri >= ci)
    eye = ri == ci
    rip, cc, same_p, first_cols, pdot = _column_packing(ru, tc)
    incl_p = same_p & (rip >= cc)
    strict_p = same_p & (rip > cc)
    seqs = [slice(s * tc, (s + 1) * tc) for s in range(sb)]
    units = [slice(c * ru, (c + 1) * ru) for c in range(nc)]

    def side_by_side(x0, x1):
        return jnp.concatenate([x0, x1], axis=1)

    def block_diag(x0, x1):
        z = jnp.zeros_like(x0)
        return jnp.concatenate([side_by_side(x0, z), side_by_side(z, x1)], axis=0)

    def unit_chain(heads, rows, q, k, v):
        per = []
        for h, kh in zip(heads, k):
            g_row = gb_ref[0, h:h + 1, :][:, rows]
            b_row = gb_ref[0, DN_HEADS + h:DN_HEADS + h + 1, :][:, rows]
            g_col = jnp.sum(jnp.where(incl, g_row, 0.0), axis=1, keepdims=True)
            g_cum_row = jnp.sum(jnp.where(eye, g_col, 0.0), axis=0, keepdims=True)
            g_tot = jnp.sum(jnp.where(same, g_row, 0.0), axis=1, keepdims=True)
            b_col = jnp.sum(jnp.where(eye, b_row, 0.0), axis=1, keepdims=True)
            per.append(dict(g_col=g_col, g_cum_row=g_cum_row, g_tot=g_tot, b_col=b_col,
                            e_g=jnp.exp(g_col), e_rest=jnp.exp(g_tot - g_col), kb=kh * b_col))
        g_cols = jnp.where(first_cols, per[0]["g_col"], per[1]["g_col"])
        g_rows = side_by_side(per[0]["g_cum_row"], per[1]["g_cum_row"])
        decay = jnp.where(incl_p, jnp.exp(jnp.minimum(g_cols - g_rows, 0.0)), 0.0)
        a_raw, qk_raw = _dot_rows([side_by_side(per[0]["kb"], per[1]["kb"]), side_by_side(q[0], q[1])],
                                  block_diag(k[0], k[1]), dot=_dot_nt)
        yield
        a = jnp.where(strict_p, a_raw * decay, 0.0)
        aqk = qk_raw * decay
        t_inv = yield from _unit_lower_inverse(a, rip, cc, tc, dot=pdot)
        rhs = [side_by_side(vh * w["b_col"], w["kb"] * w["e_g"]) for vh, w in zip(v, per)]
        sol = _dot(t_inv, block_diag(rhs[0], rhs[1]))
        yield
        out = []
        for i, w in enumerate(per):
            base = 2 * DN_D * i
            out.append(dict(u_all=sol[:, base:base + DN_D], w_all=sol[:, base + DN_D:base + 2 * DN_D],
                            qg=q[i] * w["e_g"], kd=k[i] * w["e_rest"], g_tot=w["g_tot"]))
        return aqk, out

    def pair_chain(pair):
        heads = (2 * pair, 2 * pair + 1)
        cols = [slice(h * DN_D, (h + 1) * DN_D) for h in heads]
        q = [q_ref[:, c] for c in cols]
        k = [k_ref[:, c] for c in cols]
        v = [v_ref[:, c] for c in cols]
        wy = yield from _lockstep(
            unit_chain(heads, rows, [x[rows] for x in q], [x[rows] for x in k], [x[rows] for x in v])
            for rows in units)
        for rows, (aqk, per) in zip(units, wy):
            ws = [[_dot(jnp.concatenate([w["w_all"][sq], w["qg"][sq]], axis=0), s_scr[s, h])
                   for s, sq in enumerate(seqs)] for h, w in zip(heads, per)]
            yield
            u = [_rows([w["u_all"][sq] - x[:tc] for sq, x in zip(seqs, wsh)]) for w, wsh in zip(per, ws)]
            q_s = [_rows([x[tc:] for x in wsh]) for wsh in ws]
            intra = _dot(aqk, block_diag(u[0], u[1]))
            upd = [[_dot_tn(w["kd"][sq], uh[sq]) for sq in seqs] for w, uh in zip(per, u)]
            yield
            for i, (h, w) in enumerate(zip(heads, per)):
                for s in range(sb):
                    gl = jnp.exp(w["g_tot"][s * tc:s * tc + 1, :])
                    s_scr[s, h] = s_scr[s, h] * gl + upd[i][s]
                o = q_s[i] + intra[:, i * DN_D:(i + 1) * DN_D]
                o = o * lax.rsqrt(jnp.mean(o * o, axis=-1, keepdims=True) + NORM_EPS) * nw_ref[...]
                o_ref[rows, cols[i]] = o * _silu(z_ref[rows, cols[i]])

    _run(_lockstep(pair_chain(pair) for pair in range(DN_HEADS // 2)))

    @pl.when(t == nt - 1)
    def _():
        sout_ref[...] = s_scr[...]


def _dn(p, gb, norm_w, s0, *, nb, seq, sb, tc, nc):
    nt = seq // (nc * tc)
    r = sb * nc * tc
    assert seq % (nc * tc) == 0 and nb % sb == 0 and (nt == 1 or sb == 1) and (nc == 1 or sb == 1)
    bcast = s0.shape[0] == 1
    assert not bcast or sb == 1
    hw = DN_HEADS * DN_D

    def st_idx(i):
        return 0 if bcast else i

    def rows(name):
        return pl.BlockSpec((r, hw), lambda i, t: (i * nt + t, COL[name] // hw))

    return pl.pallas_call(
        functools.partial(_dn_kernel, sb=sb, tc=tc, nc=nc, nt=nt),
        out_shape=(jax.ShapeDtypeStruct((nb * seq, hw), F32),
                   jax.ShapeDtypeStruct((nb, DN_HEADS, DN_D, DN_D), F32)),
        grid=(nb // sb, nt),
        in_specs=[
            rows("q"), rows("k"), rows("v"), rows("z"),
            pl.BlockSpec((1, 2 * DN_HEADS, r), lambda i, t: (i * nt + t, 0, 0)),
            pl.BlockSpec((1, DN_D), lambda i, t: (0, 0)),
            pl.BlockSpec((sb, DN_HEADS, DN_D, DN_D), lambda i, t: (st_idx(i), 0, 0, 0)),
        ],
        out_specs=(
            pl.BlockSpec((r, hw), lambda i, t: (i * nt + t, 0)),
            pl.BlockSpec((sb, DN_HEADS, DN_D, DN_D), lambda i, t: (i, 0, 0, 0)),
        ),
        scratch_shapes=[pltpu.VMEM((sb, DN_HEADS, DN_D, DN_D), F32)],
        compiler_params=pltpu.CompilerParams(
            dimension_semantics=("arbitrary", "arbitrary"), vmem_limit_bytes=VMEM_LIMIT),
        name="dn",
    )(p, p, p, p, gb, norm_w, s0)


def _rw_kernel(r_ref, k_ref, v_ref, l_ref, w0_ref, a0_ref, kkw_ref, kaw_ref, rkw_ref,
               gnw_ref, gnb_ref, w2_ref, a2_ref, g2_ref, s0_ref, odn_ref, ga_ref, gb_ref,
               o_ref, sout_ref, s_scr, *, sb, tc, nc, nt):
    t = pl.program_id(1)
    ru = sb * tc
    half = lax.broadcasted_iota(jnp.int32, (1, LANES), 1) < RW_HD
    ri128, ci128 = _iota2(LANES, LANES)
    bd_mask = _group(ri128, RW_HD) == _group(ci128, RW_HD)

    def seg_sum(x):
        first = jnp.sum(jnp.where(half, x, 0.0), axis=-1, keepdims=True)
        second = jnp.sum(jnp.where(half, 0.0, x), axis=-1, keepdims=True)
        return jnp.where(half, first, second)

    @pl.when(t == 0)
    def _():
        zero = jnp.zeros((RW_HD, RW_HD), F32)
        for s in range(sb):
            for p in range(RW_PAIRS):
                top = jnp.concatenate([s0_ref[s, 2 * p], zero], axis=1)
                bot = jnp.concatenate([zero, s0_ref[s, 2 * p + 1]], axis=1)
                s_scr[s, p] = jnp.concatenate([top, bot], axis=0)

    xm_l = l_ref[...]
    lora_w = jnp.tanh(xm_l).astype(BF16)
    lora_a = xm_l.astype(BF16)
    lora_g = _sigmoid(xm_l).astype(BF16)

    ri, ci = _iota2(ru, ru)
    cum_mask = jnp.where((_group(ri, tc) == _group(ci, tc)) & (ri >= ci), 1.0, 0.0).astype(BF16)
    rip, cc, same_p, _, pdot = _column_packing(ru, tc)
    incl_p = same_p & (rip >= cc)
    strict_p = same_p & (rip > cc)
    seqs = [slice(s * tc, (s + 1) * tc) for s in range(sb)]
    units = [slice(c * ru, (c + 1) * ru) for c in range(nc)]

    def stack(x):
        return jnp.concatenate([jnp.where(half, x, 0.0), jnp.where(half, 0.0, x)], axis=0)

    def unit_chain(rr, k2, vv, kk, bb, logw):
        lcum = _dot_mask(cum_mask, logw)
        yield
        ltot = jnp.broadcast_to(lcum.reshape(sb, tc, LANES)[:, tc - 1:tc, :],
                                (sb, tc, LANES)).reshape(ru, LANES)
        p_in = jnp.exp(lcum)
        p_inv = jnp.exp(-lcum)
        alpha = kk * jnp.exp(lcum - logw)
        beta = bb * p_inv
        kt = k2 * p_inv
        rt = rr * p_in
        e_rest = jnp.exp(ltot - lcum)
        beta2, kt2, v2 = stack(beta), stack(kt), stack(vv)
        l_raw, rb_raw = _dot_rows([alpha, rt], beta2, dot=_dot_nt)
        lk_raw, rk_raw = _dot_rows([alpha, rt], kt2, dot=_dot_nt)
        yield
        lmat = jnp.where(strict_p, l_raw, 0.0)
        rb = jnp.where(incl_p, rb_raw, 0.0)
        lk = jnp.where(strict_p, lk_raw, 0.0)
        rk = jnp.where(incl_p, rk_raw, 0.0)
        lkv, rkv = _dot_rows([lk, rk], v2)
        t_inv = yield from _unit_lower_inverse(lmat, rip, cc, tc, dot=pdot)
        sol = _dot(t_inv, jnp.concatenate([stack(alpha), stack(lkv)], axis=1))
        yield
        return dict(wa=sol[:, :LANES], uv=sol[:, LANES:], rt=rt, rb=rb, rkv=rkv, p_in=p_in,
                    k_end=k2 * e_rest, b_end=bb * e_rest)

    def pair_chain(p):
        c = slice(p * LANES, (p + 1) * LANES)
        rr = r_ref[:, c]
        xk = k_ref[:, c]
        vv = v_ref[:, c]
        lw = jnp.dot(lora_w, w2_ref[:, c], preferred_element_type=F32)
        la = jnp.dot(lora_a, a2_ref[:, c], preferred_element_type=F32)
        gate = jnp.dot(lora_g, g2_ref[:, c], preferred_element_type=F32)
        yield
        logw = -math.exp(-0.5) * _sigmoid(w0_ref[:, c] + lw)
        aa = _sigmoid(a0_ref[:, c] + la)
        kk = xk * kkw_ref[:, c]
        k2 = xk * (1.0 + (aa - 1.0) * kaw_ref[:, c])
        kk = kk * lax.rsqrt(seg_sum(kk * kk) + 1e-6)
        bb = kk * aa
        wy = yield from _lockstep(
            unit_chain(rr[rows], k2[rows], vv[rows], kk[rows], bb[rows], logw[rows]) for rows in units)
        for rows, w in zip(units, wy):
            st = [_dot_nt(jnp.concatenate([w["wa"][sq], w["rt"][sq]], axis=0), s_scr[s, p])
                  for s, sq in enumerate(seqs)]
            yield
            u = w["uv"] + _rows([x[:tc] for x in st])
            r_state = _rows([x[tc:] for x in st])
            rbu = _dot(w["rb"], stack(u))
            v_u = vv[rows]
            upd = [_dot_tn(jnp.concatenate([v_u[sq], -u[sq]], axis=0),
                           jnp.concatenate([w["k_end"][sq], w["b_end"][sq]], axis=0)) for sq in seqs]
            yield
            for s in range(sb):
                p_end = w["p_in"][(s + 1) * tc - 1:(s + 1) * tc, :]
                s_scr[s, p] = s_scr[s, p] * p_end + jnp.where(bd_mask, upd[s], 0.0)
            o = r_state + (w["rkv"] - rbu)
            mean = seg_sum(o) * (1.0 / RW_HD)
            d = o - mean
            var = seg_sum(d * d) * (1.0 / RW_HD)
            bonus = seg_sum(rr[rows] * k2[rows] * rkw_ref[:, c])
            on = d * lax.rsqrt(var + RW_GN_EPS) * gnw_ref[:, c] + gnb_ref[:, c]
            o_rw = (on + bonus * v_u) * gate[rows]
            mix = _sigmoid(ga_ref[rows, c]) * odn_ref[rows, c] + _sigmoid(gb_ref[rows, c]) * o_rw
            o_ref[rows, c] = mix.astype(BF16)

    _run(_lockstep(pair_chain(p) for p in range(RW_PAIRS)))

    @pl.when(t == nt - 1)
    def _():
        for s in range(sb):
            for p in range(RW_PAIRS):
                sp = s_scr[s, p]
                sout_ref[s, 2 * p] = sp[0:RW_HD, 0:RW_HD]
                sout_ref[s, 2 * p + 1] = sp[RW_HD:, RW_HD:]


def _rw(p, o_dn, weights, s0, *, nb, seq, sb, tc, nc):
    nt = seq // (nc * tc)
    r = sb * nc * tc
    assert seq % (nc * tc) == 0 and nb % sb == 0 and (nt == 1 or sb == 1) and (nc == 1 or sb == 1)
    bcast = s0.shape[0] == 1
    assert not bcast or sb == 1
    (w0, a0, kkw, kaw, rkw, gnw, gnb, w2p, a2p, g2p) = weights

    def st_idx(i):
        return 0 if bcast else i

    def vec(width):
        return pl.BlockSpec((1, width), lambda i, t: (0, 0))

    def lora_w():
        return pl.BlockSpec((RW_LORA, RW_W), lambda i, t: (0, 0))

    def rows(col):
        return pl.BlockSpec((r, RW_W), lambda i, t: (i * nt + t, col))

    def named(name):
        return rows(COL[name] // RW_W)

    return pl.pallas_call(
        functools.partial(_rw_kernel, sb=sb, tc=tc, nc=nc, nt=nt),
        out_shape=(jax.ShapeDtypeStruct((nb * seq, RW_W), BF16),
                   jax.ShapeDtypeStruct((nb, RW_HEADS, RW_HD, RW_HD), F32)),
        grid=(nb // sb, nt),
        in_specs=[
            named("r"), named("kx"), named("vx"),
            pl.BlockSpec((r, RW_LORA), lambda i, t: (i * nt + t, COL["lora"] // RW_LORA)),
            vec(RW_W), vec(RW_W), vec(RW_W), vec(RW_W), vec(RW_W), vec(RW_W), vec(RW_W),
            lora_w(), lora_w(), lora_w(),
            pl.BlockSpec((sb, RW_HEADS, RW_HD, RW_HD), lambda i, t: (st_idx(i), 0, 0, 0)),
            rows(0), named("ga"), named("gb"),
        ],
        out_specs=(
            rows(0),
            pl.BlockSpec((sb, RW_HEADS, RW_HD, RW_HD), lambda i, t: (i, 0, 0, 0)),
        ),
        scratch_shapes=[pltpu.VMEM((sb, RW_PAIRS, LANES, LANES), F32)],
        compiler_params=pltpu.CompilerParams(
            dimension_semantics=("arbitrary", "arbitrary"), vmem_limit_bytes=VMEM_LIMIT),
        name="rw",
    )(p, p, p, p, w0, a0, kkw, kaw, rkw, gnw, gnb, w2p, a2p, g2p, s0, o_dn, p, p)


def _merge_kernel(x_ref, mix_ref, wout_ref, gffn_ref, w1_ref, w2_ref, gfin_ref,
                  y_ref, x1_scr, h2_scr, acc_scr, *, nf):
    f = pl.program_id(1)

    @pl.when(f == 0)
    def _():
        x1 = x_ref[...] + jnp.dot(mix_ref[...], wout_ref[...], preferred_element_type=F32)
        x1_scr[...] = x1
        h2 = x1 * lax.rsqrt(jnp.mean(x1 * x1, axis=-1, keepdims=True) + NORM_EPS) * gffn_ref[...]
        h2_scr[...] = h2.astype(BF16)
        acc_scr[...] = jnp.zeros_like(acc_scr)

    hid = jnp.dot(h2_scr[...], w1_ref[...], preferred_element_type=F32)
    hid = jnp.square(jnp.maximum(hid, 0.0))
    acc_scr[...] += jnp.dot(hid.astype(BF16), w2_ref[...], preferred_element_type=F32)

    @pl.when(f == nf - 1)
    def _():
        y = x1_scr[...] + acc_scr[...]
        y_ref[...] = y * lax.rsqrt(jnp.mean(y * y, axis=-1, keepdims=True) + NORM_EPS) * gfin_ref[...]


def _merge(x2d, mix, w_out, g_ffn, w1, w2, g_fin, *, tm, tf):
    n = x2d.shape[0]
    nf = D_FF // tf
    assert n % tm == 0 and D_FF % tf == 0
    row = lambda i, f: (i, 0)
    return pl.pallas_call(
        functools.partial(_merge_kernel, nf=nf),
        out_shape=jax.ShapeDtypeStruct((n, D_MODEL), F32),
        grid=(n // tm, nf),
        in_specs=[
            pl.BlockSpec((tm, D_MODEL), row),
            pl.BlockSpec((tm, D_MODEL), row),
            pl.BlockSpec((D_MODEL, D_MODEL), lambda i, f: (0, 0)),
            pl.BlockSpec((1, D_MODEL), lambda i, f: (0, 0)),
            pl.BlockSpec((D_MODEL, tf), lambda i, f: (0, f)),
            pl.BlockSpec((tf, D_MODEL), lambda i, f: (f, 0)),
            pl.BlockSpec((1, D_MODEL), lambda i, f: (0, 0)),
        ],
        out_specs=pl.BlockSpec((tm, D_MODEL), row),
        scratch_shapes=[
            pltpu.VMEM((tm, D_MODEL), F32),
            pltpu.VMEM((tm, D_MODEL), BF16),
            pltpu.VMEM((tm, D_MODEL), F32),
        ],
        compiler_params=pltpu.CompilerParams(
            dimension_semantics=("arbitrary", "arbitrary"), vmem_limit_bytes=VMEM_LIMIT),
        name="merge",
    )(x2d, mix, w_out, g_ffn, w1, w2, g_fin)


def _pad_rows_front(x, rows):
    b, n, w = x.shape
    return jnp.concatenate([jnp.zeros((b, rows - n, w), x.dtype), x], axis=1)


def _layer(x, conv_buf, dn_s, rw_prev, rw_s, wts, *, sb, tc, nc, tm, tm_mlp):
    nb, seq, _ = x.shape
    n = nb * seq
    x2d = x.reshape(n, D_MODEL)
    p, gb, conv_new, shift_new = _proj(
        x2d, wts["g_mix"], wts["w_main_t"], wts["w_ab_t"], wts["a_log"], wts["dt_bias"], wts["conv_w"],
        wts["rw_mu"], _pad_rows_front(conv_buf, SUBLANES), _pad_rows_front(rw_prev[:, None, :], SUBLANES),
        nb=nb, seq=seq, tm=tm, rc=sb * nc * tc)
    o_dn, dn_new = _dn(p, gb, wts["dn_norm_w"], dn_s, nb=nb, seq=seq, sb=sb, tc=tc, nc=nc)
    mix, rw_new = _rw(p, o_dn, wts["rw"], rw_s, nb=nb, seq=seq, sb=sb, tc=tc, nc=nc)
    y = _merge(x2d, mix, wts["w_out"], wts["g_ffn"], wts["w_ff1"], wts["w_ff2"], wts["g_final"],
               tm=tm_mlp, tf=FF_TILE)
    return y.reshape(nb, seq, D_MODEL), conv_new, dn_new, shift_new[:, 0], rw_new


def kernel(x_prompt, x_sample, state_dn_conv, state_dn, state_rw_shift, state_rw, meta_tokens, g_mix_norm, w_in, dn_conv_w, dn_a_log, dn_dt_bias, dn_norm_w, rw_mu, rw_w0, rw_w2, rw_a0, rw_a2, rw_g2, rw_k_k, rw_k_a, rw_r_k, rw_gn_w, rw_gn_b, w_out, g_ffn_norm, w_ff1, w_ff2, g_final):
    assert g_mix_norm.shape[0] == 1, "single layer"
    w_t = w_in[0].T
    o_a = DN_QKV + D_MODEL
    o_rwp = o_a + 2 * DN_HEADS
    o_gate = o_rwp + RW_SHIFT_W
    src = {"q": 0, "k": D_MODEL, "v": 2 * D_MODEL, "z": DN_QKV, "r": o_rwp, "kx": o_rwp + RW_W,
           "vx": o_rwp + 2 * RW_W, "lora": o_rwp + 3 * RW_W, "ga": o_gate, "gb": o_gate + D_MODEL}
    w_main_t = jnp.concatenate([w_t[src[name]:src[name] + width] for name, width, _, _ in SEGMENTS],
                               axis=0).astype(BF16)
    w_ab_t = w_t[o_a:o_rwp].astype(BF16)

    def lora_pad(wl, lo):
        return jnp.zeros((RW_LORA, RW_W), F32).at[lo:lo + wl.shape[0]].set(wl).astype(BF16)

    row = lambda v: v.reshape(1, -1).astype(F32)
    wts = {
        "g_mix": row(g_mix_norm[0]), "w_main_t": w_main_t, "w_ab_t": w_ab_t,
        "a_log": dn_a_log[0].reshape(DN_HEADS, 1), "dt_bias": dn_dt_bias[0].reshape(DN_HEADS, 1),
        "conv_w": dn_conv_w[0], "dn_norm_w": row(dn_norm_w[0]), "rw_mu": row(rw_mu[0]),
        "rw": (row(rw_w0[0]), row(rw_a0[0]), row(rw_k_k[0]), row(rw_k_a[0]), row(rw_r_k[0]),
               row(rw_gn_w[0]), row(rw_gn_b[0]),
               lora_pad(rw_w2[0], 0), lora_pad(rw_a2[0], 64), lora_pad(rw_g2[0], 128)),
        "w_out": w_out[0].astype(BF16), "g_ffn": row(g_ffn_norm[0]),
        "w_ff1": w_ff1[0].astype(BF16), "w_ff2": w_ff2[0].astype(BF16), "g_final": row(g_final),
    }

    nbm = UNIT_ROWS // N_META
    xm = jnp.broadcast_to(meta_tokens.astype(F32)[None], (nbm, N_META, D_MODEL))
    _, conv_m, dn_m, shift_m, rw_m = _layer(
        xm, jnp.zeros((nbm, 3, DN_QKV), F32), jnp.zeros((nbm, DN_HEADS, DN_D, DN_D), F32),
        jnp.zeros((nbm, RW_SHIFT_W), F32), jnp.zeros((nbm, RW_HEADS, RW_HD, RW_HD), F32), wts,
        sb=nbm, tc=N_META, nc=1, tm=nbm * N_META, tm_mlp=nbm * N_META)

    y_p, conv_p, dn_p, shift_p, rw_p = _layer(
        x_prompt, conv_m[:1], dn_m[:1], shift_m[:1], rw_m[:1], wts,
        sb=1, tc=CHUNK, nc=CHUNKS_PER_STEP, tm=PROJ_TILE, tm_mlp=MLP_TILE)

    dec_len = x_sample.shape[1]
    y_s, conv_s, dn_s, shift_s, rw_s = _layer(
        x_sample, state_dn_conv[0], state_dn[0], state_rw_shift[0], state_rw[0], wts,
        sb=UNIT_ROWS // dec_len, tc=dec_len, nc=1, tm=PROJ_TILE_SHORT, tm_mlp=MLP_TILE)

    return (y_p, y_s, conv_p[None], dn_p[None], shift_p[None], rw_p[None],
            conv_s[None], dn_s[None], shift_s[None], rw_s[None])
```
